```python
import math
import jax
import jax.numpy as jnp
from jax import lax
import numpy as np

D_MODEL = 1024
BATCH = 4
SEQ = 4096
DEPTH = 2

HEAD_DIM = 64
FOX_HEADS = 8
FOX_WIDTH = FOX_HEADS * HEAD_DIM
DIFF_HEADS = 4
DIFF_QK_WIDTH = DIFF_HEADS * 2 * HEAD_DIM
DIFF_V_DIM = 2 * HEAD_DIM
DIFF_WIDTH = DIFF_HEADS * DIFF_V_DIM
CONV_WIDTH = D_MODEL // 2
CONV_K = 3
N_BRANCH = 3
ROPE_THETA = 500000.0
ROT_DIM = HEAD_DIM // 4
Q_BLOCK = 128
RMS_EPS = 1e-6
MAX_POS_OFFSET = 1024
SPLIT_SIZES = (FOX_WIDTH, FOX_WIDTH, FOX_WIDTH, FOX_HEADS, FOX_WIDTH,
               DIFF_QK_WIDTH, DIFF_QK_WIDTH, DIFF_WIDTH, DIFF_WIDTH,
               CONV_WIDTH, CONV_WIDTH, CONV_WIDTH, CONV_WIDTH,
               N_BRANCH * D_MODEL)
IN_WIDTH = (4 * FOX_WIDTH + FOX_HEADS + 2 * DIFF_QK_WIDTH + 2 * DIFF_WIDTH
            + 4 * CONV_WIDTH + N_BRANCH * D_MODEL)

kernel_name = 'hybrid_fox_diffattn_shortconv_gated_merge'


def _rmsnorm(x, g):
    xf = x.astype(jnp.float32)
    y = xf * lax.rsqrt(jnp.mean(xf * xf, axis=-1, keepdims=True) + RMS_EPS)
    return (y * g.astype(jnp.float32)).astype(x.dtype)


def _split_cols(p):
    idx, acc = [], 0
    for s in SPLIT_SIZES[:-1]:
        acc += s
        idx.append(acc)
    return jnp.split(p, idx, axis=-1)


def _rope_tables(positions):
    inv_freq = ROPE_THETA ** (-jnp.arange(0, ROT_DIM, 2, dtype=jnp.float32) / ROT_DIM)
    ang = positions.astype(jnp.float32)[..., None] * inv_freq
    return jnp.cos(ang), jnp.sin(ang)


def _partial_rotary(x, cos, sin):
    half = ROT_DIM // 2
    x1 = x[..., :half].astype(jnp.float32)
    x2 = x[..., half:ROT_DIM].astype(jnp.float32)
    r1 = (x1 * cos - x2 * sin).astype(x.dtype)
    r2 = (x2 * cos + x1 * sin).astype(x.dtype)
    return jnp.concatenate([r1, r2, x[..., ROT_DIM:]], axis=-1)


def _causal_mask(s0, s1):
    q_idx = jnp.arange(s0, s1)[:, None]
    k_idx = jnp.arange(s1)[None, :]
    return k_idx <= q_idx


def _fox_attention(q, k, v, log_f):
    b, s, h, d = q.shape
    q = q.transpose(0, 2, 1, 3)
    k = k.transpose(0, 2, 1, 3)
    v = v.transpose(0, 2, 1, 3)
    c = jnp.cumsum(log_f, axis=1).transpose(0, 2, 1)
    scale = HEAD_DIM ** -0.5
    outs = []
    for s0 in range(0, s, Q_BLOCK):
        s1 = s0 + Q_BLOCK
        logits = jnp.einsum('bhqd,bhkd->bhqk', q[:, :, s0:s1], k[:, :, :s1]).astype(jnp.float32) * scale
        logits = logits + c[:, :, s0:s1, None] - c[:, :, None, :s1]
        p = jax.nn.softmax(jnp.where(_causal_mask(s0, s1), logits, -jnp.inf), axis=-1)
        outs.append(jnp.einsum('bhqk,bhkd->bhqd', p.astype(v.dtype), v[:, :, :s1]))
    o = jnp.concatenate(outs, axis=2)
    return o.transpose(0, 2, 1, 3).reshape(b, s, h * d)


def _diff_attention(q, k, v, lam, lam_init, norm_g):
    b, s, h, _, d = q.shape
    q = q.transpose(0, 2, 3, 1, 4)
    k = k.transpose(0, 2, 3, 1, 4)
    v = v.transpose(0, 2, 1, 3)
    scale = HEAD_DIM ** -0.5
    outs = []
    for s0 in range(0, s, Q_BLOCK):
        s1 = s0 + Q_BLOCK
        logits = jnp.einsum('bhcqd,bhckd->bhcqk', q[:, :, :, s0:s1], k[:, :, :, :s1]).astype(jnp.float32) * scale
        p = jax.nn.softmax(jnp.where(_causal_mask(s0, s1), logits, -jnp.inf), axis=-1)
        a = p[:, :, 0] - lam * p[:, :, 1]
        outs.append(jnp.einsum('bhqk,bhkv->bhqv', a.astype(v.dtype), v[:, :, :s1]))
    o = jnp.concatenate(outs, axis=2)
    o = _rmsnorm(o, norm_g) * (1.0 - lam_init)
    return o.transpose(0, 2, 1, 3).reshape(b, s, h * DIFF_V_DIM)


def _short_conv(u, w):
    rhs = w.astype(u.dtype)[:, None, :]
    return lax.conv_general_dilated(
        u, rhs, window_strides=(1,), padding=[(CONV_K - 1, 0)],
        dimension_numbers=('NWC', 'WIO', 'NWC'), feature_group_count=u.shape[-1])


def _layer(x, cos, sin, layer, pre_g, w_in, b_f, b_m, conv_w, lq1, lk1, lq2, lk2,
           diff_g, w_fox, w_diff, w_conv, w_o, post_g):
    b, s, _ = x.shape
    h = _rmsnorm(x, pre_g)
    p = h @ w_in
    (fq, fk, fv, ff, fg, dq, dk, dv, dg, cb, cc, cx, cg, mg) = _split_cols(p)

    log_f = jax.nn.log_sigmoid((ff + b_f).astype(jnp.float32))
    y_fox = _fox_attention(fq.reshape(b, s, FOX_HEADS, HEAD_DIM),
                           fk.reshape(b, s, FOX_HEADS, HEAD_DIM),
                           fv.reshape(b, s, FOX_HEADS, HEAD_DIM), log_f)
    y_fox = y_fox * jax.nn.silu(fg)

    lam_init = 0.8 - 0.6 * math.exp(-0.3 * layer)
    f32 = jnp.float32
    lam = (jnp.exp(jnp.sum(lq1.astype(f32) * lk1.astype(f32)))
           - jnp.exp(jnp.sum(lq2.astype(f32) * lk2.astype(f32))) + lam_init)
    cs, sn = cos[:, :, None, None, :], sin[:, :, None, None, :]
    q_d = _partial_rotary(dq.reshape(b, s, DIFF_HEADS, 2, HEAD_DIM), cs, sn)
    k_d = _partial_rotary(dk.reshape(b, s, DIFF_HEADS, 2, HEAD_DIM), cs, sn)
    y_diff = _diff_attention(q_d, k_d, dv.reshape(b, s, DIFF_HEADS, DIFF_V_DIM), lam, lam_init, diff_g)
    y_diff = y_diff * jax.nn.silu(dg)

    y_conv = cb * _short_conv(cc * cx, conv_w)
    y_conv = y_conv * jax.nn.silu(cg)

    gates = jax.nn.sigmoid(mg + b_m).reshape(b, s, N_BRANCH, D_MODEL)
    m = (gates[:, :, 0] * (y_fox @ w_fox)
         + gates[:, :, 1] * (y_diff @ w_diff)
         + gates[:, :, 2] * (y_conv @ w_conv))
    o = m @ w_o
    return x + _rmsnorm(o, post_g)


def setup_inputs(seed: int = 0) -> dict:
    key = jax.random.key(seed)
    ks = jax.random.split(key, 20)
    f32 = jnp.float32

    def nrm(k, shape, scale):
        return jax.random.normal(k, shape, f32) * scale

    x = jax.random.normal(ks[0], (BATCH, SEQ, D_MODEL), f32)
    offset = jax.random.randint(ks[1], (BATCH, 1), 0, MAX_POS_OFFSET, dtype=jnp.int32)
    positions = offset + jnp.arange(SEQ, dtype=jnp.int32)[None, :]
    pre_norm_g = 1.0 + nrm(ks[2], (DEPTH, D_MODEL), 0.02)
    w_in = nrm(ks[3], (DEPTH, D_MODEL, IN_WIDTH), D_MODEL ** -0.5)
    b_forget = jax.random.uniform(ks[4], (DEPTH, FOX_HEADS), f32, 1.0, 6.0)
    b_merge = nrm(ks[5], (DEPTH, N_BRANCH * D_MODEL), 0.01)
    conv_w = nrm(ks[6], (DEPTH, CONV_K, CONV_WIDTH), CONV_K ** -0.5)
    lam_q1 = nrm(ks[7], (DEPTH, HEAD_DIM), 0.1)
    lam_k1 = nrm(ks[8], (DEPTH, HEAD_DIM), 0.1)
    lam_q2 = nrm(ks[9], (DEPTH, HEAD_DIM), 0.1)
    lam_k2 = nrm(ks[10], (DEPTH, HEAD_DIM), 0.1)
    diff_norm_g = 1.0 + nrm(ks[11], (DEPTH, DIFF_V_DIM), 0.02)
    w_br_fox = nrm(ks[12], (DEPTH, FOX_WIDTH, D_MODEL), FOX_WIDTH ** -0.5)
    w_br_diff = nrm(ks[13], (DEPTH, DIFF_WIDTH, D_MODEL), DIFF_WIDTH ** -0.5)
    w_br_conv = nrm(ks[14], (DEPTH, CONV_WIDTH, D_MODEL), CONV_WIDTH ** -0.5)
    w_out = nrm(ks[15], (DEPTH, D_MODEL, D_MODEL), D_MODEL ** -0.5)
    post_norm_g = 1.0 + nrm(ks[16], (DEPTH, D_MODEL), 0.02)
    return {'x': x, 'positions': positions, 'pre_norm_g': pre_norm_g, 'w_in': w_in,
            'b_forget': b_forget, 'b_merge': b_merge, 'conv_w': conv_w,
            'lam_q1': lam_q1, 'lam_k1': lam_k1, 'lam_q2': lam_q2, 'lam_k2': lam_k2,
            'diff_norm_g': diff_norm_g, 'w_br_fox': w_br_fox, 'w_br_diff': w_br_diff,
            'w_br_conv': w_br_conv, 'w_out': w_out, 'post_norm_g': post_norm_g}


def reference(x, positions, pre_norm_g, w_in, b_forget, b_merge, conv_w, lam_q1, lam_k1,
              lam_q2, lam_k2, diff_norm_g, w_br_fox, w_br_diff, w_br_conv, w_out, post_norm_g):
    cos, sin = _rope_tables(positions)
    for l in range(DEPTH):
        x = _layer(x, cos, sin, l, pre_norm_g[l], w_in[l], b_forget[l], b_merge[l], conv_w[l],
                   lam_q1[l], lam_k1[l], lam_q2[l], lam_k2[l], diff_norm_g[l],
                   w_br_fox[l], w_br_diff[l], w_br_conv[l], w_out[l], post_norm_g[l])
    return x
```

```python
import functools
import math

import jax
import jax.numpy as jnp
from jax import lax
from jax.experimental import pallas as pl
from jax.experimental.pallas import tpu as pltpu

F32 = jnp.float32
BF16 = jnp.bfloat16

D_MODEL = 1024
HEAD_DIM = 64
FOX_HEADS = 8
DIFF_HEADS = 4
WIDTH = 512
CONV_K = 3
N_BRANCH = 3
ROPE_THETA = 500000.0
ROT_DIM = HEAD_DIM // 4
RMS_EPS = 1e-6
LANES = 128
SUBLANES = 8

(COL_FQ, COL_FK, COL_FV, COL_FG, COL_DQ, COL_DK, COL_DV, COL_DG,
 COL_CB, COL_CC, COL_CX, COL_CG, COL_MG) = range(13)
N_COL_BLOCKS = COL_MG + N_BRANCH * D_MODEL // WIDTH
P_WIDTH = N_COL_BLOCKS * WIDTH

_OFF_FQ, _OFF_FK, _OFF_FV = 0, 512, 1024
_OFF_FF = 1536
_OFF_FG = 1544
_OFF_DQ, _OFF_DK, _OFF_DV, _OFF_DG = 2056, 2568, 3080, 3592
_OFF_CB, _OFF_CC, _OFF_CX, _OFF_CG = 4104, 4616, 5128, 5640
_OFF_MG = 6152

TM_IN = 1024
TS_FF = 512
TQ = 512
TK = 512
TM_OUT = 512
VMEM_LIMIT = 48 * 1024 * 1024


def _silu(v):
    return v * jax.nn.sigmoid(v)


def _rotary_block(acc, c, s1, s2):
    outs = []
    for i in range(WIDTH // LANES):
        v = acc[:, i * LANES:(i + 1) * LANES]
        outs.append(v * c + pltpu.roll(v, LANES - ROT_DIM // 2, 1) * s1
                    + pltpu.roll(v, ROT_DIM // 2, 1) * s2)
    return jnp.concatenate(outs, axis=1)


def _in_proj_kernel(x_ref, g_ref, w_ref, bm_ref, rc_ref, rs1_ref, rs2_ref, p_ref, h_ref, h_scr):
    j = pl.program_id(1)

    @pl.when(j == 0)
    def _():
        x = x_ref[...]
        ms = jnp.mean(x * x, axis=-1, keepdims=True)
        h = (x * lax.rsqrt(ms + RMS_EPS) * g_ref[...]).astype(BF16)
        h_scr[...] = h
        h_ref[...] = h

    acc = jnp.dot(h_scr[...], w_ref[...], preferred_element_type=F32)
    scale = HEAD_DIM ** -0.5

    is_silu = (j == COL_FG) | (j == COL_DG) | (j == COL_CG)
    is_plain = ((j == COL_FK) | (j == COL_FV) | (j == COL_DV) | (j == COL_CB)
                | (j == COL_CC) | (j == COL_CX))

    @pl.when(is_plain)
    def _():
        p_ref[...] = acc.astype(BF16)

    @pl.when(j == COL_FQ)
    def _():
        p_ref[...] = (acc * scale).astype(BF16)

    @pl.when(is_silu)
    def _():
        p_ref[...] = _silu(acc).astype(BF16)

    @pl.when(j == COL_DQ)
    def _():
        r = _rotary_block(acc, rc_ref[...], rs1_ref[...], rs2_ref[...])
        p_ref[...] = (r * scale).astype(BF16)

    @pl.when(j == COL_DK)
    def _():
        r = _rotary_block(acc, rc_ref[...], rs1_ref[...], rs2_ref[...])
        p_ref[...] = r.astype(BF16)

    @pl.when(j >= COL_MG)
    def _():
        p_ref[...] = jax.nn.sigmoid(acc + bm_ref[...]).astype(BF16)


def _in_proj(x2, pre_g, w_main, b_merge, rc, rs1, rs2):
    t = x2.shape[0]
    grid = (t // TM_IN, N_COL_BLOCKS)
    return pl.pallas_call(
        _in_proj_kernel,
        grid=grid,
        in_specs=[
            pl.BlockSpec((TM_IN, D_MODEL), lambda i, j: (i, 0)),
            pl.BlockSpec((1, D_MODEL), lambda i, j: (0, 0)),
            pl.BlockSpec((D_MODEL, WIDTH), lambda i, j: (0, j)),
            pl.BlockSpec((1, WIDTH), lambda i, j: (0, jnp.maximum(j - COL_MG, 0))),
            pl.BlockSpec((TM_IN, LANES), lambda i, j: (i, 0)),
            pl.BlockSpec((TM_IN, LANES), lambda i, j: (i, 0)),
            pl.BlockSpec((TM_IN, LANES), lambda i, j: (i, 0)),
        ],
        out_specs=[
            pl.BlockSpec((TM_IN, WIDTH), lambda i, j: (i, j)),
            pl.BlockSpec((TM_IN, D_MODEL), lambda i, j: (i, 0)),
        ],
        out_shape=[
            jax.ShapeDtypeStruct((t, P_WIDTH), BF16),
            jax.ShapeDtypeStruct((t, D_MODEL), BF16),
        ],
        scratch_shapes=[pltpu.VMEM((TM_IN, D_MODEL), BF16)],
        compiler_params=pltpu.CompilerParams(
            dimension_semantics=("arbitrary", "arbitrary"), vmem_limit_bytes=VMEM_LIMIT),
        name="in_proj",
    )(x2, pre_g, w_main, b_merge, rc, rs1, rs2)


def _split3(v):
    hi = v.astype(BF16)
    r1 = v - hi.astype(F32)
    mid = r1.astype(BF16)
    lo = (r1 - mid.astype(F32)).astype(BF16)
    return hi, mid, lo


def _forget_kernel(h_ref, wt_ref, bf_ref, crow_ref, ccol_ref, carry_scr):
    si = pl.program_id(1)

    @pl.when(si == 0)
    def _():
        carry_scr[...] = jnp.zeros_like(carry_scr)

    ff = lax.dot_general(wt_ref[...], h_ref[...], (((1,), (1,)), ((), ())),
                         preferred_element_type=F32)
    lf = jax.nn.log_sigmoid(ff + bf_ref[...])
    row = lax.broadcasted_iota(jnp.int32, (TS_FF, TS_FF), 0)
    col = lax.broadcasted_iota(jnp.int32, (TS_FF, TS_FF), 1)
    upper = (row <= col).astype(BF16)
    cum = jnp.zeros((FOX_HEADS, TS_FF), F32)
    for piece in _split3(lf):
        cum = cum + jnp.dot(piece, upper, preferred_element_type=F32)
    c = cum + carry_scr[:, 0:1]
    carry_scr[...] = jnp.broadcast_to(c[:, TS_FF - 1:TS_FF], carry_scr.shape)
    crow_ref[0, 0] = c
    padded = jnp.concatenate([c, jnp.zeros((LANES - FOX_HEADS, TS_FF), F32)], axis=0)
    ccol_ref[...] = padded.T


def _forget_cumsum(h, w_ff_t, b_f, batch, seq):
    nt = seq // TS_FF
    return pl.pallas_call(
        _forget_kernel,
        grid=(batch, nt),
        in_specs=[
            pl.BlockSpec((TS_FF, D_MODEL), lambda b, s: (b * nt + s, 0)),
            pl.BlockSpec((FOX_HEADS, D_MODEL), lambda b, s: (0, 0)),
            pl.BlockSpec((FOX_HEADS, 1), lambda b, s: (0, 0)),
        ],
        out_specs=[
            pl.BlockSpec((1, 1, FOX_HEADS, TS_FF), lambda b, s: (b, s, 0, 0)),
            pl.BlockSpec((TS_FF, LANES), lambda b, s: (b * nt + s, 0)),
        ],
        out_shape=[
            jax.ShapeDtypeStruct((batch, nt, FOX_HEADS, TS_FF), F32),
            jax.ShapeDtypeStruct((batch * seq, LANES), F32),
        ],
        scratch_shapes=[pltpu.VMEM((FOX_HEADS, LANES), F32)],
        compiler_params=pltpu.CompilerParams(
            dimension_semantics=("arbitrary", "arbitrary"), vmem_limit_bytes=VMEM_LIMIT),
        name="forget_cumsum",
    )(h, w_ff_t, b_f)


def _softmax_pv(q_m, k_ref, v_ref, blk, qi, key_bias, row_shift, t_scr, mx_scr, ls_scr, acc_scr):
    lo, hi = blk * LANES, (blk + 1) * LANES
    mx_scr[...] = jnp.full(mx_scr.shape, -jnp.inf, F32)

    def logits(ki):
        start = pl.multiple_of(ki * TK, TK)
        k = k_ref[pl.ds(start, TK), lo:hi]
        s = lax.dot_general(q_m, k, (((1,), (1,)), ((), ())), preferred_element_type=F32)
        if key_bias is not None:
            s = s - key_bias(ki)
        return s

    def track_max(s):
        m = mx_scr[...]
        for c in range(TK // LANES):
            m = jnp.maximum(m, s[:, c * LANES:(c + 1) * LANES])
        mx_scr[...] = m

    def pass1(ki, carry):
        s = logits(ki)
        t_scr[ki] = s
        track_max(s)
        return carry

    lax.fori_loop(0, qi, pass1, 0)
    s = logits(qi)
    row = lax.broadcasted_iota(jnp.int32, (TQ, TK), 0)
    col = lax.broadcasted_iota(jnp.int32, (TQ, TK), 1)
    s = jnp.where(col <= row, s, -jnp.inf)
    t_scr[qi] = s
    track_max(s)

    m = jnp.max(mx_scr[...], axis=1, keepdims=True)
    if row_shift is not None:
        m = (m + row_shift) - row_shift
    ls_scr[...] = jnp.zeros_like(ls_scr)
    acc_scr[...] = jnp.zeros_like(acc_scr)

    def pass2(ki, carry):
        start = pl.multiple_of(ki * TK, TK)
        p = jnp.exp(t_scr[ki] - m)
        ls = ls_scr[...]
        for c in range(TK // LANES):
            ls = ls + p[:, c * LANES:(c + 1) * LANES]
        ls_scr[...] = ls
        v = v_ref[pl.ds(start, TK), lo:hi]
        acc_scr[...] += jnp.dot(p.astype(BF16), v, preferred_element_type=F32)
        return carry

    lax.fori_loop(0, qi + 1, pass2, 0)
    l = jnp.sum(ls_scr[...], axis=1, keepdims=True)
    return acc_scr[...], l


def _fox_kernel(q_ref, k_ref, v_ref, g_ref, crow_ref, ccol_ref, o_ref,
                t_scr, mx_scr, ls_scr, acc_scr):
    qi = pl.program_id(1)
    lane = lax.broadcasted_iota(jnp.int32, (TQ, LANES), 1)
    first = lane < HEAD_DIM
    ccol = ccol_ref[...]
    for hp in range(FOX_HEADS // 2):
        q = q_ref[:, hp * LANES:(hp + 1) * LANES]
        halves = []
        for sub in range(2):
            head = 2 * hp + sub
            q_m = jnp.where(first if sub == 0 else ~first, q, jnp.zeros_like(q))
            cq = ccol[:, head:head + 1]
            key_bias = lambda ki, head=head: crow_ref[0, ki, head:head + 1, :]
            acc, l = _softmax_pv(q_m, k_ref, v_ref, hp, qi, key_bias, cq,
                                 t_scr, mx_scr, ls_scr, acc_scr)
            halves.append(acc / l)
        o = jnp.where(first, halves[0], halves[1])
        g = g_ref[:, hp * LANES:(hp + 1) * LANES].astype(F32)
        o_ref[:, hp * LANES:(hp + 1) * LANES] = (o * g).astype(BF16)


def _diff_kernel(q_ref, k_ref, v_ref, g_ref, lam_ref, ng_ref, o_ref,
                 t_scr, mx_scr, ls_scr, acc_scr, *, lam_init):
    qi = pl.program_id(1)
    lane = lax.broadcasted_iota(jnp.int32, (TQ, LANES), 1)
    first = lane < HEAD_DIM
    lv = lam_ref[...]
    lam = (jnp.exp(jnp.sum(lv[0:1] * lv[1:2], axis=1, keepdims=True))
           - jnp.exp(jnp.sum(lv[2:3] * lv[3:4], axis=1, keepdims=True)) + lam_init)
    for h in range(DIFF_HEADS):
        q = q_ref[:, h * LANES:(h + 1) * LANES]
        parts = []
        for comp in range(2):
            q_m = jnp.where(first if comp == 0 else ~first, q, jnp.zeros_like(q))
            acc, l = _softmax_pv(q_m, k_ref, v_ref, h, qi, None, None,
                                 t_scr, mx_scr, ls_scr, acc_scr)
            parts.append(acc / l)
        o = parts[0] - lam * parts[1]
        ms = jnp.mean(o * o, axis=-1, keepdims=True)
        y = o * lax.rsqrt(ms + RMS_EPS) * ng_ref[...] * (1.0 - lam_init)
        g = g_ref[:, h * LANES:(h + 1) * LANES].astype(F32)
        o_ref[:, h * LANES:(h + 1) * LANES] = (y * g).astype(BF16)


def _attn_scratch(seq):
    return [
        pltpu.VMEM((seq // TK, TQ, TK), F32),
        pltpu.VMEM((TQ, LANES), F32),
        pltpu.VMEM((TQ, LANES), F32),
        pltpu.VMEM((TQ, LANES), F32),
    ]


def _fox_attention(p, crow, ccol, batch, seq):
    nq = seq // TQ
    return pl.pallas_call(
        _fox_kernel,
        grid=(batch, nq),
        in_specs=[
            pl.BlockSpec((TQ, WIDTH), lambda b, i: (b * nq + i, COL_FQ)),
            pl.BlockSpec((seq, WIDTH), lambda b, i: (b, COL_FK)),
            pl.BlockSpec((seq, WIDTH), lambda b, i: (b, COL_FV)),
            pl.BlockSpec((TQ, WIDTH), lambda b, i: (b * nq + i, COL_FG)),
            pl.BlockSpec((1, seq // TK, FOX_HEADS, TK), lambda b, i: (b, 0, 0, 0)),
            pl.BlockSpec((TQ, LANES), lambda b, i: (b * nq + i, 0)),
        ],
        out_specs=pl.BlockSpec((TQ, WIDTH), lambda b, i: (b * nq + i, 0)),
        out_shape=jax.ShapeDtypeStruct((batch * seq, WIDTH), BF16),
        scratch_shapes=_attn_scratch(seq),
        compiler_params=pltpu.CompilerParams(
            dimension_semantics=("arbitrary", "arbitrary"), vmem_limit_bytes=VMEM_LIMIT),
        name="fox_attention",
    )(p, p, p, p, crow, ccol)


def _diff_attention(p, lam_vecs, norm_g, lam_init, batch, seq):
    nq = seq // TQ
    return pl.pallas_call(
        functools.partial(_diff_kernel, lam_init=lam_init),
        grid=(batch, nq),
        in_specs=[
            pl.BlockSpec((TQ, WIDTH), lambda b, i: (b * nq + i, COL_DQ)),
            pl.BlockSpec((seq, WIDTH), lambda b, i: (b, COL_DK)),
            pl.BlockSpec((seq, WIDTH), lambda b, i: (b, COL_DV)),
            pl.BlockSpec((TQ, WIDTH), lambda b, i: (b * nq + i, COL_DG)),
            pl.BlockSpec((4, HEAD_DIM), lambda b, i: (0, 0)),
            pl.BlockSpec((1, LANES), lambda b, i: (0, 0)),
        ],
        out_specs=pl.BlockSpec((TQ, WIDTH), lambda b, i: (b * nq + i, 0)),
        out_shape=jax.ShapeDtypeStruct((batch * seq, WIDTH), BF16),
        scratch_shapes=_attn_scratch(seq),
        compiler_params=pltpu.CompilerParams(
            dimension_semantics=("arbitrary", "arbitrary"), vmem_limit_bytes=VMEM_LIMIT),
        name="diff_attention",
    )(p, p, p, p, lam_vecs, norm_g)


def _merge_kernel(x_ref, yf_ref, yd_ref, cb_ref, cc_ref, cx_ref, cg_ref, hc_ref, hx_ref,
                  g0_ref, g1_ref, g2_ref, cw_ref, wf_ref, wd_ref, wc_ref, wo_ref, pg_ref,
                  o_ref, ext_scr, *, tiles_per_seq):
    i = pl.program_id(0)
    halo = SUBLANES
    u = cc_ref[...].astype(F32) * cx_ref[...].astype(F32)
    prev = hc_ref[...].astype(F32) * hx_ref[...].astype(F32)
    prev = jnp.where(i % tiles_per_seq == 0, jnp.zeros_like(prev), prev)
    ext_scr[0:halo, :] = prev
    ext_scr[halo:halo + TM_OUT, :] = u
    cw = cw_ref[...]
    conv = (cw[2:3] * u + cw[1:2] * ext_scr[halo - 1:halo - 1 + TM_OUT, :]
            + cw[0:1] * ext_scr[halo - 2:halo - 2 + TM_OUT, :])
    y_conv = (cb_ref[...].astype(F32) * conv * cg_ref[...].astype(F32)).astype(BF16)

    m = (g0_ref[...].astype(F32) * jnp.dot(yf_ref[...], wf_ref[...], preferred_element_type=F32)
         + g1_ref[...].astype(F32) * jnp.dot(yd_ref[...], wd_ref[...], preferred_element_type=F32)
         + g2_ref[...].astype(F32) * jnp.dot(y_conv, wc_ref[...], preferred_element_type=F32))
    o = jnp.dot(m.astype(BF16), wo_ref[...], preferred_element_type=F32)
    ms = jnp.mean(o * o, axis=-1, keepdims=True)
    o_ref[...] = x_ref[...] + o * lax.rsqrt(ms + RMS_EPS) * pg_ref[...]


def _merge(x2, y_fox, y_diff, p, conv_w, w_fox, w_diff, w_conv, w_out, post_g, seq):
    t = x2.shape[0]
    tiles_per_seq = seq // TM_OUT
    halo_blocks = TM_OUT // SUBLANES
    gate_col = COL_MG * WIDTH // D_MODEL

    def tile(col):
        return pl.BlockSpec((TM_OUT, WIDTH), lambda i: (i, col))

    def halo(col):
        return pl.BlockSpec((SUBLANES, WIDTH),
                            lambda i: (jnp.maximum(i * halo_blocks - 1, 0), col))

    def whole(shape):
        return pl.BlockSpec(shape, lambda i: (0, 0))

    return pl.pallas_call(
        functools.partial(_merge_kernel, tiles_per_seq=tiles_per_seq),
        grid=(t // TM_OUT,),
        in_specs=[
            pl.BlockSpec((TM_OUT, D_MODEL), lambda i: (i, 0)),
            pl.BlockSpec((TM_OUT, WIDTH), lambda i: (i, 0)),
            pl.BlockSpec((TM_OUT, WIDTH), lambda i: (i, 0)),
            tile(COL_CB), tile(COL_CC), tile(COL_CX), tile(COL_CG),
            halo(COL_CC), halo(COL_CX),
            pl.BlockSpec((TM_OUT, D_MODEL), lambda i: (i, gate_col)),
            pl.BlockSpec((TM_OUT, D_MODEL), lambda i: (i, gate_col + 1)),
            pl.BlockSpec((TM_OUT, D_MODEL), lambda i: (i, gate_col + 2)),
            whole((CONV_K, WIDTH)),
            whole((WIDTH, D_MODEL)), whole((WIDTH, D_MODEL)), whole((WIDTH, D_MODEL)),
            whole((D_MODEL, D_MODEL)),
            whole((1, D_MODEL)),
        ],
        out_specs=pl.BlockSpec((TM_OUT, D_MODEL), lambda i: (i, 0)),
        out_shape=jax.ShapeDtypeStruct((t, D_MODEL), F32),
        scratch_shapes=[pltpu.VMEM((TM_OUT + SUBLANES, WIDTH), F32)],
        compiler_params=pltpu.CompilerParams(
            dimension_semantics=("arbitrary",), vmem_limit_bytes=VMEM_LIMIT),
        name="merge",
    )(x2, y_fox, y_diff, p, p, p, p, p, p, p, p, p, conv_w, w_fox, w_diff, w_conv, w_out, post_g)


def _rope_lane_tables(positions):
    half = ROT_DIM // 2
    inv_freq = ROPE_THETA ** (-jnp.arange(0, ROT_DIM, 2, dtype=F32) / ROT_DIM)
    ang = positions.astype(F32).reshape(-1, 1) * inv_freq
    cos, sin = jnp.cos(ang), jnp.sin(ang)
    t = ang.shape[0]
    ones = jnp.ones((t, HEAD_DIM - ROT_DIM), F32)
    zeros_rest = jnp.zeros((t, HEAD_DIM - ROT_DIM), F32)
    zeros_half = jnp.zeros((t, half), F32)
    c64 = jnp.concatenate([cos, cos, ones], axis=1)
    s1_64 = jnp.concatenate([-sin, zeros_half, zeros_rest], axis=1)
    s2_64 = jnp.concatenate([zeros_half, sin, zeros_rest], axis=1)
    rep = LANES // HEAD_DIM
    return jnp.tile(c64, (1, rep)), jnp.tile(s1_64, (1, rep)), jnp.tile(s2_64, (1, rep))


def _regroup_w_in(w):
    offs = [_OFF_FQ, _OFF_FK, _OFF_FV, _OFF_FG, _OFF_DQ, _OFF_DK, _OFF_DV, _OFF_DG,
            _OFF_CB, _OFF_CC, _OFF_CX, _OFF_CG]
    cols = [w[:, o:o + WIDTH] for o in offs] + [w[:, _OFF_MG:_OFF_MG + N_BRANCH * D_MODEL]]
    return jnp.concatenate(cols, axis=1).astype(BF16)


def kernel(x, positions, pre_norm_g, w_in, b_forget, b_merge, conv_w, lam_q1, lam_k1, lam_q2,
           lam_k2, diff_norm_g, w_br_fox, w_br_diff, w_br_conv, w_out, post_norm_g):
    batch, seq, _ = x.shape
    depth = w_in.shape[0]
    assert seq % TQ == 0 and TQ == TK == TS_FF and (batch * seq) % TM_IN == 0
    rc, rs1, rs2 = _rope_lane_tables(positions)
    x2 = x.reshape(batch * seq, D_MODEL)
    for l in range(depth):
        lam_init = 0.8 - 0.6 * math.exp(-0.3 * l)
        w_main = _regroup_w_in(w_in[l])
        w_ff_t = w_in[l][:, _OFF_FF:_OFF_FF + FOX_HEADS].T.astype(BF16)
        p, h = _in_proj(x2, pre_norm_g[l].reshape(1, D_MODEL), w_main,
                        b_merge[l].reshape(1, N_BRANCH * D_MODEL), rc, rs1, rs2)
        crow, ccol = _forget_cumsum(h, w_ff_t, b_forget[l].reshape(FOX_HEADS, 1), batch, seq)
        y_fox = _fox_attention(p, crow, ccol, batch, seq)
        lam_vecs = jnp.stack([lam_q1[l], lam_k1[l], lam_q2[l], lam_k2[l]])
        y_diff = _diff_attention(p, lam_vecs, diff_norm_g[l].reshape(1, LANES), lam_init, batch, seq)
        x2 = _merge(x2, y_fox, y_diff, p, conv_w[l],
                    w_br_fox[l].astype(BF16), w_br_diff[l].astype(BF16),
                    w_br_conv[l].astype(BF16), w_out[l].astype(BF16),
                    post_norm_g[l].reshape(1, D_MODEL), seq)
    return x2.reshape(batch, seq, D_MODEL)
```

```python
import functools
import math

import jax
import jax.numpy as jnp
from jax import lax
from jax.experimental import pallas as pl
from jax.experimental.pallas import tpu as pltpu

F32 = jnp.float32
BF16 = jnp.bfloat16

D_MODEL = 1024
HEAD_DIM = 64
FOX_HEADS = 8
DIFF_HEADS = 4
WIDTH = 512
CONV_K = 3
N_BRANCH = 3
ROPE_THETA = 500000.0
ROT_DIM = HEAD_DIM // 4
RMS_EPS = 1e-6
LANES = 128
SUBLANES = 8

(A_FQ, A_FK, A_FV, A_DV, A_CB, A_CC, A_CX) = range(7)
(S_FG, S_DG, S_CG) = range(3)
(R_DQ, R_DK) = range(2)
N_GATE_BLOCKS = N_BRANCH * D_MODEL // WIDTH

_OFF_FQ, _OFF_FK, _OFF_FV = 0, 512, 1024
_OFF_FF = 1536
_OFF_FG = 1544
_OFF_DQ, _OFF_DK, _OFF_DV, _OFF_DG = 2056, 2568, 3080, 3592
_OFF_CB, _OFF_CC, _OFF_CX, _OFF_CG = 4104, 4616, 5128, 5640
_OFF_MG = 6152

TM_IN = 1024
TN_PLAIN = 1792
TN_SILU = 1536
TN_ROT = 1024
TN_GATE = 1536
TS_FF = 512
TQ = 512
TK = 512
TM_OUT = 512
VMEM_LIMIT = 48 * 1024 * 1024


def _silu(v):
    return v * jax.nn.sigmoid(v)


def _rotary_block(acc, c, s1, s2):
    outs = []
    for i in range(acc.shape[1] // LANES):
        v = acc[:, i * LANES:(i + 1) * LANES]
        outs.append(v * c + pltpu.roll(v, LANES - ROT_DIM // 2, 1) * s1
                    + pltpu.roll(v, ROT_DIM // 2, 1) * s2)
    return jnp.concatenate(outs, axis=1)


def _plain_proj_kernel(x_ref, g_ref, w_ref, sc_ref, o_ref, h_ref, h_scr):
    @pl.when(pl.program_id(1) == 0)
    def _():
        x = x_ref[...]
        ms = jnp.mean(x * x, axis=-1, keepdims=True)
        h = (x * lax.rsqrt(ms + RMS_EPS) * g_ref[...]).astype(BF16)
        h_scr[...] = h
        h_ref[...] = h

    acc = jnp.dot(h_scr[...], w_ref[...], preferred_element_type=F32)
    o_ref[...] = (acc * sc_ref[...]).astype(BF16)


def _silu_proj_kernel(h_ref, w_ref, o_ref):
    acc = jnp.dot(h_ref[...], w_ref[...], preferred_element_type=F32)
    o_ref[...] = _silu(acc).astype(BF16)


def _rot_proj_kernel(h_ref, w_ref, sc_ref, rc_ref, rs1_ref, rs2_ref, o_ref):
    acc = jnp.dot(h_ref[...], w_ref[...], preferred_element_type=F32)
    r = _rotary_block(acc, rc_ref[...], rs1_ref[...], rs2_ref[...])
    o_ref[...] = (r * sc_ref[...]).astype(BF16)


def _gate_proj_kernel(h_ref, w_ref, b_ref, o_ref):
    acc = jnp.dot(h_ref[...], w_ref[...], preferred_element_type=F32)
    o_ref[...] = jax.nn.sigmoid(acc + b_ref[...]).astype(BF16)


_PROJ_PARAMS = pltpu.CompilerParams(
    dimension_semantics=("arbitrary", "arbitrary"), vmem_limit_bytes=VMEM_LIMIT)
_ROW_TILE = pl.BlockSpec((TM_IN, D_MODEL), lambda i, j: (i, 0))
_ROPE_TILE = pl.BlockSpec((TM_IN, LANES), lambda i, j: (i, 0))


def _w_block(tn):
    return pl.BlockSpec((D_MODEL, tn), lambda i, j: (0, j))


def _col_vec(tn):
    return pl.BlockSpec((1, tn), lambda i, j: (0, j))


def _out_block(tn):
    return pl.BlockSpec((TM_IN, tn), lambda i, j: (i, j))


def _plain_proj(x2, pre_g, w, col_scale, tn):
    t, n = x2.shape[0], w.shape[1]
    return pl.pallas_call(
        _plain_proj_kernel,
        grid=(t // TM_IN, n // tn),
        in_specs=[_ROW_TILE, pl.BlockSpec((1, D_MODEL), lambda i, j: (0, 0)),
                  _w_block(tn), _col_vec(tn)],
        out_specs=[_out_block(tn), _ROW_TILE],
        out_shape=[jax.ShapeDtypeStruct((t, n), BF16), jax.ShapeDtypeStruct((t, D_MODEL), BF16)],
        scratch_shapes=[pltpu.VMEM((TM_IN, D_MODEL), BF16)],
        compiler_params=_PROJ_PARAMS,
        name="proj_plain",
    )(x2, pre_g, w, col_scale)


def _epilogue_proj(body, name, h, w, tn, extra, extra_specs):
    t, n = h.shape[0], w.shape[1]
    return pl.pallas_call(
        body,
        grid=(t // TM_IN, n // tn),
        in_specs=[_ROW_TILE, _w_block(tn)] + extra_specs,
        out_specs=_out_block(tn),
        out_shape=jax.ShapeDtypeStruct((t, n), BF16),
        compiler_params=_PROJ_PARAMS,
        name=name,
    )(h, w, *extra)


def _split3(v):
    hi = v.astype(BF16)
    r1 = v - hi.astype(F32)
    mid = r1.astype(BF16)
    lo = (r1 - mid.astype(F32)).astype(BF16)
    return hi, mid, lo


def _forget_kernel(h_ref, wt_ref, bf_ref, crow_ref, ccol_ref, carry_scr):
    si = pl.program_id(1)

    @pl.when(si == 0)
    def _():
        carry_scr[...] = jnp.zeros_like(carry_scr)

    ff = lax.dot_general(wt_ref[...], h_ref[...], (((1,), (1,)), ((), ())),
                         preferred_element_type=F32)
    lf = jax.nn.log_sigmoid(ff + bf_ref[...])
    row = lax.broadcasted_iota(jnp.int32, (TS_FF, TS_FF), 0)
    col = lax.broadcasted_iota(jnp.int32, (TS_FF, TS_FF), 1)
    upper = (row <= col).astype(BF16)
    cum = jnp.zeros((FOX_HEADS, TS_FF), F32)
    for piece in _split3(lf):
        cum = cum + jnp.dot(piece, upper, preferred_element_type=F32)
    c = cum + carry_scr[:, 0:1]
    carry_scr[...] = jnp.broadcast_to(c[:, TS_FF - 1:TS_FF], carry_scr.shape)
    crow_ref[0, 0] = c
    padded = jnp.concatenate([c, jnp.zeros((LANES - FOX_HEADS, TS_FF), F32)], axis=0)
    ccol_ref[...] = padded.T


def _forget_cumsum(h, w_ff_t, b_f, batch, seq):
    nt = seq // TS_FF
    return pl.pallas_call(
        _forget_kernel,
        grid=(batch, nt),
        in_specs=[
            pl.BlockSpec((TS_FF, D_MODEL), lambda b, s: (b * nt + s, 0)),
            pl.BlockSpec((FOX_HEADS, D_MODEL), lambda b, s: (0, 0)),
            pl.BlockSpec((FOX_HEADS, 1), lambda b, s: (0, 0)),
        ],
        out_specs=[
            pl.BlockSpec((1, 1, FOX_HEADS, TS_FF), lambda b, s: (b, s, 0, 0)),
            pl.BlockSpec((TS_FF, LANES), lambda b, s: (b * nt + s, 0)),
        ],
        out_shape=[
            jax.ShapeDtypeStruct((batch, nt, FOX_HEADS, TS_FF), F32),
            jax.ShapeDtypeStruct((batch * seq, LANES), F32),
        ],
        scratch_shapes=[pltpu.VMEM((FOX_HEADS, LANES), F32)],
        compiler_params=pltpu.CompilerParams(
            dimension_semantics=("arbitrary", "arbitrary"), vmem_limit_bytes=VMEM_LIMIT),
        name="forget_cumsum",
    )(h, w_ff_t, b_f)


def _softmax_pv(q_m, k_ref, v_ref, blk, qi, key_bias, row_shift, t_scr, mx_scr, ls_scr, acc_scr):
    lo, hi = blk * LANES, (blk + 1) * LANES
    mx_scr[...] = jnp.full(mx_scr.shape, -jnp.inf, F32)

    def logits(ki):
        start = pl.multiple_of(ki * TK, TK)
        k = k_ref[pl.ds(start, TK), lo:hi]
        s = lax.dot_general(q_m, k, (((1,), (1,)), ((), ())), preferred_element_type=F32)
        if key_bias is not None:
            s = s - key_bias(ki)
        return s

    def track_max(s):
        m = mx_scr[...]
        for c in range(TK // LANES):
            m = jnp.maximum(m, s[:, c * LANES:(c + 1) * LANES])
        mx_scr[...] = m

    def pass1(ki, carry):
        s = logits(ki)
        t_scr[ki] = s
        track_max(s)
        return carry

    lax.fori_loop(0, qi, pass1, 0)
    s = logits(qi)
    row = lax.broadcasted_iota(jnp.int32, (TQ, TK), 0)
    col = lax.broadcasted_iota(jnp.int32, (TQ, TK), 1)
    s = jnp.where(col <= row, s, -jnp.inf)
    t_scr[qi] = s
    track_max(s)

    m = jnp.max(mx_scr[...], axis=1, keepdims=True)
    if row_shift is not None:
        m = (m + row_shift) - row_shift
    ls_scr[...] = jnp.zeros_like(ls_scr)
    acc_scr[...] = jnp.zeros_like(acc_scr)

    def pass2(ki, carry):
        start = pl.multiple_of(ki * TK, TK)
        p = jnp.exp(t_scr[ki] - m)
        ls = ls_scr[...]
        for c in range(TK // LANES):
            ls = ls + p[:, c * LANES:(c + 1) * LANES]
        ls_scr[...] = ls
        v = v_ref[pl.ds(start, TK), lo:hi]
        acc_scr[...] += jnp.dot(p.astype(BF16), v, preferred_element_type=F32)
        return carry

    lax.fori_loop(0, qi + 1, pass2, 0)
    l = jnp.sum(ls_scr[...], axis=1, keepdims=True)
    return acc_scr[...], l


def _fox_kernel(q_ref, k_ref, v_ref, g_ref, crow_ref, ccol_ref, o_ref,
                t_scr, mx_scr, ls_scr, acc_scr):
    qi = pl.program_id(1)
    lane = lax.broadcasted_iota(jnp.int32, (TQ, LANES), 1)
    first = lane < HEAD_DIM
    ccol = ccol_ref[...]
    for hp in range(FOX_HEADS // 2):
        q = q_ref[:, hp * LANES:(hp + 1) * LANES]
        halves = []
        for sub in range(2):
            head = 2 * hp + sub
            q_m = jnp.where(first if sub == 0 else ~first, q, jnp.zeros_like(q))
            cq = ccol[:, head:head + 1]
            key_bias = lambda ki, head=head: crow_ref[0, ki, head:head + 1, :]
            acc, l = _softmax_pv(q_m, k_ref, v_ref, hp, qi, key_bias, cq,
                                 t_scr, mx_scr, ls_scr, acc_scr)
            halves.append(acc / l)
        o = jnp.where(first, halves[0], halves[1])
        g = g_ref[:, hp * LANES:(hp + 1) * LANES].astype(F32)
        o_ref[:, hp * LANES:(hp + 1) * LANES] = (o * g).astype(BF16)


def _diff_kernel(q_ref, k_ref, v_ref, g_ref, lam_ref, ng_ref, o_ref,
                 t_scr, mx_scr, ls_scr, acc_scr, *, lam_init):
    qi = pl.program_id(1)
    lane = lax.broadcasted_iota(jnp.int32, (TQ, LANES), 1)
    first = lane < HEAD_DIM
    lv = lam_ref[...]
    lam = (jnp.exp(jnp.sum(lv[0:1] * lv[1:2], axis=1, keepdims=True))
           - jnp.exp(jnp.sum(lv[2:3] * lv[3:4], axis=1, keepdims=True)) + lam_init)
    for h in range(DIFF_HEADS):
        q = q_ref[:, h * LANES:(h + 1) * LANES]
        parts = []
        for comp in range(2):
            q_m = jnp.where(first if comp == 0 else ~first, q, jnp.zeros_like(q))
            acc, l = _softmax_pv(q_m, k_ref, v_ref, h, qi, None, None,
                                 t_scr, mx_scr, ls_scr, acc_scr)
            parts.append(acc / l)
        o = parts[0] - lam * parts[1]
        ms = jnp.mean(o * o, axis=-1, keepdims=True)
        y = o * lax.rsqrt(ms + RMS_EPS) * ng_ref[...] * (1.0 - lam_init)
        g = g_ref[:, h * LANES:(h + 1) * LANES].astype(F32)
        o_ref[:, h * LANES:(h + 1) * LANES] = (y * g).astype(BF16)


def _attn_scratch(seq):
    return [
        pltpu.VMEM((seq // TK, TQ, TK), F32),
        pltpu.VMEM((TQ, LANES), F32),
        pltpu.VMEM((TQ, LANES), F32),
        pltpu.VMEM((TQ, LANES), F32),
    ]


def _fox_attention(pa, ps, crow, ccol, batch, seq):
    nq = seq // TQ
    return pl.pallas_call(
        _fox_kernel,
        grid=(batch, nq),
        in_specs=[
            pl.BlockSpec((TQ, WIDTH), lambda b, i: (b * nq + i, A_FQ)),
            pl.BlockSpec((seq, WIDTH), lambda b, i: (b, A_FK)),
            pl.BlockSpec((seq, WIDTH), lambda b, i: (b, A_FV)),
            pl.BlockSpec((TQ, WIDTH), lambda b, i: (b * nq + i, S_FG)),
            pl.BlockSpec((1, seq // TK, FOX_HEADS, TK), lambda b, i: (b, 0, 0, 0)),
            pl.BlockSpec((TQ, LANES), lambda b, i: (b * nq + i, 0)),
        ],
        out_specs=pl.BlockSpec((TQ, WIDTH), lambda b, i: (b * nq + i, 0)),
        out_shape=jax.ShapeDtypeStruct((batch * seq, WIDTH), BF16),
        scratch_shapes=_attn_scratch(seq),
        compiler_params=pltpu.CompilerParams(
            dimension_semantics=("arbitrary", "arbitrary"), vmem_limit_bytes=VMEM_LIMIT),
        name="fox_attention",
    )(pa, pa, pa, ps, crow, ccol)


def _diff_attention(pr, pa, ps, lam_vecs, norm_g, lam_init, batch, seq):
    nq = seq // TQ
    return pl.pallas_call(
        functools.partial(_diff_kernel, lam_init=lam_init),
        grid=(batch, nq),
        in_specs=[
            pl.BlockSpec((TQ, WIDTH), lambda b, i: (b * nq + i, R_DQ)),
            pl.BlockSpec((seq, WIDTH), lambda b, i: (b, R_DK)),
            pl.BlockSpec((seq, WIDTH), lambda b, i: (b, A_DV)),
            pl.BlockSpec((TQ, WIDTH), lambda b, i: (b * nq + i, S_DG)),
            pl.BlockSpec((4, HEAD_DIM), lambda b, i: (0, 0)),
            pl.BlockSpec((1, LANES), lambda b, i: (0, 0)),
        ],
        out_specs=pl.BlockSpec((TQ, WIDTH), lambda b, i: (b * nq + i, 0)),
        out_shape=jax.ShapeDtypeStruct((batch * seq, WIDTH), BF16),
        scratch_shapes=_attn_scratch(seq),
        compiler_params=pltpu.CompilerParams(
            dimension_semantics=("arbitrary", "arbitrary"), vmem_limit_bytes=VMEM_LIMIT),
        name="diff_attention",
    )(pr, pr, pa, ps, lam_vecs, norm_g)


def _merge_kernel(x_ref, yf_ref, yd_ref, cb_ref, cc_ref, cx_ref, cg_ref, hc_ref, hx_ref,
                  g0_ref, g1_ref, g2_ref, cw_ref, wf_ref, wd_ref, wc_ref, wo_ref, pg_ref,
                  o_ref, ext_scr, *, tiles_per_seq):
    i = pl.program_id(0)
    halo = SUBLANES
    u = cc_ref[...].astype(F32) * cx_ref[...].astype(F32)
    prev = hc_ref[...].astype(F32) * hx_ref[...].astype(F32)
    prev = jnp.where(i % tiles_per_seq == 0, jnp.zeros_like(prev), prev)
    ext_scr[0:halo, :] = prev
    ext_scr[halo:halo + TM_OUT, :] = u
    cw = cw_ref[...]
    conv = (cw[2:3] * u + cw[1:2] * ext_scr[halo - 1:halo - 1 + TM_OUT, :]
            + cw[0:1] * ext_scr[halo - 2:halo - 2 + TM_OUT, :])
    y_conv = (cb_ref[...].astype(F32) * conv * cg_ref[...].astype(F32)).astype(BF16)

    m = (g0_ref[...].astype(F32) * jnp.dot(yf_ref[...], wf_ref[...], preferred_element_type=F32)
         + g1_ref[...].astype(F32) * jnp.dot(yd_ref[...], wd_ref[...], preferred_element_type=F32)
         + g2_ref[...].astype(F32) * jnp.dot(y_conv, wc_ref[...], preferred_element_type=F32))
    o = jnp.dot(m.astype(BF16), wo_ref[...], preferred_element_type=F32)
    ms = jnp.mean(o * o, axis=-1, keepdims=True)
    o_ref[...] = x_ref[...] + o * lax.rsqrt(ms + RMS_EPS) * pg_ref[...]


def _merge(x2, y_fox, y_diff, pa, ps, pg, conv_w, w_fox, w_diff, w_conv, w_out, post_g, seq):
    t = x2.shape[0]
    tiles_per_seq = seq // TM_OUT
    halo_blocks = TM_OUT // SUBLANES

    def tile(col):
        return pl.BlockSpec((TM_OUT, WIDTH), lambda i: (i, col))

    def halo(col):
        return pl.BlockSpec((SUBLANES, WIDTH),
                            lambda i: (jnp.maximum(i * halo_blocks - 1, 0), col))

    def whole(shape):
        return pl.BlockSpec(shape, lambda i: (0, 0))

    return pl.pallas_call(
        functools.partial(_merge_kernel, tiles_per_seq=tiles_per_seq),
        grid=(t // TM_OUT,),
        in_specs=[
            pl.BlockSpec((TM_OUT, D_MODEL), lambda i: (i, 0)),
            pl.BlockSpec((TM_OUT, WIDTH), lambda i: (i, 0)),
            pl.BlockSpec((TM_OUT, WIDTH), lambda i: (i, 0)),
            tile(A_CB), tile(A_CC), tile(A_CX), tile(S_CG),
            halo(A_CC), halo(A_CX),
            pl.BlockSpec((TM_OUT, D_MODEL), lambda i: (i, 0)),
            pl.BlockSpec((TM_OUT, D_MODEL), lambda i: (i, 1)),
            pl.BlockSpec((TM_OUT, D_MODEL), lambda i: (i, 2)),
            whole((CONV_K, WIDTH)),
            whole((WIDTH, D_MODEL)), whole((WIDTH, D_MODEL)), whole((WIDTH, D_MODEL)),
            whole((D_MODEL, D_MODEL)),
            whole((1, D_MODEL)),
        ],
        out_specs=pl.BlockSpec((TM_OUT, D_MODEL), lambda i: (i, 0)),
        out_shape=jax.ShapeDtypeStruct((t, D_MODEL), F32),
        scratch_shapes=[pltpu.VMEM((TM_OUT + SUBLANES, WIDTH), F32)],
        compiler_params=pltpu.CompilerParams(
            dimension_semantics=("arbitrary",), vmem_limit_bytes=VMEM_LIMIT),
        name="merge",
    )(x2, y_fox, y_diff, pa, pa, pa, ps, pa, pa, pg, pg, pg,
      conv_w, w_fox, w_diff, w_conv, w_out, post_g)


def _rope_lane_tables(positions):
    half = ROT_DIM // 2
    inv_freq = ROPE_THETA ** (-jnp.arange(0, ROT_DIM, 2, dtype=F32) / ROT_DIM)
    ang = positions.astype(F32).reshape(-1, 1) * inv_freq
    cos, sin = jnp.cos(ang), jnp.sin(ang)
    t = ang.shape[0]
    ones = jnp.ones((t, HEAD_DIM - ROT_DIM), F32)
    zeros_rest = jnp.zeros((t, HEAD_DIM - ROT_DIM), F32)
    zeros_half = jnp.zeros((t, half), F32)
    c64 = jnp.concatenate([cos, cos, ones], axis=1)
    s1_64 = jnp.concatenate([-sin, zeros_half, zeros_rest], axis=1)
    s2_64 = jnp.concatenate([zeros_half, sin, zeros_rest], axis=1)
    rep = LANES // HEAD_DIM
    return jnp.tile(c64, (1, rep)), jnp.tile(s1_64, (1, rep)), jnp.tile(s2_64, (1, rep))


def _gather_cols(w, offsets):
    return jnp.concatenate([w[:, o:o + WIDTH] for o in offsets], axis=1).astype(BF16)


def _scale_vec(n_blocks, scaled_block):
    blk = jnp.arange(n_blocks * WIDTH, dtype=jnp.int32) // WIDTH
    return jnp.where(blk == scaled_block, HEAD_DIM ** -0.5, 1.0).astype(F32).reshape(1, -1)


def kernel(x, positions, pre_norm_g, w_in, b_forget, b_merge, conv_w, lam_q1, lam_k1, lam_q2,
           lam_k2, diff_norm_g, w_br_fox, w_br_diff, w_br_conv, w_out, post_norm_g):
    batch, seq, _ = x.shape
    depth = w_in.shape[0]
    assert seq % TQ == 0 and TQ == TK == TS_FF and (batch * seq) % TM_IN == 0
    rc, rs1, rs2 = _rope_lane_tables(positions)
    x2 = x.reshape(batch * seq, D_MODEL)
    for l in range(depth):
        lam_init = 0.8 - 0.6 * math.exp(-0.3 * l)
        w = w_in[l]
        w_plain = _gather_cols(w, [_OFF_FQ, _OFF_FK, _OFF_FV, _OFF_DV, _OFF_CB, _OFF_CC, _OFF_CX])
        w_silu = _gather_cols(w, [_OFF_FG, _OFF_DG, _OFF_CG])
        w_rot = _gather_cols(w, [_OFF_DQ, _OFF_DK])
        w_gate = w[:, _OFF_MG:_OFF_MG + N_BRANCH * D_MODEL].astype(BF16)
        w_ff_t = w[:, _OFF_FF:_OFF_FF + FOX_HEADS].T.astype(BF16)
        pa, h = _plain_proj(x2, pre_norm_g[l].reshape(1, D_MODEL), w_plain,
                            _scale_vec(7, A_FQ), TN_PLAIN)
        ps = _epilogue_proj(_silu_proj_kernel, "proj_silu", h, w_silu, TN_SILU, [], [])
        pr = _epilogue_proj(_rot_proj_kernel, "proj_rotary", h, w_rot, TN_ROT,
                            [_scale_vec(2, R_DQ), rc, rs1, rs2],
                            [_col_vec(TN_ROT), _ROPE_TILE, _ROPE_TILE, _ROPE_TILE])
        pg = _epilogue_proj(_gate_proj_kernel, "proj_gate", h, w_gate, TN_GATE,
                            [b_merge[l].reshape(1, N_BRANCH * D_MODEL)], [_col_vec(TN_GATE)])
        crow, ccol = _forget_cumsum(h, w_ff_t, b_forget[l].reshape(FOX_HEADS, 1), batch, seq)
        y_fox = _fox_attention(pa, ps, crow, ccol, batch, seq)
        lam_vecs = jnp.stack([lam_q1[l], lam_k1[l], lam_q2[l], lam_k2[l]])
        y_diff = _diff_attention(pr, pa, ps, lam_vecs, diff_norm_g[l].reshape(1, LANES),
                                 lam_init, batch, seq)
        x2 = _merge(x2, y_fox, y_diff, pa, ps, pg, conv_w[l],
                    w_br_fox[l].astype(BF16), w_br_diff[l].astype(BF16),
                    w_br_conv[l].astype(BF16), w_out[l].astype(BF16),
                    post_norm_g[l].reshape(1, D_MODEL), seq)
    return x2.reshape(batch, seq, D_MODEL)
```

```python
import functools
import math
from typing import Any, Callable, NamedTuple, Optional

import jax
import jax.numpy as jnp
from jax import lax
from jax.experimental import pallas as pl
from jax.experimental.pallas import tpu as pltpu

F32 = jnp.float32
BF16 = jnp.bfloat16

D_MODEL = 1024
HEAD_DIM = 64
FOX_HEADS = 8
DIFF_HEADS = 4
WIDTH = 512
CONV_K = 3
N_BRANCH = 3
ROPE_THETA = 500000.0
ROT_DIM = HEAD_DIM // 4
RMS_EPS = 1e-6
LANES = 128
SUBLANES = 8

(A_FQ, A_FK, A_FV, A_DV, A_CB, A_CC, A_CX) = range(7)
(S_FG, S_DG, S_CG) = range(3)
(R_DQ, R_DK) = range(2)
N_GATE_BLOCKS = N_BRANCH * D_MODEL // WIDTH

_OFF_FQ, _OFF_FK, _OFF_FV = 0, 512, 1024
_OFF_FF = 1536
_OFF_FG = 1544
_OFF_DQ, _OFF_DK, _OFF_DV, _OFF_DG = 2056, 2568, 3080, 3592
_OFF_CB, _OFF_CC, _OFF_CX, _OFF_CG = 4104, 4616, 5128, 5640
_OFF_MG = 6152

TM_IN = 1024
TN_PLAIN = 1792
TN_SILU = 1536
TN_ROT = 1024
TN_GATE = 1536
TS_FF = 512
TQ = 512
TK = 512
TM_OUT = 512
VMEM_LIMIT = 48 * 1024 * 1024


def _silu(v):
    return v * jax.nn.sigmoid(v)


def _rotary_block(acc, c, s1, s2):
    outs = []
    for i in range(acc.shape[1] // LANES):
        v = acc[:, i * LANES:(i + 1) * LANES]
        outs.append(v * c + pltpu.roll(v, LANES - ROT_DIM // 2, 1) * s1
                    + pltpu.roll(v, ROT_DIM // 2, 1) * s2)
    return jnp.concatenate(outs, axis=1)


def _plain_proj_kernel(x_ref, g_ref, w_ref, sc_ref, o_ref, h_ref, h_scr):
    @pl.when(pl.program_id(1) == 0)
    def _():
        x = x_ref[...]
        ms = jnp.mean(x * x, axis=-1, keepdims=True)
        h = (x * lax.rsqrt(ms + RMS_EPS) * g_ref[...]).astype(BF16)
        h_scr[...] = h
        h_ref[...] = h

    acc = jnp.dot(h_scr[...], w_ref[...], preferred_element_type=F32)
    o_ref[...] = (acc * sc_ref[...]).astype(BF16)


def _silu_proj_kernel(h_ref, w_ref, o_ref):
    acc = jnp.dot(h_ref[...], w_ref[...], preferred_element_type=F32)
    o_ref[...] = _silu(acc).astype(BF16)


def _rot_proj_kernel(h_ref, w_ref, sc_ref, rc_ref, rs1_ref, rs2_ref, o_ref):
    acc = jnp.dot(h_ref[...], w_ref[...], preferred_element_type=F32)
    r = _rotary_block(acc, rc_ref[...], rs1_ref[...], rs2_ref[...])
    o_ref[...] = (r * sc_ref[...]).astype(BF16)


def _gate_proj_kernel(h_ref, w_ref, b_ref, o_ref):
    acc = jnp.dot(h_ref[...], w_ref[...], preferred_element_type=F32)
    o_ref[...] = jax.nn.sigmoid(acc + b_ref[...]).astype(BF16)


_PROJ_PARAMS = pltpu.CompilerParams(
    dimension_semantics=("arbitrary", "arbitrary"), vmem_limit_bytes=VMEM_LIMIT)
_ROW_TILE = pl.BlockSpec((TM_IN, D_MODEL), lambda i, j: (i, 0))
_ROPE_TILE = pl.BlockSpec((TM_IN, LANES), lambda i, j: (i, 0))


def _w_block(tn):
    return pl.BlockSpec((D_MODEL, tn), lambda i, j: (0, j))


def _col_vec(tn):
    return pl.BlockSpec((1, tn), lambda i, j: (0, j))


def _out_block(tn):
    return pl.BlockSpec((TM_IN, tn), lambda i, j: (i, j))


def _plain_proj(x2, pre_g, w, col_scale, tn):
    t, n = x2.shape[0], w.shape[1]
    return pl.pallas_call(
        _plain_proj_kernel,
        grid=(t // TM_IN, n // tn),
        in_specs=[_ROW_TILE, pl.BlockSpec((1, D_MODEL), lambda i, j: (0, 0)),
                  _w_block(tn), _col_vec(tn)],
        out_specs=[_out_block(tn), _ROW_TILE],
        out_shape=[jax.ShapeDtypeStruct((t, n), BF16), jax.ShapeDtypeStruct((t, D_MODEL), BF16)],
        scratch_shapes=[pltpu.VMEM((TM_IN, D_MODEL), BF16)],
        compiler_params=_PROJ_PARAMS,
        name="proj_plain",
    )(x2, pre_g, w, col_scale)


def _epilogue_proj(body, name, h, w, tn, extra, extra_specs):
    t, n = h.shape[0], w.shape[1]
    return pl.pallas_call(
        body,
        grid=(t // TM_IN, n // tn),
        in_specs=[_ROW_TILE, _w_block(tn)] + extra_specs,
        out_specs=_out_block(tn),
        out_shape=jax.ShapeDtypeStruct((t, n), BF16),
        compiler_params=_PROJ_PARAMS,
        name=name,
    )(h, w, *extra)


def _split3(v):
    hi = v.astype(BF16)
    r1 = v - hi.astype(F32)
    mid = r1.astype(BF16)
    lo = (r1 - mid.astype(F32)).astype(BF16)
    return hi, mid, lo


def _forget_kernel(h_ref, wt_ref, bf_ref, crow_ref, ccol_ref, carry_scr):
    si = pl.program_id(1)

    @pl.when(si == 0)
    def _():
        carry_scr[...] = jnp.zeros_like(carry_scr)

    ff = lax.dot_general(wt_ref[...], h_ref[...], (((1,), (1,)), ((), ())),
                         preferred_element_type=F32)
    lf = jax.nn.log_sigmoid(ff + bf_ref[...])
    row = lax.broadcasted_iota(jnp.int32, (TS_FF, TS_FF), 0)
    col = lax.broadcasted_iota(jnp.int32, (TS_FF, TS_FF), 1)
    upper = (row <= col).astype(BF16)
    cum = jnp.zeros((FOX_HEADS, TS_FF), F32)
    for piece in _split3(lf):
        cum = cum + jnp.dot(piece, upper, preferred_element_type=F32)
    c = cum + carry_scr[:, 0:1]
    carry_scr[...] = jnp.broadcast_to(c[:, TS_FF - 1:TS_FF], carry_scr.shape)
    crow_ref[0, 0] = c
    padded = jnp.concatenate([c, jnp.zeros((LANES - FOX_HEADS, TS_FF), F32)], axis=0)
    ccol_ref[...] = padded.T


def _forget_cumsum(h, w_ff_t, b_f, batch, seq):
    nt = seq // TS_FF
    return pl.pallas_call(
        _forget_kernel,
        grid=(batch, nt),
        in_specs=[
            pl.BlockSpec((TS_FF, D_MODEL), lambda b, s: (b * nt + s, 0)),
            pl.BlockSpec((FOX_HEADS, D_MODEL), lambda b, s: (0, 0)),
            pl.BlockSpec((FOX_HEADS, 1), lambda b, s: (0, 0)),
        ],
        out_specs=[
            pl.BlockSpec((1, 1, FOX_HEADS, TS_FF), lambda b, s: (b, s, 0, 0)),
            pl.BlockSpec((TS_FF, LANES), lambda b, s: (b * nt + s, 0)),
        ],
        out_shape=[
            jax.ShapeDtypeStruct((batch, nt, FOX_HEADS, TS_FF), F32),
            jax.ShapeDtypeStruct((batch * seq, LANES), F32),
        ],
        scratch_shapes=[pltpu.VMEM((FOX_HEADS, LANES), F32)],
        compiler_params=pltpu.CompilerParams(
            dimension_semantics=("arbitrary", "arbitrary"), vmem_limit_bytes=VMEM_LIMIT),
        name="forget_cumsum",
    )(h, w_ff_t, b_f)


class _Map(NamedTuple):
    slot: int
    blk: int
    key_bias: Optional[Callable[[Any], Any]]
    row_shift: Any


def _attention_maps(maps, qi, qm_scr, k_ref, v_ref, bufs, mb_scr, ls_scr, acc_scr, emit):
    row = lax.broadcasted_iota(jnp.int32, (TQ, TK), 0)
    col = lax.broadcasted_iota(jnp.int32, (TQ, TK), 1)
    causal = col <= row
    n_lane_blocks = TK // LANES

    def lanes(v, c):
        return v[:, c * LANES:(c + 1) * LANES]

    def chunk_rows(ref, mp, ki):
        start = pl.multiple_of(ki * TK, TK)
        return ref[pl.ds(start, TK), mp.blk * LANES:(mp.blk + 1) * LANES]

    def pass1_begin(buf):
        bufs[buf][1][...] = jnp.full((TQ, LANES), -jnp.inf, F32)

    def pass1_chunk(mp, buf, ki, diagonal):
        t_scr, mx_scr = bufs[buf]
        s = lax.dot_general(qm_scr[mp.slot], chunk_rows(k_ref, mp, ki),
                            (((1,), (1,)), ((), ())), preferred_element_type=F32)
        if mp.key_bias is not None:
            s = s - mp.key_bias(ki)
        if diagonal:
            s = jnp.where(causal, s, -jnp.inf)
        t_scr[ki] = s
        m = mx_scr[...]
        for c in range(n_lane_blocks):
            m = jnp.maximum(m, lanes(s, c))
        mx_scr[...] = m

    def pass2_begin(mp, buf):
        m = jnp.max(bufs[buf][1][...], axis=1, keepdims=True)
        if mp.row_shift is not None:
            m = (m + mp.row_shift) - mp.row_shift
        mb_scr[...] = jnp.broadcast_to(m, (TQ, LANES))
        ls_scr[...] = jnp.zeros((TQ, LANES), F32)
        acc_scr[...] = jnp.zeros((TQ, LANES), F32)

    def pass2_chunk(mp, buf, ki):
        t = bufs[buf][0][ki]
        mb = mb_scr[...]
        ps = [jnp.exp(lanes(t, c) - mb) for c in range(n_lane_blocks)]
        ls = ls_scr[...]
        for p in ps:
            ls = ls + p
        ls_scr[...] = ls
        p = jnp.concatenate(ps, axis=1).astype(BF16)
        acc_scr[...] += jnp.dot(p, chunk_rows(v_ref, mp, ki), preferred_element_type=F32)

    def pass2_end(n):
        l = jnp.sum(ls_scr[...], axis=1, keepdims=True)
        emit(n, acc_scr[...] / l)

    def loop(n_chunks, body):
        def step(ki, carry):
            body(ki)
            return carry
        lax.fori_loop(0, n_chunks, step, 0)

    pass1_begin(0)
    pass1_chunk(maps[0], 0, qi, True)
    loop(qi, lambda ki: pass1_chunk(maps[0], 0, ki, False))
    for n in range(1, len(maps)):
        buf, prev = n % 2, (n - 1) % 2
        pass2_begin(maps[n - 1], prev)
        pass1_begin(buf)
        pass1_chunk(maps[n], buf, qi, True)

        def both(ki, n=n, buf=buf, prev=prev):
            pass2_chunk(maps[n - 1], prev, ki)
            pass1_chunk(maps[n], buf, ki, False)

        loop(qi, both)
        pass2_chunk(maps[n - 1], prev, qi)
        pass2_end(n - 1)
    last = len(maps) - 1
    pass2_begin(maps[last], last % 2)
    loop(qi + 1, lambda ki: pass2_chunk(maps[last], last % 2, ki))
    pass2_end(last)


def _store_masked_queries(q_ref, qm_scr):
    first = lax.broadcasted_iota(jnp.int32, (TQ, LANES), 1) < HEAD_DIM
    for b in range(WIDTH // LANES):
        q = q_ref[:, b * LANES:(b + 1) * LANES]
        zero = jnp.zeros_like(q)
        qm_scr[2 * b] = jnp.where(first, q, zero)
        qm_scr[2 * b + 1] = jnp.where(first, zero, q)


def _fox_kernel(q_ref, k_ref, v_ref, g_ref, crow_ref, ccol_ref, o_ref,
                qm_scr, ta_scr, tb_scr, mxa_scr, mxb_scr, mb_scr, ls_scr, acc_scr, res_scr):
    qi = pl.program_id(1)
    first = lax.broadcasted_iota(jnp.int32, (TQ, LANES), 1) < HEAD_DIM
    _store_masked_queries(q_ref, qm_scr)
    ccol = ccol_ref[...]
    maps = [
        _Map(slot=head, blk=head // 2,
             key_bias=lambda ki, head=head: crow_ref[0, ki, head:head + 1, :],
             row_shift=ccol[:, head:head + 1])
        for head in range(FOX_HEADS)
    ]

    def emit(head, o):
        if head % 2 == 0:
            res_scr[...] = o
        else:
            blk = slice((head // 2) * LANES, (head // 2 + 1) * LANES)
            y = jnp.where(first, res_scr[...], o) * g_ref[:, blk].astype(F32)
            o_ref[:, blk] = y.astype(BF16)

    _attention_maps(maps, qi, qm_scr, k_ref, v_ref, ((ta_scr, mxa_scr), (tb_scr, mxb_scr)),
                    mb_scr, ls_scr, acc_scr, emit)


def _diff_kernel(q_ref, k_ref, v_ref, g_ref, lam_ref, ng_ref, o_ref,
                 qm_scr, ta_scr, tb_scr, mxa_scr, mxb_scr, mb_scr, ls_scr, acc_scr, res_scr,
                 *, lam_init):
    qi = pl.program_id(1)
    _store_masked_queries(q_ref, qm_scr)
    lv = lam_ref[...]
    lam = (jnp.exp(jnp.sum(lv[0:1] * lv[1:2], axis=1, keepdims=True))
           - jnp.exp(jnp.sum(lv[2:3] * lv[3:4], axis=1, keepdims=True)) + lam_init)
    maps = [_Map(slot=n, blk=n // 2, key_bias=None, row_shift=None)
            for n in range(2 * DIFF_HEADS)]

    def emit(n, o):
        if n % 2 == 0:
            res_scr[...] = o
        else:
            blk = slice((n // 2) * LANES, (n // 2 + 1) * LANES)
            d = res_scr[...] - lam * o
            ms = jnp.mean(d * d, axis=-1, keepdims=True)
            y = d * lax.rsqrt(ms + RMS_EPS) * ng_ref[...] * (1.0 - lam_init)
            o_ref[:, blk] = (y * g_ref[:, blk].astype(F32)).astype(BF16)

    _attention_maps(maps, qi, qm_scr, k_ref, v_ref, ((ta_scr, mxa_scr), (tb_scr, mxb_scr)),
                    mb_scr, ls_scr, acc_scr, emit)


def _attn_scratch(seq):
    stat = pltpu.VMEM((TQ, LANES), F32)
    logits = pltpu.VMEM((seq // TK, TQ, TK), F32)
    return [pltpu.VMEM((2 * WIDTH // LANES, TQ, LANES), BF16),
            logits, logits, stat, stat,
            stat, stat, stat, stat]


def _fox_attention(pa, ps, crow, ccol, batch, seq):
    nq = seq // TQ
    return pl.pallas_call(
        _fox_kernel,
        grid=(batch, nq),
        in_specs=[
            pl.BlockSpec((TQ, WIDTH), lambda b, i: (b * nq + i, A_FQ)),
            pl.BlockSpec((seq, WIDTH), lambda b, i: (b, A_FK)),
            pl.BlockSpec((seq, WIDTH), lambda b, i: (b, A_FV)),
            pl.BlockSpec((TQ, WIDTH), lambda b, i: (b * nq + i, S_FG)),
            pl.BlockSpec((1, seq // TK, FOX_HEADS, TK), lambda b, i: (b, 0, 0, 0)),
            pl.BlockSpec((TQ, LANES), lambda b, i: (b * nq + i, 0)),
        ],
        out_specs=pl.BlockSpec((TQ, WIDTH), lambda b, i: (b * nq + i, 0)),
        out_shape=jax.ShapeDtypeStruct((batch * seq, WIDTH), BF16),
        scratch_shapes=_attn_scratch(seq),
        compiler_params=pltpu.CompilerParams(
            dimension_semantics=("arbitrary", "arbitrary"), vmem_limit_bytes=VMEM_LIMIT),
        name="fox_attention",
    )(pa, pa, pa, ps, crow, ccol)


def _diff_attention(pr, pa, ps, lam_vecs, norm_g, lam_init, batch, seq):
    nq = seq // TQ
    return pl.pallas_call(
        functools.partial(_diff_kernel, lam_init=lam_init),
        grid=(batch, nq),
        in_specs=[
            pl.BlockSpec((TQ, WIDTH), lambda b, i: (b * nq + i, R_DQ)),
            pl.BlockSpec((seq, WIDTH), lambda b, i: (b, R_DK)),
            pl.BlockSpec((seq, WIDTH), lambda b, i: (b, A_DV)),
            pl.BlockSpec((TQ, WIDTH), lambda b, i: (b * nq + i, S_DG)),
            pl.BlockSpec((4, HEAD_DIM), lambda b, i: (0, 0)),
            pl.BlockSpec((1, LANES), lambda b, i: (0, 0)),
        ],
        out_specs=pl.BlockSpec((TQ, WIDTH), lambda b, i: (b * nq + i, 0)),
        out_shape=jax.ShapeDtypeStruct((batch * seq, WIDTH), BF16),
        scratch_shapes=_attn_scratch(seq),
        compiler_params=pltpu.CompilerParams(
            dimension_semantics=("arbitrary", "arbitrary"), vmem_limit_bytes=VMEM_LIMIT),
        name="diff_attention",
    )(pr, pr, pa, ps, lam_vecs, norm_g)


def _merge_kernel(x_ref, yf_ref, yd_ref, cb_ref, cc_ref, cx_ref, cg_ref, hc_ref, hx_ref,
                  g0_ref, g1_ref, g2_ref, cw_ref, wf_ref, wd_ref, wc_ref, wo_ref, pg_ref,
                  o_ref, ext_scr, *, tiles_per_seq):
    i = pl.program_id(0)
    halo = SUBLANES
    u = cc_ref[...].astype(F32) * cx_ref[...].astype(F32)
    prev = hc_ref[...].astype(F32) * hx_ref[...].astype(F32)
    prev = jnp.where(i % tiles_per_seq == 0, jnp.zeros_like(prev), prev)
    ext_scr[0:halo, :] = prev
    ext_scr[halo:halo + TM_OUT, :] = u
    cw = cw_ref[...]
    conv = (cw[2:3] * u + cw[1:2] * ext_scr[halo - 1:halo - 1 + TM_OUT, :]
            + cw[0:1] * ext_scr[halo - 2:halo - 2 + TM_OUT, :])
    y_conv = (cb_ref[...].astype(F32) * conv * cg_ref[...].astype(F32)).astype(BF16)

    m = (g0_ref[...].astype(F32) * jnp.dot(yf_ref[...], wf_ref[...], preferred_element_type=F32)
         + g1_ref[...].astype(F32) * jnp.dot(yd_ref[...], wd_ref[...], preferred_element_type=F32)
         + g2_ref[...].astype(F32) * jnp.dot(y_conv, wc_ref[...], preferred_element_type=F32))
    o = jnp.dot(m.astype(BF16), wo_ref[...], preferred_element_type=F32)
    ms = jnp.mean(o * o, axis=-1, keepdims=True)
    o_ref[...] = x_ref[...] + o * lax.rsqrt(ms + RMS_EPS) * pg_ref[...]


def _merge(x2, y_fox, y_diff, pa, ps, pg, conv_w, w_fox, w_diff, w_conv, w_out, post_g, seq):
    t = x2.shape[0]
    tiles_per_seq = seq // TM_OUT
    halo_blocks = TM_OUT // SUBLANES

    def tile(col):
        return pl.BlockSpec((TM_OUT, WIDTH), lambda i: (i, col))

    def halo(col):
        return pl.BlockSpec((SUBLANES, WIDTH),
                            lambda i: (jnp.maximum(i * halo_blocks - 1, 0), col))

    def whole(shape):
        return pl.BlockSpec(shape, lambda i: (0, 0))

    return pl.pallas_call(
        functools.partial(_merge_kernel, tiles_per_seq=tiles_per_seq),
        grid=(t // TM_OUT,),
        in_specs=[
            pl.BlockSpec((TM_OUT, D_MODEL), lambda i: (i, 0)),
            pl.BlockSpec((TM_OUT, WIDTH), lambda i: (i, 0)),
            pl.BlockSpec((TM_OUT, WIDTH), lambda i: (i, 0)),
            tile(A_CB), tile(A_CC), tile(A_CX), tile(S_CG),
            halo(A_CC), halo(A_CX),
            pl.BlockSpec((TM_OUT, D_MODEL), lambda i: (i, 0)),
            pl.BlockSpec((TM_OUT, D_MODEL), lambda i: (i, 1)),
            pl.BlockSpec((TM_OUT, D_MODEL), lambda i: (i, 2)),
            whole((CONV_K, WIDTH)),
            whole((WIDTH, D_MODEL)), whole((WIDTH, D_MODEL)), whole((WIDTH, D_MODEL)),
            whole((D_MODEL, D_MODEL)),
            whole((1, D_MODEL)),
        ],
        out_specs=pl.BlockSpec((TM_OUT, D_MODEL), lambda i: (i, 0)),
        out_shape=jax.ShapeDtypeStruct((t, D_MODEL), F32),
        scratch_shapes=[pltpu.VMEM((TM_OUT + SUBLANES, WIDTH), F32)],
        compiler_params=pltpu.CompilerParams(
            dimension_semantics=("arbitrary",), vmem_limit_bytes=VMEM_LIMIT),
        name="merge",
    )(x2, y_fox, y_diff, pa, pa, pa, ps, pa, pa, pg, pg, pg,
      conv_w, w_fox, w_diff, w_conv, w_out, post_g)


def _rope_lane_tables(positions):
    half = ROT_DIM // 2
    inv_freq = ROPE_THETA ** (-jnp.arange(0, ROT_DIM, 2, dtype=F32) / ROT_DIM)
    ang = positions.astype(F32).reshape(-1, 1) * inv_freq
    cos, sin = jnp.cos(ang), jnp.sin(ang)
    t = ang.shape[0]
    ones = jnp.ones((t, HEAD_DIM - ROT_DIM), F32)
    zeros_rest = jnp.zeros((t, HEAD_DIM - ROT_DIM), F32)
    zeros_half = jnp.zeros((t, half), F32)
    c64 = jnp.concatenate([cos, cos, ones], axis=1)
    s1_64 = jnp.concatenate([-sin, zeros_half, zeros_rest], axis=1)
    s2_64 = jnp.concatenate([zeros_half, sin, zeros_rest], axis=1)
    rep = LANES // HEAD_DIM
    return jnp.tile(c64, (1, rep)), jnp.tile(s1_64, (1, rep)), jnp.tile(s2_64, (1, rep))


def _gather_cols(w, offsets):
    return jnp.concatenate([w[:, o:o + WIDTH] for o in offsets], axis=1).astype(BF16)


def _scale_vec(n_blocks, scaled_block):
    blk = jnp.arange(n_blocks * WIDTH, dtype=jnp.int32) // WIDTH
    return jnp.where(blk == scaled_block, HEAD_DIM ** -0.5, 1.0).astype(F32).reshape(1, -1)


def kernel(x, positions, pre_norm_g, w_in, b_forget, b_merge, conv_w, lam_q1, lam_k1, lam_q2,
           lam_k2, diff_norm_g, w_br_fox, w_br_diff, w_br_conv, w_out, post_norm_g):
    batch, seq, _ = x.shape
    depth = w_in.shape[0]
    assert seq % TQ == 0 and TQ == TK == TS_FF and (batch * seq) % TM_IN == 0
    rc, rs1, rs2 = _rope_lane_tables(positions)
    x2 = x.reshape(batch * seq, D_MODEL)
    for l in range(depth):
        lam_init = 0.8 - 0.6 * math.exp(-0.3 * l)
        w = w_in[l]
        w_plain = _gather_cols(w, [_OFF_FQ, _OFF_FK, _OFF_FV, _OFF_DV, _OFF_CB, _OFF_CC, _OFF_CX])
        w_silu = _gather_cols(w, [_OFF_FG, _OFF_DG, _OFF_CG])
        w_rot = _gather_cols(w, [_OFF_DQ, _OFF_DK])
        w_gate = w[:, _OFF_MG:_OFF_MG + N_BRANCH * D_MODEL].astype(BF16)
        w_ff_t = w[:, _OFF_FF:_OFF_FF + FOX_HEADS].T.astype(BF16)
        pa, h = _plain_proj(x2, pre_norm_g[l].reshape(1, D_MODEL), w_plain,
                            _scale_vec(7, A_FQ), TN_PLAIN)
        ps = _epilogue_proj(_silu_proj_kernel, "proj_silu", h, w_silu, TN_SILU, [], [])
        pr = _epilogue_proj(_rot_proj_kernel, "proj_rotary", h, w_rot, TN_ROT,
                            [_scale_vec(2, R_DQ), rc, rs1, rs2],
                            [_col_vec(TN_ROT), _ROPE_TILE, _ROPE_TILE, _ROPE_TILE])
        pg = _epilogue_proj(_gate_proj_kernel, "proj_gate", h, w_gate, TN_GATE,
                            [b_merge[l].reshape(1, N_BRANCH * D_MODEL)], [_col_vec(TN_GATE)])
        crow, ccol = _forget_cumsum(h, w_ff_t, b_forget[l].reshape(FOX_HEADS, 1), batch, seq)
        y_fox = _fox_attention(pa, ps, crow, ccol, batch, seq)
        lam_vecs = jnp.stack([lam_q1[l], lam_k1[l], lam_q2[l], lam_k2[l]])
        y_diff = _diff_attention(pr, pa, ps, lam_vecs, diff_norm_g[l].reshape(1, LANES),
                                 lam_init, batch, seq)
        x2 = _merge(x2, y_fox, y_diff, pa, ps, pg, conv_w[l],
                    w_br_fox[l].astype(BF16), w_br_diff[l].astype(BF16),
                    w_br_conv[l].astype(BF16), w_out[l].astype(BF16),
                    post_norm_g[l].reshape(1, D_MODEL), seq)
    return x2.reshape(batch, seq, D_MODEL)
```

```python
import functools
import math
from typing import Any, Callable, NamedTuple, Optional

import jax
import jax.numpy as jnp
from jax import lax
from jax.experimental import pallas as pl
from jax.experimental.pallas import tpu as pltpu

F32 = jnp.float32
BF16 = jnp.bfloat16

D_MODEL = 1024
HEAD_DIM = 64
FOX_HEADS = 8
DIFF_HEADS = 4
WIDTH = 512
CONV_K = 3
N_BRANCH = 3
ROPE_THETA = 500000.0
ROT_DIM = HEAD_DIM // 4
RMS_EPS = 1e-6
LANES = 128
SUBLANES = 8

(A_FQ, A_FK, A_FV, A_DV, A_CB, A_CC, A_CX) = range(7)
(S_FG, S_DG, S_CG) = range(3)
(R_DQ, R_DK) = range(2)
N_GATE_BLOCKS = N_BRANCH * D_MODEL // WIDTH

_OFF_FQ, _OFF_FK, _OFF_FV = 0, 512, 1024
_OFF_FF = 1536
_OFF_FG = 1544
_OFF_DQ, _OFF_DK, _OFF_DV, _OFF_DG = 2056, 2568, 3080, 3592
_OFF_CB, _OFF_CC, _OFF_CX, _OFF_CG = 4104, 4616, 5128, 5640
_OFF_MG = 6152

TM_IN = 1024
TN_PLAIN = 1792
TN_SILU = 1536
TN_ROT = 1024
TN_GATE = 1536
TS_FF = 512
TQ = 512
TK = 512
TM_OUT = 512
VMEM_LIMIT = 48 * 1024 * 1024


def _silu(v):
    return v * jax.nn.sigmoid(v)


def _rotary_block(acc, c, s1, s2):
    outs = []
    for i in range(acc.shape[1] // LANES):
        v = acc[:, i * LANES:(i + 1) * LANES]
        outs.append(v * c + pltpu.roll(v, LANES - ROT_DIM // 2, 1) * s1
                    + pltpu.roll(v, ROT_DIM // 2, 1) * s2)
    return jnp.concatenate(outs, axis=1)


def _plain_proj_kernel(x_ref, g_ref, w_ref, sc_ref, o_ref, h_ref, h_scr):
    @pl.when(pl.program_id(1) == 0)
    def _():
        x = x_ref[...]
        ms = jnp.mean(x * x, axis=-1, keepdims=True)
        h = (x * lax.rsqrt(ms + RMS_EPS) * g_ref[...]).astype(BF16)
        h_scr[...] = h
        h_ref[...] = h

    acc = jnp.dot(h_scr[...], w_ref[...], preferred_element_type=F32)
    o_ref[...] = (acc * sc_ref[...]).astype(BF16)


def _silu_proj_kernel(h_ref, w_ref, o_ref):
    acc = jnp.dot(h_ref[...], w_ref[...], preferred_element_type=F32)
    o_ref[...] = _silu(acc).astype(BF16)


def _rot_proj_kernel(h_ref, w_ref, sc_ref, rc_ref, rs1_ref, rs2_ref, o_ref):
    acc = jnp.dot(h_ref[...], w_ref[...], preferred_element_type=F32)
    r = _rotary_block(acc, rc_ref[...], rs1_ref[...], rs2_ref[...])
    o_ref[...] = (r * sc_ref[...]).astype(BF16)


def _gate_proj_kernel(h_ref, w_ref, b_ref, o_ref):
    acc = jnp.dot(h_ref[...], w_ref[...], preferred_element_type=F32)
    o_ref[...] = jax.nn.sigmoid(acc + b_ref[...]).astype(BF16)


_PROJ_PARAMS = pltpu.CompilerParams(
    dimension_semantics=("arbitrary", "arbitrary"), vmem_limit_bytes=VMEM_LIMIT)
_ROW_TILE = pl.BlockSpec((TM_IN, D_MODEL), lambda i, j: (i, 0))
_ROPE_TILE = pl.BlockSpec((TM_IN, LANES), lambda i, j: (i, 0))


def _w_block(tn):
    return pl.BlockSpec((D_MODEL, tn), lambda i, j: (0, j))


def _col_vec(tn):
    return pl.BlockSpec((1, tn), lambda i, j: (0, j))


def _out_block(tn):
    return pl.BlockSpec((TM_IN, tn), lambda i, j: (i, j))


def _plain_proj(x2, pre_g, w, col_scale, tn):
    t, n = x2.shape[0], w.shape[1]
    return pl.pallas_call(
        _plain_proj_kernel,
        grid=(t // TM_IN, n // tn),
        in_specs=[_ROW_TILE, pl.BlockSpec((1, D_MODEL), lambda i, j: (0, 0)),
                  _w_block(tn), _col_vec(tn)],
        out_specs=[_out_block(tn), _ROW_TILE],
        out_shape=[jax.ShapeDtypeStruct((t, n), BF16), jax.ShapeDtypeStruct((t, D_MODEL), BF16)],
        scratch_shapes=[pltpu.VMEM((TM_IN, D_MODEL), BF16)],
        compiler_params=_PROJ_PARAMS,
        name="proj_plain",
    )(x2, pre_g, w, col_scale)


def _epilogue_proj(body, name, h, w, tn, extra, extra_specs):
    t, n = h.shape[0], w.shape[1]
    return pl.pallas_call(
        body,
        grid=(t // TM_IN, n // tn),
        in_specs=[_ROW_TILE, _w_block(tn)] + extra_specs,
        out_specs=_out_block(tn),
        out_shape=jax.ShapeDtypeStruct((t, n), BF16),
        compiler_params=_PROJ_PARAMS,
        name=name,
    )(h, w, *extra)


def _split3(v):
    hi = v.astype(BF16)
    r1 = v - hi.astype(F32)
    mid = r1.astype(BF16)
    lo = (r1 - mid.astype(F32)).astype(BF16)
    return hi, mid, lo


def _forget_kernel(h_ref, wt_ref, bf_ref, crow_ref, ccol_ref, carry_scr):
    si = pl.program_id(1)

    @pl.when(si == 0)
    def _():
        carry_scr[...] = jnp.zeros_like(carry_scr)

    ff = lax.dot_general(wt_ref[...], h_ref[...], (((1,), (1,)), ((), ())),
                         preferred_element_type=F32)
    lf = jax.nn.log_sigmoid(ff + bf_ref[...])
    row = lax.broadcasted_iota(jnp.int32, (TS_FF, TS_FF), 0)
    col = lax.broadcasted_iota(jnp.int32, (TS_FF, TS_FF), 1)
    upper = (row <= col).astype(BF16)
    cum = jnp.zeros((FOX_HEADS, TS_FF), F32)
    for piece in _split3(lf):
        cum = cum + jnp.dot(piece, upper, preferred_element_type=F32)
    c = cum + carry_scr[:, 0:1]
    carry_scr[...] = jnp.broadcast_to(c[:, TS_FF - 1:TS_FF], carry_scr.shape)
    crow_ref[0, 0] = c
    padded = jnp.concatenate([c, jnp.zeros((LANES - FOX_HEADS, TS_FF), F32)], axis=0)
    ccol_ref[...] = padded.T


def _forget_cumsum(h, w_ff_t, b_f, batch, seq):
    nt = seq // TS_FF
    return pl.pallas_call(
        _forget_kernel,
        grid=(batch, nt),
        in_specs=[
            pl.BlockSpec((TS_FF, D_MODEL), lambda b, s: (b * nt + s, 0)),
            pl.BlockSpec((FOX_HEADS, D_MODEL), lambda b, s: (0, 0)),
            pl.BlockSpec((FOX_HEADS, 1), lambda b, s: (0, 0)),
        ],
        out_specs=[
            pl.BlockSpec((1, 1, FOX_HEADS, TS_FF), lambda b, s: (b, s, 0, 0)),
            pl.BlockSpec((TS_FF, LANES), lambda b, s: (b * nt + s, 0)),
        ],
        out_shape=[
            jax.ShapeDtypeStruct((batch, nt, FOX_HEADS, TS_FF), F32),
            jax.ShapeDtypeStruct((batch * seq, LANES), F32),
        ],
        scratch_shapes=[pltpu.VMEM((FOX_HEADS, LANES), F32)],
        compiler_params=pltpu.CompilerParams(
            dimension_semantics=("arbitrary", "arbitrary"), vmem_limit_bytes=VMEM_LIMIT),
        name="forget_cumsum",
    )(h, w_ff_t, b_f)


class _Map(NamedTuple):
    slot: int
    blk: int
    key_bias: Optional[Callable[[Any], Any]]
    row_shift: Any


def _attention_maps(maps, qi, qm_scr, k_ref, v_ref, bufs, mb_scr, ls_scr, acc_scr, emit):
    row = lax.broadcasted_iota(jnp.int32, (TQ, TK), 0)
    col = lax.broadcasted_iota(jnp.int32, (TQ, TK), 1)
    causal = col <= row
    n_lane_blocks = TK // LANES

    def lanes(v, c):
        return v[:, c * LANES:(c + 1) * LANES]

    def chunk_rows(ref, mp, ki):
        start = pl.multiple_of(ki * TK, TK)
        return ref[pl.ds(start, TK), mp.blk * LANES:(mp.blk + 1) * LANES]

    def pass1_begin(buf):
        bufs[buf][1][...] = jnp.full((TQ, LANES), -jnp.inf, F32)

    def pass1_chunk(mp, buf, ki, diagonal):
        t_scr, mx_scr = bufs[buf]
        s = lax.dot_general(qm_scr[mp.slot], chunk_rows(k_ref, mp, ki),
                            (((1,), (1,)), ((), ())), preferred_element_type=F32)
        if mp.key_bias is not None:
            s = s - mp.key_bias(ki)
        if diagonal:
            s = jnp.where(causal, s, -jnp.inf)
        t_scr[ki] = s
        m = mx_scr[...]
        for c in range(n_lane_blocks):
            m = jnp.maximum(m, lanes(s, c))
        mx_scr[...] = m

    def pass2_begin(mp, buf):
        m = jnp.max(bufs[buf][1][...], axis=1, keepdims=True)
        if mp.row_shift is not None:
            m = (m + mp.row_shift) - mp.row_shift
        mb_scr[...] = jnp.broadcast_to(m, (TQ, LANES))
        ls_scr[...] = jnp.zeros((TQ, LANES), F32)
        acc_scr[...] = jnp.zeros((TQ, LANES), F32)

    def pass2_chunk(mp, buf, ki):
        t = bufs[buf][0][ki]
        mb = mb_scr[...]
        ps = [jnp.exp(lanes(t, c) - mb) for c in range(n_lane_blocks)]
        ls = ls_scr[...]
        for p in ps:
            ls = ls + p
        ls_scr[...] = ls
        p = jnp.concatenate(ps, axis=1).astype(BF16)
        acc_scr[...] += jnp.dot(p, chunk_rows(v_ref, mp, ki), preferred_element_type=F32)

    def pass2_end(n):
        l = jnp.sum(ls_scr[...], axis=1, keepdims=True)
        emit(n, acc_scr[...] / l)

    def loop(n_chunks, body):
        def step(i, carry):
            body(2 * i)
            body(2 * i + 1)
            return carry
        lax.fori_loop(0, lax.shift_right_logical(n_chunks, 1), step, 0)

        @pl.when(lax.bitwise_and(n_chunks, 1) == 1)
        def _():
            body(n_chunks - 1)

    pass1_begin(0)
    pass1_chunk(maps[0], 0, qi, True)
    loop(qi, lambda ki: pass1_chunk(maps[0], 0, ki, False))
    for n in range(1, len(maps)):
        buf, prev = n % 2, (n - 1) % 2
        pass2_begin(maps[n - 1], prev)
        pass1_begin(buf)
        pass1_chunk(maps[n], buf, qi, True)

        def both(ki, n=n, buf=buf, prev=prev):
            pass2_chunk(maps[n - 1], prev, ki)
            pass1_chunk(maps[n], buf, ki, False)

        loop(qi, both)
        pass2_chunk(maps[n - 1], prev, qi)
        pass2_end(n - 1)
    last = len(maps) - 1
    pass2_begin(maps[last], last % 2)
    loop(qi + 1, lambda ki: pass2_chunk(maps[last], last % 2, ki))
    pass2_end(last)


def _store_masked_queries(q_ref, qm_scr):
    first = lax.broadcasted_iota(jnp.int32, (TQ, LANES), 1) < HEAD_DIM
    for b in range(WIDTH // LANES):
        q = q_ref[:, b * LANES:(b + 1) * LANES]
        zero = jnp.zeros_like(q)
        qm_scr[2 * b] = jnp.where(first, q, zero)
        qm_scr[2 * b + 1] = jnp.where(first, zero, q)


def _fox_kernel(q_ref, k_ref, v_ref, g_ref, crow_ref, ccol_ref, o_ref,
                qm_scr, ta_scr, tb_scr, mxa_scr, mxb_scr, mb_scr, ls_scr, acc_scr, res_scr):
    qi = pl.program_id(1)
    first = lax.broadcasted_iota(jnp.int32, (TQ, LANES), 1) < HEAD_DIM
    _store_masked_queries(q_ref, qm_scr)
    ccol = ccol_ref[...]
    maps = [
        _Map(slot=head, blk=head // 2,
             key_bias=lambda ki, head=head: crow_ref[0, ki, head:head + 1, :],
             row_shift=ccol[:, head:head + 1])
        for head in range(FOX_HEADS)
    ]

    def emit(head, o):
        if head % 2 == 0:
            res_scr[...] = o
        else:
            blk = slice((head // 2) * LANES, (head // 2 + 1) * LANES)
            y = jnp.where(first, res_scr[...], o) * g_ref[:, blk].astype(F32)
            o_ref[:, blk] = y.astype(BF16)

    _attention_maps(maps, qi, qm_scr, k_ref, v_ref, ((ta_scr, mxa_scr), (tb_scr, mxb_scr)),
                    mb_scr, ls_scr, acc_scr, emit)


def _diff_kernel(q_ref, k_ref, v_ref, g_ref, lam_ref, ng_ref, o_ref,
                 qm_scr, ta_scr, tb_scr, mxa_scr, mxb_scr, mb_scr, ls_scr, acc_scr, res_scr,
                 *, lam_init):
    qi = pl.program_id(1)
    _store_masked_queries(q_ref, qm_scr)
    lv = lam_ref[...]
    lam = (jnp.exp(jnp.sum(lv[0:1] * lv[1:2], axis=1, keepdims=True))
           - jnp.exp(jnp.sum(lv[2:3] * lv[3:4], axis=1, keepdims=True)) + lam_init)
    maps = [_Map(slot=n, blk=n // 2, key_bias=None, row_shift=None)
            for n in range(2 * DIFF_HEADS)]

    def emit(n, o):
        if n % 2 == 0:
            res_scr[...] = o
        else:
            blk = slice((n // 2) * LANES, (n // 2 + 1) * LANES)
            d = res_scr[...] - lam * o
            ms = jnp.mean(d * d, axis=-1, keepdims=True)
            y = d * lax.rsqrt(ms + RMS_EPS) * ng_ref[...] * (1.0 - lam_init)
            o_ref[:, blk] = (y * g_ref[:, blk].astype(F32)).astype(BF16)

    _attention_maps(maps, qi, qm_scr, k_ref, v_ref, ((ta_scr, mxa_scr), (tb_scr, mxb_scr)),
                    mb_scr, ls_scr, acc_scr, emit)


def _attn_scratch(seq):
    stat = pltpu.VMEM((TQ, LANES), F32)
    logits = pltpu.VMEM((seq // TK, TQ, TK), F32)
    return [pltpu.VMEM((2 * WIDTH // LANES, TQ, LANES), BF16),
            logits, logits, stat, stat,
            stat, stat, stat, stat]


def _fox_attention(pa, ps, crow, ccol, batch, seq):
    nq = seq // TQ
    return pl.pallas_call(
        _fox_kernel,
        grid=(batch, nq),
        in_specs=[
            pl.BlockSpec((TQ, WIDTH), lambda b, i: (b * nq + i, A_FQ)),
            pl.BlockSpec((seq, WIDTH), lambda b, i: (b, A_FK)),
            pl.BlockSpec((seq, WIDTH), lambda b, i: (b, A_FV)),
            pl.BlockSpec((TQ, WIDTH), lambda b, i: (b * nq + i, S_FG)),
            pl.BlockSpec((1, seq // TK, FOX_HEADS, TK), lambda b, i: (b, 0, 0, 0)),
            pl.BlockSpec((TQ, LANES), lambda b, i: (b * nq + i, 0)),
        ],
        out_specs=pl.BlockSpec((TQ, WIDTH), lambda b, i: (b * nq + i, 0)),
        out_shape=jax.ShapeDtypeStruct((batch * seq, WIDTH), BF16),
        scratch_shapes=_attn_scratch(seq),
        compiler_params=pltpu.CompilerParams(
            dimension_semantics=("arbitrary", "arbitrary"), vmem_limit_bytes=VMEM_LIMIT),
        name="fox_attention",
    )(pa, pa, pa, ps, crow, ccol)


def _diff_attention(pr, pa, ps, lam_vecs, norm_g, lam_init, batch, seq):
    nq = seq // TQ
    return pl.pallas_call(
        functools.partial(_diff_kernel, lam_init=lam_init),
        grid=(batch, nq),
        in_specs=[
            pl.BlockSpec((TQ, WIDTH), lambda b, i: (b * nq + i, R_DQ)),
            pl.BlockSpec((seq, WIDTH), lambda b, i: (b, R_DK)),
            pl.BlockSpec((seq, WIDTH), lambda b, i: (b, A_DV)),
            pl.BlockSpec((TQ, WIDTH), lambda b, i: (b * nq + i, S_DG)),
            pl.BlockSpec((4, HEAD_DIM), lambda b, i: (0, 0)),
            pl.BlockSpec((1, LANES), lambda b, i: (0, 0)),
        ],
        out_specs=pl.BlockSpec((TQ, WIDTH), lambda b, i: (b * nq + i, 0)),
        out_shape=jax.ShapeDtypeStruct((batch * seq, WIDTH), BF16),
        scratch_shapes=_attn_scratch(seq),
        compiler_params=pltpu.CompilerParams(
            dimension_semantics=("arbitrary", "arbitrary"), vmem_limit_bytes=VMEM_LIMIT),
        name="diff_attention",
    )(pr, pr, pa, ps, lam_vecs, norm_g)


def _merge_kernel(x_ref, yf_ref, yd_ref, cb_ref, cc_ref, cx_ref, cg_ref, hc_ref, hx_ref,
                  g0_ref, g1_ref, g2_ref, cw_ref, wf_ref, wd_ref, wc_ref, wo_ref, pg_ref,
                  o_ref, ext_scr, *, tiles_per_seq):
    i = pl.program_id(0)
    halo = SUBLANES
    u = cc_ref[...].astype(F32) * cx_ref[...].astype(F32)
    prev = hc_ref[...].astype(F32) * hx_ref[...].astype(F32)
    prev = jnp.where(i % tiles_per_seq == 0, jnp.zeros_like(prev), prev)
    ext_scr[0:halo, :] = prev
    ext_scr[halo:halo + TM_OUT, :] = u
    cw = cw_ref[...]
    conv = (cw[2:3] * u + cw[1:2] * ext_scr[halo - 1:halo - 1 + TM_OUT, :]
            + cw[0:1] * ext_scr[halo - 2:halo - 2 + TM_OUT, :])
    y_conv = (cb_ref[...].astype(F32) * conv * cg_ref[...].astype(F32)).astype(BF16)

    m = (g0_ref[...].astype(F32) * jnp.dot(yf_ref[...], wf_ref[...], preferred_element_type=F32)
         + g1_ref[...].astype(F32) * jnp.dot(yd_ref[...], wd_ref[...], preferred_element_type=F32)
         + g2_ref[...].astype(F32) * jnp.dot(y_conv, wc_ref[...], preferred_element_type=F32))
    o = jnp.dot(m.astype(BF16), wo_ref[...], preferred_element_type=F32)
    ms = jnp.mean(o * o, axis=-1, keepdims=True)
    o_ref[...] = x_ref[...] + o * lax.rsqrt(ms + RMS_EPS) * pg_ref[...]


def _merge(x2, y_fox, y_diff, pa, ps, pg, conv_w, w_fox, w_diff, w_conv, w_out, post_g, seq):
    t = x2.shape[0]
    tiles_per_seq = seq // TM_OUT
    halo_blocks = TM_OUT // SUBLANES

    def tile(col):
        return pl.BlockSpec((TM_OUT, WIDTH), lambda i: (i, col))

    def halo(col):
        return pl.BlockSpec((SUBLANES, WIDTH),
                            lambda i: (jnp.maximum(i * halo_blocks - 1, 0), col))

    def whole(shape):
        return pl.BlockSpec(shape, lambda i: (0, 0))

    return pl.pallas_call(
        functools.partial(_merge_kernel, tiles_per_seq=tiles_per_seq),
        grid=(t // TM_OUT,),
        in_specs=[
            pl.BlockSpec((TM_OUT, D_MODEL), lambda i: (i, 0)),
            pl.BlockSpec((TM_OUT, WIDTH), lambda i: (i, 0)),
            pl.BlockSpec((TM_OUT, WIDTH), lambda i: (i, 0)),
            tile(A_CB), tile(A_CC), tile(A_CX), tile(S_CG),
            halo(A_CC), halo(A_CX),
            pl.BlockSpec((TM_OUT, D_MODEL), lambda i: (i, 0)),
            pl.BlockSpec((TM_OUT, D_MODEL), lambda i: (i, 1)),
            pl.BlockSpec((TM_OUT, D_MODEL), lambda i: (i, 2)),
            whole((CONV_K, WIDTH)),
            whole((WIDTH, D_MODEL)), whole((WIDTH, D_MODEL)), whole((WIDTH, D_MODEL)),
            whole((D_MODEL, D_MODEL)),
            whole((1, D_MODEL)),
        ],
        out_specs=pl.BlockSpec((TM_OUT, D_MODEL), lambda i: (i, 0)),
        out_shape=jax.ShapeDtypeStruct((t, D_MODEL), F32),
        scratch_shapes=[pltpu.VMEM((TM_OUT + SUBLANES, WIDTH), F32)],
        compiler_params=pltpu.CompilerParams(
            dimension_semantics=("arbitrary",), vmem_limit_bytes=VMEM_LIMIT),
        name="merge",
    )(x2, y_fox, y_diff, pa, pa, pa, ps, pa, pa, pg, pg, pg,
      conv_w, w_fox, w_diff, w_conv, w_out, post_g)


def _rope_lane_tables(positions):
    half = ROT_DIM // 2
    inv_freq = ROPE_THETA ** (-jnp.arange(0, ROT_DIM, 2, dtype=F32) / ROT_DIM)
    ang = positions.astype(F32).reshape(-1, 1) * inv_freq
    cos, sin = jnp.cos(ang), jnp.sin(ang)
    t = ang.shape[0]
    ones = jnp.ones((t, HEAD_DIM - ROT_DIM), F32)
    zeros_rest = jnp.zeros((t, HEAD_DIM - ROT_DIM), F32)
    zeros_half = jnp.zeros((t, half), F32)
    c64 = jnp.concatenate([cos, cos, ones], axis=1)
    s1_64 = jnp.concatenate([-sin, zeros_half, zeros_rest], axis=1)
    s2_64 = jnp.concatenate([zeros_half, sin, zeros_rest], axis=1)
    rep = LANES // HEAD_DIM
    return jnp.tile(c64, (1, rep)), jnp.tile(s1_64, (1, rep)), jnp.tile(s2_64, (1, rep))


def _gather_cols(w, offsets):
    return jnp.concatenate([w[:, o:o + WIDTH] for o in offsets], axis=1).astype(BF16)


def _scale_vec(n_blocks, scaled_block):
    blk = jnp.arange(n_blocks * WIDTH, dtype=jnp.int32) // WIDTH
    return jnp.where(blk == scaled_block, HEAD_DIM ** -0.5, 1.0).astype(F32).reshape(1, -1)


def kernel(x, positions, pre_norm_g, w_in, b_forget, b_merge, conv_w, lam_q1, lam_k1, lam_q2,
           lam_k2, diff_norm_g, w_br_fox, w_br_diff, w_br_conv, w_out, post_norm_g):
    batch, seq, _ = x.shape
    depth = w_in.shape[0]
    assert seq % TQ == 0 and TQ == TK == TS_FF and (batch * seq) % TM_IN == 0
    rc, rs1, rs2 = _rope_lane_tables(positions)
    x2 = x.reshape(batch * seq, D_MODEL)
    for l in range(depth):
        lam_init = 0.8 - 0.6 * math.exp(-0.3 * l)
        w = w_in[l]
        w_plain = _gather_cols(w, [_OFF_FQ, _OFF_FK, _OFF_FV, _OFF_DV, _OFF_CB, _OFF_CC, _OFF_CX])
        w_silu = _gather_cols(w, [_OFF_FG, _OFF_DG, _OFF_CG])
        w_rot = _gather_cols(w, [_OFF_DQ, _OFF_DK])
        w_gate = w[:, _OFF_MG:_OFF_MG + N_BRANCH * D_MODEL].astype(BF16)
        w_ff_t = w[:, _OFF_FF:_OFF_FF + FOX_HEADS].T.astype(BF16)
        pa, h = _plain_proj(x2, pre_norm_g[l].reshape(1, D_MODEL), w_plain,
                            _scale_vec(7, A_FQ), TN_PLAIN)
        ps = _epilogue_proj(_silu_proj_kernel, "proj_silu", h, w_silu, TN_SILU, [], [])
        pr = _epilogue_proj(_rot_proj_kernel, "proj_rotary", h, w_rot, TN_ROT,
                            [_scale_vec(2, R_DQ), rc, rs1, rs2],
                            [_col_vec(TN_ROT), _ROPE_TILE, _ROPE_TILE, _ROPE_TILE])
        pg = _epilogue_proj(_gate_proj_kernel, "proj_gate", h, w_gate, TN_GATE,
                            [b_merge[l].reshape(1, N_BRANCH * D_MODEL)], [_col_vec(TN_GATE)])
        crow, ccol = _forget_cumsum(h, w_ff_t, b_forget[l].reshape(FOX_HEADS, 1), batch, seq)
        y_fox = _fox_attention(pa, ps, crow, ccol, batch, seq)
        lam_vecs = jnp.stack([lam_q1[l], lam_k1[l], lam_q2[l], lam_k2[l]])
        y_diff = _diff_attention(pr, pa, ps, lam_vecs, diff_norm_g[l].reshape(1, LANES),
                                 lam_init, batch, seq)
        x2 = _merge(x2, y_fox, y_diff, pa, ps, pg, conv_w[l],
                    w_br_fox[l].astype(BF16), w_br_diff[l].astype(BF16),
                    w_br_conv[l].astype(BF16), w_out[l].astype(BF16),
                    post_norm_g[l].reshape(1, D_MODEL), seq)
    return x2.reshape(batch, seq, D_MODEL)
```

```python
import functools
import math
from typing import Any, Callable, NamedTuple, Optional

import jax
import jax.numpy as jnp
from jax import lax
from jax.experimental import pallas as pl
from jax.experimental.pallas import tpu as pltpu

F32 = jnp.float32
BF16 = jnp.bfloat16

D_MODEL = 1024
HEAD_DIM = 64
FOX_HEADS = 8
DIFF_HEADS = 4
WIDTH = 512
CONV_K = 3
N_BRANCH = 3
ROPE_THETA = 500000.0
ROT_DIM = HEAD_DIM // 4
RMS_EPS = 1e-6
LANES = 128
SUBLANES = 8
ONES_ROWS = 2 * SUBLANES
LOG2E = math.log2(math.e)

(A_FQ, A_FK, A_FV, A_DV, A_CB, A_CC, A_CX) = range(7)
(S_FG, S_DG, S_CG) = range(3)
(R_DQ, R_DK) = range(2)
N_GATE_BLOCKS = N_BRANCH * D_MODEL // WIDTH

_OFF_FQ, _OFF_FK, _OFF_FV = 0, 512, 1024
_OFF_FF = 1536
_OFF_FG = 1544
_OFF_DQ, _OFF_DK, _OFF_DV, _OFF_DG = 2056, 2568, 3080, 3592
_OFF_CB, _OFF_CC, _OFF_CX, _OFF_CG = 4104, 4616, 5128, 5640
_OFF_MG = 6152

TM_IN = 1024
TN_PLAIN = 1792
TN_SILU = 1536
TN_ROT = 1024
TN_GATE = 1536
TS_FF = 512
TQ = 512
TK = 512
TM_OUT = 512
VMEM_LIMIT = 48 * 1024 * 1024
ATTN_VMEM_LIMIT = 56 * 1024 * 1024


def _silu(v):
    return v * jax.nn.sigmoid(v)


def _rotary_block(acc, c, s1, s2):
    outs = []
    for i in range(acc.shape[1] // LANES):
        v = acc[:, i * LANES:(i + 1) * LANES]
        outs.append(v * c + pltpu.roll(v, LANES - ROT_DIM // 2, 1) * s1
                    + pltpu.roll(v, ROT_DIM // 2, 1) * s2)
    return jnp.concatenate(outs, axis=1)


def _plain_proj_kernel(x_ref, g_ref, w_ref, sc_ref, o_ref, h_ref, h_scr):
    @pl.when(pl.program_id(1) == 0)
    def _():
        x = x_ref[...]
        ms = jnp.mean(x * x, axis=-1, keepdims=True)
        h = (x * lax.rsqrt(ms + RMS_EPS) * g_ref[...]).astype(BF16)
        h_scr[...] = h
        h_ref[...] = h

    acc = jnp.dot(h_scr[...], w_ref[...], preferred_element_type=F32)
    o_ref[...] = (acc * sc_ref[...]).astype(BF16)


def _silu_proj_kernel(h_ref, w_ref, o_ref):
    acc = jnp.dot(h_ref[...], w_ref[...], preferred_element_type=F32)
    o_ref[...] = _silu(acc).astype(BF16)


def _rot_proj_kernel(h_ref, w_ref, sc_ref, rc_ref, rs1_ref, rs2_ref, o_ref):
    acc = jnp.dot(h_ref[...], w_ref[...], preferred_element_type=F32)
    r = _rotary_block(acc, rc_ref[...], rs1_ref[...], rs2_ref[...])
    o_ref[...] = (r * sc_ref[...]).astype(BF16)


def _gate_proj_kernel(h_ref, w_ref, b_ref, o_ref):
    acc = jnp.dot(h_ref[...], w_ref[...], preferred_element_type=F32)
    o_ref[...] = jax.nn.sigmoid(acc + b_ref[...]).astype(BF16)


_PROJ_PARAMS = pltpu.CompilerParams(
    dimension_semantics=("arbitrary", "arbitrary"), vmem_limit_bytes=VMEM_LIMIT)
_ROW_TILE = pl.BlockSpec((TM_IN, D_MODEL), lambda i, j: (i, 0))
_ROPE_TILE = pl.BlockSpec((TM_IN, LANES), lambda i, j: (i, 0))


def _w_block(tn):
    return pl.BlockSpec((D_MODEL, tn), lambda i, j: (0, j))


def _col_vec(tn):
    return pl.BlockSpec((1, tn), lambda i, j: (0, j))


def _out_block(tn):
    return pl.BlockSpec((TM_IN, tn), lambda i, j: (i, j))


def _plain_proj(x2, pre_g, w, col_scale, tn):
    t, n = x2.shape[0], w.shape[1]
    return pl.pallas_call(
        _plain_proj_kernel,
        grid=(t // TM_IN, n // tn),
        in_specs=[_ROW_TILE, pl.BlockSpec((1, D_MODEL), lambda i, j: (0, 0)),
                  _w_block(tn), _col_vec(tn)],
        out_specs=[_out_block(tn), _ROW_TILE],
        out_shape=[jax.ShapeDtypeStruct((t, n), BF16), jax.ShapeDtypeStruct((t, D_MODEL), BF16)],
        scratch_shapes=[pltpu.VMEM((TM_IN, D_MODEL), BF16)],
        compiler_params=_PROJ_PARAMS,
        name="proj_plain",
    )(x2, pre_g, w, col_scale)


def _epilogue_proj(body, name, h, w, tn, extra, extra_specs):
    t, n = h.shape[0], w.shape[1]
    return pl.pallas_call(
        body,
        grid=(t // TM_IN, n // tn),
        in_specs=[_ROW_TILE, _w_block(tn)] + extra_specs,
        out_specs=_out_block(tn),
        out_shape=jax.ShapeDtypeStruct((t, n), BF16),
        compiler_params=_PROJ_PARAMS,
        name=name,
    )(h, w, *extra)


def _split3(v):
    hi = v.astype(BF16)
    r1 = v - hi.astype(F32)
    mid = r1.astype(BF16)
    lo = (r1 - mid.astype(F32)).astype(BF16)
    return hi, mid, lo


def _forget_kernel(h_ref, wt_ref, bf_ref, crow_ref, ccol_ref, carry_scr):
    si = pl.program_id(1)

    @pl.when(si == 0)
    def _():
        carry_scr[...] = jnp.zeros_like(carry_scr)

    ff = lax.dot_general(wt_ref[...], h_ref[...], (((1,), (1,)), ((), ())),
                         preferred_element_type=F32)
    lf = jax.nn.log_sigmoid(ff + bf_ref[...])
    row = lax.broadcasted_iota(jnp.int32, (TS_FF, TS_FF), 0)
    col = lax.broadcasted_iota(jnp.int32, (TS_FF, TS_FF), 1)
    upper = (row <= col).astype(BF16)
    cum = jnp.zeros((FOX_HEADS, TS_FF), F32)
    for piece in _split3(lf):
        cum = cum + jnp.dot(piece, upper, preferred_element_type=F32)
    c = cum + carry_scr[:, 0:1]
    carry_scr[...] = jnp.broadcast_to(c[:, TS_FF - 1:TS_FF], carry_scr.shape)
    c2 = c * LOG2E
    crow_ref[0, 0] = c2
    padded = jnp.concatenate([c2, jnp.zeros((LANES - FOX_HEADS, TS_FF), F32)], axis=0)
    ccol_ref[...] = padded.T


def _forget_cumsum(h, w_ff_t, b_f, batch, seq):
    nt = seq // TS_FF
    return pl.pallas_call(
        _forget_kernel,
        grid=(batch, nt),
        in_specs=[
            pl.BlockSpec((TS_FF, D_MODEL), lambda b, s: (b * nt + s, 0)),
            pl.BlockSpec((FOX_HEADS, D_MODEL), lambda b, s: (0, 0)),
            pl.BlockSpec((FOX_HEADS, 1), lambda b, s: (0, 0)),
        ],
        out_specs=[
            pl.BlockSpec((1, 1, FOX_HEADS, TS_FF), lambda b, s: (b, s, 0, 0)),
            pl.BlockSpec((TS_FF, LANES), lambda b, s: (b * nt + s, 0)),
        ],
        out_shape=[
            jax.ShapeDtypeStruct((batch, nt, FOX_HEADS, TS_FF), F32),
            jax.ShapeDtypeStruct((batch * seq, LANES), F32),
        ],
        scratch_shapes=[pltpu.VMEM((FOX_HEADS, LANES), F32)],
        compiler_params=pltpu.CompilerParams(
            dimension_semantics=("arbitrary", "arbitrary"), vmem_limit_bytes=VMEM_LIMIT),
        name="forget_cumsum",
    )(h, w_ff_t, b_f)


class _Map(NamedTuple):
    slot: int
    blk: int
    v_row0: int
    v_dim: int
    key_bias: Optional[Callable[[Any], Any]]
    query_shift: Optional[Callable[[], Any]]


def _attention_maps(maps, qi, qm_scr, k_ref, vt_scr, bufs, mb_scr, acc_scr, p_scr, emit):
    key_idx = lax.broadcasted_iota(jnp.int32, (TK, TQ), 0)
    query_idx = lax.broadcasted_iota(jnp.int32, (TK, TQ), 1)
    causal = key_idx <= query_idx
    groups = TK // SUBLANES

    def grouped(x):
        return x.reshape(groups, SUBLANES, TQ)

    def pass1_begin(buf):
        bufs[buf][1][...] = jnp.full((SUBLANES, TQ), -jnp.inf, F32)

    def pass1_chunk(mp, buf, ki, diagonal):
        t_scr, mx_scr = bufs[buf]
        start = pl.multiple_of(ki * TK, TK)
        k = k_ref[pl.ds(start, TK), mp.blk * LANES:(mp.blk + 1) * LANES]
        s = lax.dot_general(k, qm_scr[mp.slot], (((1,), (1,)), ((), ())),
                            preferred_element_type=F32)
        if mp.key_bias is not None:
            s = s - mp.key_bias(start)
        if diagonal:
            s = jnp.where(causal, s, -jnp.inf)
        t_scr[ki] = s
        mx_scr[...] = jnp.maximum(mx_scr[...], jnp.max(grouped(s), axis=0))

    def pass2_begin(mp, buf):
        m = jnp.max(bufs[buf][1][...], axis=0, keepdims=True)
        if mp.query_shift is not None:
            shift = mp.query_shift()
            m = (m + shift) - shift
        mb_scr[...] = jnp.broadcast_to(m, (SUBLANES, TQ))
        acc_scr[...] = jnp.zeros(acc_scr.shape, F32)

    def probs(buf, ki):
        p = jnp.exp2(grouped(bufs[buf][0][ki]) - mb_scr[...])
        return p.reshape(TK, TQ).astype(BF16)

    def accumulate(mp, chunk_probs):
        rows = slice(mp.v_row0, mp.v_row0 + mp.v_dim + ONES_ROWS)
        total = None
        for ki, p_t in chunk_probs:
            d = jnp.dot(vt_scr[ki, rows, :], p_t, preferred_element_type=F32)
            total = d if total is None else total + d
        acc_scr[0:mp.v_dim + ONES_ROWS, :] += total

    def pass2_end(n):
        v_dim = maps[n].v_dim
        l = acc_scr[v_dim:v_dim + 1, :]
        emit(n, acc_scr[0:v_dim, :] / l)

    def stage(n1, n2):
        m1, b1 = (maps[n1], n1 % 2) if n1 is not None else (None, None)
        m2, b2 = (maps[n2], n2 % 2) if n2 is not None else (None, None)
        if m2 is not None:
            pass2_begin(m2, b2)
        if m1 is not None:
            pass1_begin(b1)
            pass1_chunk(m1, b1, qi, True)
        n_pairs = lax.shift_right_logical(qi, 1)

        def pair(i, multiply_previous):
            if m2 is not None:
                if multiply_previous:
                    accumulate(m2, [(2 * i - 2, p_scr[0]), (2 * i - 1, p_scr[1])])
                p_scr[0] = probs(b2, 2 * i)
                p_scr[1] = probs(b2, 2 * i + 1)
            if m1 is not None:
                pass1_chunk(m1, b1, 2 * i, False)
                pass1_chunk(m1, b1, 2 * i + 1, False)

        def trip(i, carry):
            pair(i, True)
            return carry

        if m2 is None:
            lax.fori_loop(0, n_pairs, trip, 0)
        else:
            @pl.when(n_pairs > 0)
            def _():
                pair(0, False)

            lax.fori_loop(1, n_pairs, trip, 0)

            @pl.when(n_pairs > 0)
            def _():
                accumulate(m2, [(2 * n_pairs - 2, p_scr[0]), (2 * n_pairs - 1, p_scr[1])])

        @pl.when(lax.bitwise_and(qi, 1) == 1)
        def _():
            if m2 is not None:
                accumulate(m2, [(qi - 1, probs(b2, qi - 1))])
            if m1 is not None:
                pass1_chunk(m1, b1, qi - 1, False)

        if m2 is not None:
            accumulate(m2, [(qi, probs(b2, qi))])
            pass2_end(n2)

    stage(0, None)
    for n in range(1, len(maps)):
        stage(n, n - 1)
    stage(None, len(maps) - 1)


def _store_masked_queries(q_ref, qm_scr):
    first = lax.broadcasted_iota(jnp.int32, (TQ, LANES), 1) < HEAD_DIM
    for b in range(WIDTH // LANES):
        q = q_ref[:, b * LANES:(b + 1) * LANES]
        zero = jnp.zeros_like(q)
        qm_scr[2 * b] = jnp.where(first, q, zero)
        qm_scr[2 * b + 1] = jnp.where(first, zero, q)


def _store_transposed_values(v_ref, vt_scr, v_dim):
    stride = v_dim + ONES_ROWS
    per_block = LANES // v_dim

    @pl.when(pl.program_id(1) == 0)
    def _():
        def chunk(ki, carry):
            start = pl.multiple_of(ki * TK, TK)
            for b in range(WIDTH // LANES):
                v_t = v_ref[pl.ds(start, TK), b * LANES:(b + 1) * LANES].astype(F32).T.astype(BF16)
                for s in range(per_block):
                    row0 = (b * per_block + s) * stride
                    vt_scr[ki, row0:row0 + v_dim, :] = v_t[s * v_dim:(s + 1) * v_dim, :]
                    vt_scr[ki, row0 + v_dim:row0 + stride, :] = jnp.ones((ONES_ROWS, TK), BF16)
            return carry
        lax.fori_loop(0, vt_scr.shape[0], chunk, 0)


def _fox_kernel(q_ref, k_ref, v_ref, g_ref, crow_ref, ccol_ref, o_ref,
                qm_scr, vt_scr, ta_scr, tb_scr, mxa_scr, mxb_scr, mb_scr, acc_scr, res_scr, p_scr):
    qi = pl.program_id(1)
    _store_masked_queries(q_ref, qm_scr)
    _store_transposed_values(v_ref, vt_scr, HEAD_DIM)
    maps = [
        _Map(slot=head, blk=head // 2, v_row0=head * (HEAD_DIM + ONES_ROWS), v_dim=HEAD_DIM,
             key_bias=lambda start, head=head: ccol_ref[pl.ds(start, TK), head:head + 1],
             query_shift=lambda head=head: crow_ref[0, qi, head:head + 1, :])
        for head in range(FOX_HEADS)
    ]

    def emit(head, o_t):
        half = head % 2
        res_scr[half * HEAD_DIM:(half + 1) * HEAD_DIM, :] = o_t
        if half == 1:
            blk = slice((head // 2) * LANES, (head // 2 + 1) * LANES)
            y = res_scr[...].T * g_ref[:, blk].astype(F32)
            o_ref[:, blk] = y.astype(BF16)

    _attention_maps(maps, qi, qm_scr, k_ref, vt_scr, ((ta_scr, mxa_scr), (tb_scr, mxb_scr)),
                    mb_scr, acc_scr, p_scr, emit)


def _diff_kernel(q_ref, k_ref, v_ref, g_ref, lam_ref, ng_ref, o_ref,
                 qm_scr, vt_scr, ta_scr, tb_scr, mxa_scr, mxb_scr, mb_scr, acc_scr, res_scr, p_scr,
                 *, lam_init):
    qi = pl.program_id(1)
    _store_masked_queries(q_ref, qm_scr)
    _store_transposed_values(v_ref, vt_scr, LANES)
    lv = lam_ref[...]
    lam = (jnp.exp(jnp.sum(lv[0:1] * lv[1:2], axis=1, keepdims=True))
           - jnp.exp(jnp.sum(lv[2:3] * lv[3:4], axis=1, keepdims=True)) + lam_init)
    maps = [_Map(slot=n, blk=n // 2, v_row0=(n // 2) * (LANES + ONES_ROWS), v_dim=LANES,
                 key_bias=None, query_shift=None)
            for n in range(2 * DIFF_HEADS)]

    def emit(n, o_t):
        if n % 2 == 0:
            res_scr[...] = o_t
        else:
            blk = slice((n // 2) * LANES, (n // 2 + 1) * LANES)
            d = (res_scr[...] - lam * o_t).T
            ms = jnp.mean(d * d, axis=-1, keepdims=True)
            y = d * lax.rsqrt(ms + RMS_EPS) * ng_ref[...] * (1.0 - lam_init)
            o_ref[:, blk] = (y * g_ref[:, blk].astype(F32)).astype(BF16)

    _attention_maps(maps, qi, qm_scr, k_ref, vt_scr, ((ta_scr, mxa_scr), (tb_scr, mxb_scr)),
                    mb_scr, acc_scr, p_scr, emit)


def _per_sequence(block_shape, col):
    return pl.BlockSpec(block_shape, lambda b, i: (b, col))


def _attn_scratch(seq, v_dim):
    stat = pltpu.VMEM((SUBLANES, TQ), F32)
    logits = pltpu.VMEM((seq // TK, TK, TQ), F32)
    vt_rows = (WIDTH // v_dim) * (v_dim + ONES_ROWS)
    return [pltpu.VMEM((2 * WIDTH // LANES, TQ, LANES), BF16),
            pltpu.VMEM((seq // TK, vt_rows, TK), BF16),
            logits, logits, stat, stat,
            stat,
            pltpu.VMEM((v_dim + ONES_ROWS, TQ), F32),
            pltpu.VMEM((LANES, TQ), F32),
            pltpu.VMEM((2, TK, TQ), BF16)]


def _fox_attention(pa, ps, crow, ccol, batch, seq):
    nq = seq // TQ
    return pl.pallas_call(
        _fox_kernel,
        grid=(batch, nq),
        in_specs=[
            pl.BlockSpec((TQ, WIDTH), lambda b, i: (b * nq + i, A_FQ)),
            _per_sequence((seq, WIDTH), A_FK),
            _per_sequence((seq, WIDTH), A_FV),
            pl.BlockSpec((TQ, WIDTH), lambda b, i: (b * nq + i, S_FG)),
            pl.BlockSpec((1, seq // TK, FOX_HEADS, TK), lambda b, i: (b, 0, 0, 0)),
            _per_sequence((seq, LANES), 0),
        ],
        out_specs=pl.BlockSpec((TQ, WIDTH), lambda b, i: (b * nq + i, 0)),
        out_shape=jax.ShapeDtypeStruct((batch * seq, WIDTH), BF16),
        scratch_shapes=_attn_scratch(seq, HEAD_DIM),
        compiler_params=pltpu.CompilerParams(
            dimension_semantics=("arbitrary", "arbitrary"), vmem_limit_bytes=ATTN_VMEM_LIMIT),
        name="fox_attention",
    )(pa, pa, pa, ps, crow, ccol)


def _diff_attention(pr, pa, ps, lam_vecs, norm_g, lam_init, batch, seq):
    nq = seq // TQ
    return pl.pallas_call(
        functools.partial(_diff_kernel, lam_init=lam_init),
        grid=(batch, nq),
        in_specs=[
            pl.BlockSpec((TQ, WIDTH), lambda b, i: (b * nq + i, R_DQ)),
            _per_sequence((seq, WIDTH), R_DK),
            _per_sequence((seq, WIDTH), A_DV),
            pl.BlockSpec((TQ, WIDTH), lambda b, i: (b * nq + i, S_DG)),
            pl.BlockSpec((4, HEAD_DIM), lambda b, i: (0, 0)),
            pl.BlockSpec((1, LANES), lambda b, i: (0, 0)),
        ],
        out_specs=pl.BlockSpec((TQ, WIDTH), lambda b, i: (b * nq + i, 0)),
        out_shape=jax.ShapeDtypeStruct((batch * seq, WIDTH), BF16),
        scratch_shapes=_attn_scratch(seq, LANES),
        compiler_params=pltpu.CompilerParams(
            dimension_semantics=("arbitrary", "arbitrary"), vmem_limit_bytes=ATTN_VMEM_LIMIT),
        name="diff_attention",
    )(pr, pr, pa, ps, lam_vecs, norm_g)


def _merge_kernel(x_ref, yf_ref, yd_ref, cb_ref, cc_ref, cx_ref, cg_ref, hc_ref, hx_ref,
                  g0_ref, g1_ref, g2_ref, cw_ref, wf_ref, wd_ref, wc_ref, wo_ref, pg_ref,
                  o_ref, ext_scr, *, tiles_per_seq):
    i = pl.program_id(0)
    halo = SUBLANES
    u = cc_ref[...].astype(F32) * cx_ref[...].astype(F32)
    prev = hc_ref[...].astype(F32) * hx_ref[...].astype(F32)
    prev = jnp.where(i % tiles_per_seq == 0, jnp.zeros_like(prev), prev)
    ext_scr[0:halo, :] = prev
    ext_scr[halo:halo + TM_OUT, :] = u
    cw = cw_ref[...]
    conv = (cw[2:3] * u + cw[1:2] * ext_scr[halo - 1:halo - 1 + TM_OUT, :]
            + cw[0:1] * ext_scr[halo - 2:halo - 2 + TM_OUT, :])
    y_conv = (cb_ref[...].astype(F32) * conv * cg_ref[...].astype(F32)).astype(BF16)

    m = (g0_ref[...].astype(F32) * jnp.dot(yf_ref[...], wf_ref[...], preferred_element_type=F32)
         + g1_ref[...].astype(F32) * jnp.dot(yd_ref[...], wd_ref[...], preferred_element_type=F32)
         + g2_ref[...].astype(F32) * jnp.dot(y_conv, wc_ref[...], preferred_element_type=F32))
    o = jnp.dot(m.astype(BF16), wo_ref[...], preferred_element_type=F32)
    ms = jnp.mean(o * o, axis=-1, keepdims=True)
    o_ref[...] = x_ref[...] + o * lax.rsqrt(ms + RMS_EPS) * pg_ref[...]


def _merge(x2, y_fox, y_diff, pa, ps, pg, conv_w, w_fox, w_diff, w_conv, w_out, post_g, seq):
    t = x2.shape[0]
    tiles_per_seq = seq // TM_OUT
    halo_blocks = TM_OUT // SUBLANES

    def tile(col):
        return pl.BlockSpec((TM_OUT, WIDTH), lambda i: (i, col))

    def halo(col):
        return pl.BlockSpec((SUBLANES, WIDTH),
                            lambda i: (jnp.maximum(i * halo_blocks - 1, 0), col))

    def whole(shape):
        return pl.BlockSpec(shape, lambda i: (0, 0))

    return pl.pallas_call(
        functools.partial(_merge_kernel, tiles_per_seq=tiles_per_seq),
        grid=(t // TM_OUT,),
        in_specs=[
            pl.BlockSpec((TM_OUT, D_MODEL), lambda i: (i, 0)),
            pl.BlockSpec((TM_OUT, WIDTH), lambda i: (i, 0)),
            pl.BlockSpec((TM_OUT, WIDTH), lambda i: (i, 0)),
            tile(A_CB), tile(A_CC), tile(A_CX), tile(S_CG),
            halo(A_CC), halo(A_CX),
            pl.BlockSpec((TM_OUT, D_MODEL), lambda i: (i, 0)),
            pl.BlockSpec((TM_OUT, D_MODEL), lambda i: (i, 1)),
            pl.BlockSpec((TM_OUT, D_MODEL), lambda i: (i, 2)),
            whole((CONV_K, WIDTH)),
            whole((WIDTH, D_MODEL)), whole((WIDTH, D_MODEL)), whole((WIDTH, D_MODEL)),
            whole((D_MODEL, D_MODEL)),
            whole((1, D_MODEL)),
        ],
        out_specs=pl.BlockSpec((TM_OUT, D_MODEL), lambda i: (i, 0)),
        out_shape=jax.ShapeDtypeStruct((t, D_MODEL), F32),
        scratch_shapes=[pltpu.VMEM((TM_OUT + SUBLANES, WIDTH), F32)],
        compiler_params=pltpu.CompilerParams(
            dimension_semantics=("arbitrary",), vmem_limit_bytes=VMEM_LIMIT),
        name="merge",
    )(x2, y_fox, y_diff, pa, pa, pa, ps, pa, pa, pg, pg, pg,
      conv_w, w_fox, w_diff, w_conv, w_out, post_g)


def _rope_lane_tables(positions):
    half = ROT_DIM // 2
    inv_freq = ROPE_THETA ** (-jnp.arange(0, ROT_DIM, 2, dtype=F32) / ROT_DIM)
    ang = positions.astype(F32).reshape(-1, 1) * inv_freq
    cos, sin = jnp.cos(ang), jnp.sin(ang)
    t = ang.shape[0]
    ones = jnp.ones((t, HEAD_DIM - ROT_DIM), F32)
    zeros_rest = jnp.zeros((t, HEAD_DIM - ROT_DIM), F32)
    zeros_half = jnp.zeros((t, half), F32)
    c64 = jnp.concatenate([cos, cos, ones], axis=1)
    s1_64 = jnp.concatenate([-sin, zeros_half, zeros_rest], axis=1)
    s2_64 = jnp.concatenate([zeros_half, sin, zeros_rest], axis=1)
    rep = LANES // HEAD_DIM
    return jnp.tile(c64, (1, rep)), jnp.tile(s1_64, (1, rep)), jnp.tile(s2_64, (1, rep))


def _gather_cols(w, offsets):
    return jnp.concatenate([w[:, o:o + WIDTH] for o in offsets], axis=1).astype(BF16)


def _scale_vec(n_blocks, scaled_block):
    blk = jnp.arange(n_blocks * WIDTH, dtype=jnp.int32) // WIDTH
    q_scale = HEAD_DIM ** -0.5 * LOG2E
    return jnp.where(blk == scaled_block, q_scale, 1.0).astype(F32).reshape(1, -1)


def kernel(x, positions, pre_norm_g, w_in, b_forget, b_merge, conv_w, lam_q1, lam_k1, lam_q2,
           lam_k2, diff_norm_g, w_br_fox, w_br_diff, w_br_conv, w_out, post_norm_g):
    batch, seq, _ = x.shape
    depth = w_in.shape[0]
    assert seq % TQ == 0 and TQ == TK == TS_FF and (batch * seq) % TM_IN == 0
    rc, rs1, rs2 = _rope_lane_tables(positions)
    x2 = x.reshape(batch * seq, D_MODEL)
    for l in range(depth):
        lam_init = 0.8 - 0.6 * math.exp(-0.3 * l)
        w = w_in[l]
        w_plain = _gather_cols(w, [_OFF_FQ, _OFF_FK, _OFF_FV, _OFF_DV, _OFF_CB, _OFF_CC, _OFF_CX])
        w_silu = _gather_cols(w, [_OFF_FG, _OFF_DG, _OFF_CG])
        w_rot = _gather_cols(w, [_OFF_DQ, _OFF_DK])
        w_gate = w[:, _OFF_MG:_OFF_MG + N_BRANCH * D_MODEL].astype(BF16)
        w_ff_t = w[:, _OFF_FF:_OFF_FF + FOX_HEADS].T.astype(BF16)
        pa, h = _plain_proj(x2, pre_norm_g[l].reshape(1, D_MODEL), w_plain,
                            _scale_vec(7, A_FQ), TN_PLAIN)
        ps = _epilogue_proj(_silu_proj_kernel, "proj_silu", h, w_silu, TN_SILU, [], [])
        pr = _epilogue_proj(_rot_proj_kernel, "proj_rotary", h, w_rot, TN_ROT,
                            [_scale_vec(2, R_DQ), rc, rs1, rs2],
                            [_col_vec(TN_ROT), _ROPE_TILE, _ROPE_TILE, _ROPE_TILE])
        pg = _epilogue_proj(_gate_proj_kernel, "proj_gate", h, w_gate, TN_GATE,
                            [b_merge[l].reshape(1, N_BRANCH * D_MODEL)], [_col_vec(TN_GATE)])
        crow, ccol = _forget_cumsum(h, w_ff_t, b_forget[l].reshape(FOX_HEADS, 1), batch, seq)
        y_fox = _fox_attention(pa, ps, crow, ccol, batch, seq)
        lam_vecs = jnp.stack([lam_q1[l], lam_k1[l], lam_q2[l], lam_k2[l]])
        y_diff = _diff_attention(pr, pa, ps, lam_vecs, diff_norm_g[l].reshape(1, LANES),
                                 lam_init, batch, seq)
        x2 = _merge(x2, y_fox, y_diff, pa, ps, pg, conv_w[l],
                    w_br_fox[l].astype(BF16), w_br_diff[l].astype(BF16),
                    w_br_conv[l].astype(BF16), w_out[l].astype(BF16),
                    post_norm_g[l].reshape(1, D_MODEL), seq)
    return x2.reshape(batch, seq, D_MODEL)
```

```python
import functools
import math
from typing import Any, Callable, NamedTuple, Optional

import jax
import jax.numpy as jnp
from jax import lax
from jax.experimental import pallas as pl
from jax.experimental.pallas import tpu as pltpu

F32 = jnp.float32
BF16 = jnp.bfloat16

D_MODEL = 1024
HEAD_DIM = 64
FOX_HEADS = 8
DIFF_HEADS = 4
WIDTH = 512
CONV_K = 3
N_BRANCH = 3
ROPE_THETA = 500000.0
ROT_DIM = HEAD_DIM // 4
RMS_EPS = 1e-6
LANES = 128
SUBLANES = 8
ONES_ROWS = 2 * SUBLANES
LOG2E = math.log2(math.e)

(A_FQ, A_FK, A_FV, A_DV, A_CB, A_CC, A_CX) = range(7)
(S_FG, S_DG, S_CG) = range(3)
(R_DQ, R_DK) = range(2)
N_GATE_BLOCKS = N_BRANCH * D_MODEL // WIDTH

_OFF_FQ, _OFF_FK, _OFF_FV = 0, 512, 1024
_OFF_FF = 1536
_OFF_FG = 1544
_OFF_DQ, _OFF_DK, _OFF_DV, _OFF_DG = 2056, 2568, 3080, 3592
_OFF_CB, _OFF_CC, _OFF_CX, _OFF_CG = 4104, 4616, 5128, 5640
_OFF_MG = 6152

TM_IN = 1024
TN_PLAIN = 1792
TN_SILU = 1536
TN_ROT = 1024
TN_GATE = 1536
TS_FF = 512
TQ = 512
TK = 512
TM_OUT = 512
VMEM_LIMIT = 48 * 1024 * 1024
ATTN_VMEM_LIMIT = 56 * 1024 * 1024


def _silu(v):
    return v * jax.nn.sigmoid(v)


def _rotary_block(acc, c, s1, s2):
    outs = []
    for i in range(acc.shape[1] // LANES):
        v = acc[:, i * LANES:(i + 1) * LANES]
        outs.append(v * c + pltpu.roll(v, LANES - ROT_DIM // 2, 1) * s1
                    + pltpu.roll(v, ROT_DIM // 2, 1) * s2)
    return jnp.concatenate(outs, axis=1)


def _plain_proj_kernel(x_ref, g_ref, w_ref, sc_ref, o_ref, h_ref, h_scr):
    @pl.when(pl.program_id(1) == 0)
    def _():
        x = x_ref[...]
        ms = jnp.mean(x * x, axis=-1, keepdims=True)
        h = (x * lax.rsqrt(ms + RMS_EPS) * g_ref[...]).astype(BF16)
        h_scr[...] = h
        h_ref[...] = h

    acc = jnp.dot(h_scr[...], w_ref[...], preferred_element_type=F32)
    o_ref[...] = (acc * sc_ref[...]).astype(BF16)


def _silu_proj_kernel(h_ref, w_ref, o_ref):
    acc = jnp.dot(h_ref[...], w_ref[...], preferred_element_type=F32)
    o_ref[...] = _silu(acc).astype(BF16)


def _rot_proj_kernel(h_ref, w_ref, sc_ref, rc_ref, rs1_ref, rs2_ref, o_ref):
    acc = jnp.dot(h_ref[...], w_ref[...], preferred_element_type=F32)
    r = _rotary_block(acc, rc_ref[...], rs1_ref[...], rs2_ref[...])
    o_ref[...] = (r * sc_ref[...]).astype(BF16)


def _gate_proj_kernel(h_ref, w_ref, b_ref, o_ref):
    acc = jnp.dot(h_ref[...], w_ref[...], preferred_element_type=F32)
    o_ref[...] = jax.nn.sigmoid(acc + b_ref[...]).astype(BF16)


_PROJ_PARAMS = pltpu.CompilerParams(
    dimension_semantics=("arbitrary", "arbitrary"), vmem_limit_bytes=VMEM_LIMIT)
_ROW_TILE = pl.BlockSpec((TM_IN, D_MODEL), lambda i, j: (i, 0))
_ROPE_TILE = pl.BlockSpec((TM_IN, LANES), lambda i, j: (i, 0))


def _w_block(tn):
    return pl.BlockSpec((D_MODEL, tn), lambda i, j: (0, j))


def _col_vec(tn):
    return pl.BlockSpec((1, tn), lambda i, j: (0, j))


def _out_block(tn):
    return pl.BlockSpec((TM_IN, tn), lambda i, j: (i, j))


def _plain_proj(x2, pre_g, w, col_scale, tn):
    t, n = x2.shape[0], w.shape[1]
    return pl.pallas_call(
        _plain_proj_kernel,
        grid=(t // TM_IN, n // tn),
        in_specs=[_ROW_TILE, pl.BlockSpec((1, D_MODEL), lambda i, j: (0, 0)),
                  _w_block(tn), _col_vec(tn)],
        out_specs=[_out_block(tn), _ROW_TILE],
        out_shape=[jax.ShapeDtypeStruct((t, n), BF16), jax.ShapeDtypeStruct((t, D_MODEL), BF16)],
        scratch_shapes=[pltpu.VMEM((TM_IN, D_MODEL), BF16)],
        compiler_params=_PROJ_PARAMS,
        name="proj_plain",
    )(x2, pre_g, w, col_scale)


def _epilogue_proj(body, name, h, w, tn, extra, extra_specs):
    t, n = h.shape[0], w.shape[1]
    return pl.pallas_call(
        body,
        grid=(t // TM_IN, n // tn),
        in_specs=[_ROW_TILE, _w_block(tn)] + extra_specs,
        out_specs=_out_block(tn),
        out_shape=jax.ShapeDtypeStruct((t, n), BF16),
        compiler_params=_PROJ_PARAMS,
        name=name,
    )(h, w, *extra)


def _split3(v):
    hi = v.astype(BF16)
    r1 = v - hi.astype(F32)
    mid = r1.astype(BF16)
    lo = (r1 - mid.astype(F32)).astype(BF16)
    return hi, mid, lo


def _forget_kernel(h_ref, wt_ref, bf_ref, crow_ref, ccol_ref, carry_scr):
    si = pl.program_id(1)

    @pl.when(si == 0)
    def _():
        carry_scr[...] = jnp.zeros_like(carry_scr)

    ff = lax.dot_general(wt_ref[...], h_ref[...], (((1,), (1,)), ((), ())),
                         preferred_element_type=F32)
    lf = jax.nn.log_sigmoid(ff + bf_ref[...])
    row = lax.broadcasted_iota(jnp.int32, (TS_FF, TS_FF), 0)
    col = lax.broadcasted_iota(jnp.int32, (TS_FF, TS_FF), 1)
    upper = (row <= col).astype(BF16)
    cum = jnp.zeros((FOX_HEADS, TS_FF), F32)
    for piece in _split3(lf):
        cum = cum + jnp.dot(piece, upper, preferred_element_type=F32)
    c = cum + carry_scr[:, 0:1]
    carry_scr[...] = jnp.broadcast_to(c[:, TS_FF - 1:TS_FF], carry_scr.shape)
    c2 = c * LOG2E
    crow_ref[0, 0] = c2
    padded = jnp.concatenate([c2, jnp.zeros((LANES - FOX_HEADS, TS_FF), F32)], axis=0)
    ccol_ref[...] = padded.T


def _forget_cumsum(h, w_ff_t, b_f, batch, seq):
    nt = seq // TS_FF
    return pl.pallas_call(
        _forget_kernel,
        grid=(batch, nt),
        in_specs=[
            pl.BlockSpec((TS_FF, D_MODEL), lambda b, s: (b * nt + s, 0)),
            pl.BlockSpec((FOX_HEADS, D_MODEL), lambda b, s: (0, 0)),
            pl.BlockSpec((FOX_HEADS, 1), lambda b, s: (0, 0)),
        ],
        out_specs=[
            pl.BlockSpec((1, 1, FOX_HEADS, TS_FF), lambda b, s: (b, s, 0, 0)),
            pl.BlockSpec((TS_FF, LANES), lambda b, s: (b * nt + s, 0)),
        ],
        out_shape=[
            jax.ShapeDtypeStruct((batch, nt, FOX_HEADS, TS_FF), F32),
            jax.ShapeDtypeStruct((batch * seq, LANES), F32),
        ],
        scratch_shapes=[pltpu.VMEM((FOX_HEADS, LANES), F32)],
        compiler_params=pltpu.CompilerParams(
            dimension_semantics=("arbitrary", "arbitrary"), vmem_limit_bytes=VMEM_LIMIT),
        name="forget_cumsum",
    )(h, w_ff_t, b_f)


class _Map(NamedTuple):
    slot: int
    blk: int
    v_row0: int
    v_dim: int
    key_bias: Optional[Callable[[Any], Any]]
    query_shift: Optional[Callable[[], Any]]


def _attention_maps(maps, qi, qm_scr, k_ref, vt_scr, bufs, mb_scr, acc_scr, p_scr, emit):
    key_idx = lax.broadcasted_iota(jnp.int32, (TK, TQ), 0)
    query_idx = lax.broadcasted_iota(jnp.int32, (TK, TQ), 1)
    causal = key_idx <= query_idx
    groups = TK // SUBLANES

    def grouped(x):
        return x.reshape(groups, SUBLANES, TQ)

    def pass1_begin(buf):
        bufs[buf][1][...] = jnp.full((SUBLANES, TQ), -jnp.inf, F32)

    def pass1_chunk(mp, buf, ki, diagonal):
        t_scr, mx_scr = bufs[buf]
        start = pl.multiple_of(ki * TK, TK)
        k = k_ref[pl.ds(start, TK), mp.blk * LANES:(mp.blk + 1) * LANES]
        s = lax.dot_general(k, qm_scr[mp.slot], (((1,), (1,)), ((), ())),
                            preferred_element_type=F32)
        if mp.key_bias is not None:
            s = s - mp.key_bias(start)
        if diagonal:
            s = jnp.where(causal, s, -jnp.inf)
        t_scr[ki] = s
        mx_scr[...] = jnp.maximum(mx_scr[...], jnp.max(grouped(s), axis=0))

    def pass2_begin(mp, buf):
        m = jnp.max(bufs[buf][1][...], axis=0, keepdims=True)
        if mp.query_shift is not None:
            shift = mp.query_shift()
            m = (m + shift) - shift
        mb_scr[...] = jnp.broadcast_to(m, (SUBLANES, TQ))
        acc_scr[...] = jnp.zeros(acc_scr.shape, F32)

    def probs(buf, ki):
        p = jnp.exp2(grouped(bufs[buf][0][ki]) - mb_scr[...])
        return p.reshape(TK, TQ).astype(BF16)

    def accumulate(mp, chunk_probs):
        rows = slice(mp.v_row0, mp.v_row0 + mp.v_dim + ONES_ROWS)
        total = None
        for ki, p_t in chunk_probs:
            d = jnp.dot(vt_scr[ki, rows, :], p_t, preferred_element_type=F32)
            total = d if total is None else total + d
        acc_scr[0:mp.v_dim + ONES_ROWS, :] += total

    def pass2_end(n):
        v_dim = maps[n].v_dim
        l = acc_scr[v_dim:v_dim + 1, :]
        emit(n, acc_scr[0:v_dim, :] / l)

    def stage(n1, n2):
        m1, b1 = (maps[n1], n1 % 2) if n1 is not None else (None, None)
        m2, b2 = (maps[n2], n2 % 2) if n2 is not None else (None, None)
        if m2 is not None:
            pass2_begin(m2, b2)
        if m1 is not None:
            pass1_begin(b1)
            pass1_chunk(m1, b1, qi, True)
        n_pairs = lax.shift_right_logical(qi, 1)
        is_odd = lax.bitwise_and(qi, 1) == 1

        def stored(slot, valid):
            p_t = p_scr[slot]
            return jnp.where(valid, p_t, jnp.zeros_like(p_t))

        def stored_pair(i):
            valid = i > 0
            return [(jnp.maximum(2 * i - 2, 0), stored(0, valid)),
                    (jnp.maximum(2 * i - 1, 0), stored(1, valid))]

        def trip(i, carry):
            if m2 is not None:
                accumulate(m2, stored_pair(i))
                p_scr[0] = probs(b2, 2 * i)
                p_scr[1] = probs(b2, 2 * i + 1)
            if m1 is not None:
                pass1_chunk(m1, b1, 2 * i, False)
                pass1_chunk(m1, b1, 2 * i + 1, False)
            return carry

        lax.fori_loop(0, n_pairs, trip, 0)

        @pl.when(is_odd)
        def _():
            if m2 is not None:
                p_scr[2] = probs(b2, qi - 1)
            if m1 is not None:
                pass1_chunk(m1, b1, qi - 1, False)

        if m2 is not None:
            accumulate(m2, stored_pair(n_pairs) + [(jnp.maximum(qi - 1, 0), stored(2, is_odd)),
                                                   (qi, probs(b2, qi))])
            pass2_end(n2)

    stage(0, None)
    for n in range(1, len(maps)):
        stage(n, n - 1)
    stage(None, len(maps) - 1)


def _store_masked_queries(q_ref, qm_scr):
    first = lax.broadcasted_iota(jnp.int32, (TQ, LANES), 1) < HEAD_DIM
    for b in range(WIDTH // LANES):
        q = q_ref[:, b * LANES:(b + 1) * LANES]
        zero = jnp.zeros_like(q)
        qm_scr[2 * b] = jnp.where(first, q, zero)
        qm_scr[2 * b + 1] = jnp.where(first, zero, q)


def _store_transposed_values(v_ref, vt_scr, v_dim):
    stride = v_dim + ONES_ROWS
    per_block = LANES // v_dim

    @pl.when(pl.program_id(1) == 0)
    def _():
        def chunk(ki, carry):
            start = pl.multiple_of(ki * TK, TK)
            for b in range(WIDTH // LANES):
                v_t = v_ref[pl.ds(start, TK), b * LANES:(b + 1) * LANES].astype(F32).T.astype(BF16)
                for s in range(per_block):
                    row0 = (b * per_block + s) * stride
                    vt_scr[ki, row0:row0 + v_dim, :] = v_t[s * v_dim:(s + 1) * v_dim, :]
                    vt_scr[ki, row0 + v_dim:row0 + stride, :] = jnp.ones((ONES_ROWS, TK), BF16)
            return carry
        lax.fori_loop(0, vt_scr.shape[0], chunk, 0)


def _fox_kernel(q_ref, k_ref, v_ref, g_ref, crow_ref, ccol_ref, o_ref,
                qm_scr, vt_scr, ta_scr, tb_scr, mxa_scr, mxb_scr, mb_scr, acc_scr, res_scr, p_scr):
    qi = pl.program_id(1)
    _store_masked_queries(q_ref, qm_scr)
    _store_transposed_values(v_ref, vt_scr, HEAD_DIM)
    maps = [
        _Map(slot=head, blk=head // 2, v_row0=head * (HEAD_DIM + ONES_ROWS), v_dim=HEAD_DIM,
             key_bias=lambda start, head=head: ccol_ref[pl.ds(start, TK), head:head + 1],
             query_shift=lambda head=head: crow_ref[0, qi, head:head + 1, :])
        for head in range(FOX_HEADS)
    ]

    def emit(head, o_t):
        half = head % 2
        res_scr[half * HEAD_DIM:(half + 1) * HEAD_DIM, :] = o_t
        if half == 1:
            blk = slice((head // 2) * LANES, (head // 2 + 1) * LANES)
            y = res_scr[...].T * g_ref[:, blk].astype(F32)
            o_ref[:, blk] = y.astype(BF16)

    _attention_maps(maps, qi, qm_scr, k_ref, vt_scr, ((ta_scr, mxa_scr), (tb_scr, mxb_scr)),
                    mb_scr, acc_scr, p_scr, emit)


def _diff_kernel(q_ref, k_ref, v_ref, g_ref, lam_ref, ng_ref, o_ref,
                 qm_scr, vt_scr, ta_scr, tb_scr, mxa_scr, mxb_scr, mb_scr, acc_scr, res_scr, p_scr,
                 *, lam_init):
    qi = pl.program_id(1)
    _store_masked_queries(q_ref, qm_scr)
    _store_transposed_values(v_ref, vt_scr, LANES)
    lv = lam_ref[...]
    lam = (jnp.exp(jnp.sum(lv[0:1] * lv[1:2], axis=1, keepdims=True))
           - jnp.exp(jnp.sum(lv[2:3] * lv[3:4], axis=1, keepdims=True)) + lam_init)
    maps = [_Map(slot=n, blk=n // 2, v_row0=(n // 2) * (LANES + ONES_ROWS), v_dim=LANES,
                 key_bias=None, query_shift=None)
            for n in range(2 * DIFF_HEADS)]

    def emit(n, o_t):
        if n % 2 == 0:
            res_scr[...] = o_t
        else:
            blk = slice((n // 2) * LANES, (n // 2 + 1) * LANES)
            d = (res_scr[...] - lam * o_t).T
            ms = jnp.mean(d * d, axis=-1, keepdims=True)
            y = d * lax.rsqrt(ms + RMS_EPS) * ng_ref[...] * (1.0 - lam_init)
            o_ref[:, blk] = (y * g_ref[:, blk].astype(F32)).astype(BF16)

    _attention_maps(maps, qi, qm_scr, k_ref, vt_scr, ((ta_scr, mxa_scr), (tb_scr, mxb_scr)),
                    mb_scr, acc_scr, p_scr, emit)


def _per_sequence(block_shape, col):
    return pl.BlockSpec(block_shape, lambda b, i: (b, col))


def _attn_scratch(seq, v_dim):
    stat = pltpu.VMEM((SUBLANES, TQ), F32)
    logits = pltpu.VMEM((seq // TK, TK, TQ), F32)
    vt_rows = (WIDTH // v_dim) * (v_dim + ONES_ROWS)
    return [pltpu.VMEM((2 * WIDTH // LANES, TQ, LANES), BF16),
            pltpu.VMEM((seq // TK, vt_rows, TK), BF16),
            logits, logits, stat, stat,
            stat,
            pltpu.VMEM((v_dim + ONES_ROWS, TQ), F32),
            pltpu.VMEM((LANES, TQ), F32),
            pltpu.VMEM((3, TK, TQ), BF16)]


def _fox_attention(pa, ps, crow, ccol, batch, seq):
    nq = seq // TQ
    return pl.pallas_call(
        _fox_kernel,
        grid=(batch, nq),
        in_specs=[
            pl.BlockSpec((TQ, WIDTH), lambda b, i: (b * nq + i, A_FQ)),
            _per_sequence((seq, WIDTH), A_FK),
            _per_sequence((seq, WIDTH), A_FV),
            pl.BlockSpec((TQ, WIDTH), lambda b, i: (b * nq + i, S_FG)),
            pl.BlockSpec((1, seq // TK, FOX_HEADS, TK), lambda b, i: (b, 0, 0, 0)),
            _per_sequence((seq, LANES), 0),
        ],
        out_specs=pl.BlockSpec((TQ, WIDTH), lambda b, i: (b * nq + i, 0)),
        out_shape=jax.ShapeDtypeStruct((batch * seq, WIDTH), BF16),
        scratch_shapes=_attn_scratch(seq, HEAD_DIM),
        compiler_params=pltpu.CompilerParams(
            dimension_semantics=("arbitrary", "arbitrary"), vmem_limit_bytes=ATTN_VMEM_LIMIT),
        name="fox_attention",
    )(pa, pa, pa, ps, crow, ccol)


def _diff_attention(pr, pa, ps, lam_vecs, norm_g, lam_init, batch, seq):
    nq = seq // TQ
    return pl.pallas_call(
        functools.partial(_diff_kernel, lam_init=lam_init),
        grid=(batch, nq),
        in_specs=[
            pl.BlockSpec((TQ, WIDTH), lambda b, i: (b * nq + i, R_DQ)),
            _per_sequence((seq, WIDTH), R_DK),
            _per_sequence((seq, WIDTH), A_DV),
            pl.BlockSpec((TQ, WIDTH), lambda b, i: (b * nq + i, S_DG)),
            pl.BlockSpec((4, HEAD_DIM), lambda b, i: (0, 0)),
            pl.BlockSpec((1, LANES), lambda b, i: (0, 0)),
        ],
        out_specs=pl.BlockSpec((TQ, WIDTH), lambda b, i: (b * nq + i, 0)),
        out_shape=jax.ShapeDtypeStruct((batch * seq, WIDTH), BF16),
        scratch_shapes=_attn_scratch(seq, LANES),
        compiler_params=pltpu.CompilerParams(
            dimension_semantics=("arbitrary", "arbitrary"), vmem_limit_bytes=ATTN_VMEM_LIMIT),
        name="diff_attention",
    )(pr, pr, pa, ps, lam_vecs, norm_g)


def _merge_kernel(x_ref, yf_ref, yd_ref, cb_ref, cc_ref, cx_ref, cg_ref, hc_ref, hx_ref,
                  g0_ref, g1_ref, g2_ref, cw_ref, wf_ref, wd_ref, wc_ref, wo_ref, pg_ref,
                  o_ref, ext_scr, *, tiles_per_seq):
    i = pl.program_id(0)
    halo = SUBLANES
    u = cc_ref[...].astype(F32) * cx_ref[...].astype(F32)
    prev = hc_ref[...].astype(F32) * hx_ref[...].astype(F32)
    prev = jnp.where(i % tiles_per_seq == 0, jnp.zeros_like(prev), prev)
    ext_scr[0:halo, :] = prev
    ext_scr[halo:halo + TM_OUT, :] = u
    cw = cw_ref[...]
    conv = (cw[2:3] * u + cw[1:2] * ext_scr[halo - 1:halo - 1 + TM_OUT, :]
            + cw[0:1] * ext_scr[halo - 2:halo - 2 + TM_OUT, :])
    y_conv = (cb_ref[...].astype(F32) * conv * cg_ref[...].astype(F32)).astype(BF16)

    m = (g0_ref[...].astype(F32) * jnp.dot(yf_ref[...], wf_ref[...], preferred_element_type=F32)
         + g1_ref[...].astype(F32) * jnp.dot(yd_ref[...], wd_ref[...], preferred_element_type=F32)
         + g2_ref[...].astype(F32) * jnp.dot(y_conv, wc_ref[...], preferred_element_type=F32))
    o = jnp.dot(m.astype(BF16), wo_ref[...], preferred_element_type=F32)
    ms = jnp.mean(o * o, axis=-1, keepdims=True)
    o_ref[...] = x_ref[...] + o * lax.rsqrt(ms + RMS_EPS) * pg_ref[...]


def _merge(x2, y_fox, y_diff, pa, ps, pg, conv_w, w_fox, w_diff, w_conv, w_out, post_g, seq):
    t = x2.shape[0]
    tiles_per_seq = seq // TM_OUT
    halo_blocks = TM_OUT // SUBLANES

    def tile(col):
        return pl.BlockSpec((TM_OUT, WIDTH), lambda i: (i, col))

    def halo(col):
        return pl.BlockSpec((SUBLANES, WIDTH),
                            lambda i: (jnp.maximum(i * halo_blocks - 1, 0), col))

    def whole(shape):
        return pl.BlockSpec(shape, lambda i: (0, 0))

    return pl.pallas_call(
        functools.partial(_merge_kernel, tiles_per_seq=tiles_per_seq),
        grid=(t // TM_OUT,),
        in_specs=[
            pl.BlockSpec((TM_OUT, D_MODEL), lambda i: (i, 0)),
            pl.BlockSpec((TM_OUT, WIDTH), lambda i: (i, 0)),
            pl.BlockSpec((TM_OUT, WIDTH), lambda i: (i, 0)),
            tile(A_CB), tile(A_CC), tile(A_CX), tile(S_CG),
            halo(A_CC), halo(A_CX),
            pl.BlockSpec((TM_OUT, D_MODEL), lambda i: (i, 0)),
            pl.BlockSpec((TM_OUT, D_MODEL), lambda i: (i, 1)),
            pl.BlockSpec((TM_OUT, D_MODEL), lambda i: (i, 2)),
            whole((CONV_K, WIDTH)),
            whole((WIDTH, D_MODEL)), whole((WIDTH, D_MODEL)), whole((WIDTH, D_MODEL)),
            whole((D_MODEL, D_MODEL)),
            whole((1, D_MODEL)),
        ],
        out_specs=pl.BlockSpec((TM_OUT, D_MODEL), lambda i: (i, 0)),
        out_shape=jax.ShapeDtypeStruct((t, D_MODEL), F32),
        scratch_shapes=[pltpu.VMEM((TM_OUT + SUBLANES, WIDTH), F32)],
        compiler_params=pltpu.CompilerParams(
            dimension_semantics=("arbitrary",), vmem_limit_bytes=VMEM_LIMIT),
        name="merge",
    )(x2, y_fox, y_diff, pa, pa, pa, ps, pa, pa, pg, pg, pg,
      conv_w, w_fox, w_diff, w_conv, w_out, post_g)


def _rope_lane_tables(positions):
    half = ROT_DIM // 2
    inv_freq = ROPE_THETA ** (-jnp.arange(0, ROT_DIM, 2, dtype=F32) / ROT_DIM)
    ang = positions.astype(F32).reshape(-1, 1) * inv_freq
    cos, sin = jnp.cos(ang), jnp.sin(ang)
    t = ang.shape[0]
    ones = jnp.ones((t, HEAD_DIM - ROT_DIM), F32)
    zeros_rest = jnp.zeros((t, HEAD_DIM - ROT_DIM), F32)
    zeros_half = jnp.zeros((t, half), F32)
    c64 = jnp.concatenate([cos, cos, ones], axis=1)
    s1_64 = jnp.concatenate([-sin, zeros_half, zeros_rest], axis=1)
    s2_64 = jnp.concatenate([zeros_half, sin, zeros_rest], axis=1)
    rep = LANES // HEAD_DIM
    return jnp.tile(c64, (1, rep)), jnp.tile(s1_64, (1, rep)), jnp.tile(s2_64, (1, rep))


def _gather_cols(w, offsets):
    return jnp.concatenate([w[:, o:o + WIDTH] for o in offsets], axis=1).astype(BF16)


def _scale_vec(n_blocks, scaled_block):
    blk = jnp.arange(n_blocks * WIDTH, dtype=jnp.int32) // WIDTH
    q_scale = HEAD_DIM ** -0.5 * LOG2E
    return jnp.where(blk == scaled_block, q_scale, 1.0).astype(F32).reshape(1, -1)


def kernel(x, positions, pre_norm_g, w_in, b_forget, b_merge, conv_w, lam_q1, lam_k1, lam_q2,
           lam_k2, diff_norm_g, w_br_fox, w_br_diff, w_br_conv, w_out, post_norm_g):
    batch, seq, _ = x.shape
    depth = w_in.shape[0]
    assert seq % TQ == 0 and TQ == TK == TS_FF and (batch * seq) % TM_IN == 0
    rc, rs1, rs2 = _rope_lane_tables(positions)
    x2 = x.reshape(batch * seq, D_MODEL)
    for l in range(depth):
        lam_init = 0.8 - 0.6 * math.exp(-0.3 * l)
        w = w_in[l]
        w_plain = _gather_cols(w, [_OFF_FQ, _OFF_FK, _OFF_FV, _OFF_DV, _OFF_CB, _OFF_CC, _OFF_CX])
        w_silu = _gather_cols(w, [_OFF_FG, _OFF_DG, _OFF_CG])
        w_rot = _gather_cols(w, [_OFF_DQ, _OFF_DK])
        w_gate = w[:, _OFF_MG:_OFF_MG + N_BRANCH * D_MODEL].astype(BF16)
        w_ff_t = w[:, _OFF_FF:_OFF_FF + FOX_HEADS].T.astype(BF16)
        pa, h = _plain_proj(x2, pre_norm_g[l].reshape(1, D_MODEL), w_plain,
                            _scale_vec(7, A_FQ), TN_PLAIN)
        ps = _epilogue_proj(_silu_proj_kernel, "proj_silu", h, w_silu, TN_SILU, [], [])
        pr = _epilogue_proj(_rot_proj_kernel, "proj_rotary", h, w_rot, TN_ROT,
                            [_scale_vec(2, R_DQ), rc, rs1, rs2],
                            [_col_vec(TN_ROT), _ROPE_TILE, _ROPE_TILE, _ROPE_TILE])
        pg = _epilogue_proj(_gate_proj_kernel, "proj_gate", h, w_gate, TN_GATE,
                            [b_merge[l].reshape(1, N_BRANCH * D_MODEL)], [_col_vec(TN_GATE)])
        crow, ccol = _forget_cumsum(h, w_ff_t, b_forget[l].reshape(FOX_HEADS, 1), batch, seq)
        y_fox = _fox_attention(pa, ps, crow, ccol, batch, seq)
        lam_vecs = jnp.stack([lam_q1[l], lam_k1[l], lam_q2[l], lam_k2[l]])
        y_diff = _diff_attention(pr, pa, ps, lam_vecs, diff_norm_g[l].reshape(1, LANES),
                                 lam_init, batch, seq)
        x2 = _merge(x2, y_fox, y_diff, pa, ps, pg, conv_w[l],
                    w_br_fox[l].astype(BF16), w_br_diff[l].astype(BF16),
                    w_br_conv[l].astype(BF16), w_out[l].astype(BF16),
                    post_norm_g[l].reshape(1, D_MODEL), seq)
    return x2.reshape(batch, seq, D_MODEL)
```

```python
import functools
import math
from typing import Any, Callable, NamedTuple, Optional

import jax
import jax.numpy as jnp
from jax import lax
from jax.experimental import pallas as pl
from jax.experimental.pallas import tpu as pltpu

F32 = jnp.float32
BF16 = jnp.bfloat16

D_MODEL = 1024
HEAD_DIM = 64
FOX_HEADS = 8
DIFF_HEADS = 4
WIDTH = 512
CONV_K = 3
N_BRANCH = 3
ROPE_THETA = 500000.0
ROT_DIM = HEAD_DIM // 4
RMS_EPS = 1e-6
LANES = 128
SUBLANES = 8

(A_FQ, A_FK, A_FV, A_DV, A_CB, A_CC, A_CX) = range(7)
(S_FG, S_DG, S_CG) = range(3)
(R_DQ, R_DK) = range(2)
N_GATE_BLOCKS = N_BRANCH * D_MODEL // WIDTH

_OFF_FQ, _OFF_FK, _OFF_FV = 0, 512, 1024
_OFF_FF = 1536
_OFF_FG = 1544
_OFF_DQ, _OFF_DK, _OFF_DV, _OFF_DG = 2056, 2568, 3080, 3592
_OFF_CB, _OFF_CC, _OFF_CX, _OFF_CG = 4104, 4616, 5128, 5640
_OFF_MG = 6152

TM_IN = 1024
TN_PLAIN = 1792
TN_SILU = 1536
TN_ROT = 1024
TN_GATE = 1536
TS_FF = 512
TQ = 512
TK = 512
TM_OUT = 512
VMEM_LIMIT = 48 * 1024 * 1024


def _silu(v):
    return v * jax.nn.sigmoid(v)


def _rotary_block(acc, c, s1, s2):
    outs = []
    for i in range(acc.shape[1] // LANES):
        v = acc[:, i * LANES:(i + 1) * LANES]
        outs.append(v * c + pltpu.roll(v, LANES - ROT_DIM // 2, 1) * s1
                    + pltpu.roll(v, ROT_DIM // 2, 1) * s2)
    return jnp.concatenate(outs, axis=1)


def _plain_proj_kernel(x_ref, g_ref, w_ref, sc_ref, o_ref, h_ref, h_scr):
    @pl.when(pl.program_id(1) == 0)
    def _():
        x = x_ref[...]
        ms = jnp.mean(x * x, axis=-1, keepdims=True)
        h = (x * lax.rsqrt(ms + RMS_EPS) * g_ref[...]).astype(BF16)
        h_scr[...] = h
        h_ref[...] = h

    acc = jnp.dot(h_scr[...], w_ref[...], preferred_element_type=F32)
    o_ref[...] = (acc * sc_ref[...]).astype(BF16)


def _silu_proj_kernel(h_ref, w_ref, o_ref):
    acc = jnp.dot(h_ref[...], w_ref[...], preferred_element_type=F32)
    o_ref[...] = _silu(acc).astype(BF16)


def _rot_proj_kernel(h_ref, w_ref, sc_ref, rc_ref, rs1_ref, rs2_ref, o_ref):
    acc = jnp.dot(h_ref[...], w_ref[...], preferred_element_type=F32)
    r = _rotary_block(acc, rc_ref[...], rs1_ref[...], rs2_ref[...])
    o_ref[...] = (r * sc_ref[...]).astype(BF16)


def _gate_proj_kernel(h_ref, w_ref, b_ref, o_ref):
    acc = jnp.dot(h_ref[...], w_ref[...], preferred_element_type=F32)
    o_ref[...] = jax.nn.sigmoid(acc + b_ref[...]).astype(BF16)


_PROJ_PARAMS = pltpu.CompilerParams(
    dimension_semantics=("arbitrary", "arbitrary"), vmem_limit_bytes=VMEM_LIMIT)
_ROW_TILE = pl.BlockSpec((TM_IN, D_MODEL), lambda i, j: (i, 0))
_ROPE_TILE = pl.BlockSpec((TM_IN, LANES), lambda i, j: (i, 0))


def _w_block(tn):
    return pl.BlockSpec((D_MODEL, tn), lambda i, j: (0, j))


def _col_vec(tn):
    return pl.BlockSpec((1, tn), lambda i, j: (0, j))


def _out_block(tn):
    return pl.BlockSpec((TM_IN, tn), lambda i, j: (i, j))


def _plain_proj(x2, pre_g, w, col_scale, tn):
    t, n = x2.shape[0], w.shape[1]
    return pl.pallas_call(
        _plain_proj_kernel,
        grid=(t // TM_IN, n // tn),
        in_specs=[_ROW_TILE, pl.BlockSpec((1, D_MODEL), lambda i, j: (0, 0)),
                  _w_block(tn), _col_vec(tn)],
        out_specs=[_out_block(tn), _ROW_TILE],
        out_shape=[jax.ShapeDtypeStruct((t, n), BF16), jax.ShapeDtypeStruct((t, D_MODEL), BF16)],
        scratch_shapes=[pltpu.VMEM((TM_IN, D_MODEL), BF16)],
        compiler_params=_PROJ_PARAMS,
        name="proj_plain",
    )(x2, pre_g, w, col_scale)


def _epilogue_proj(body, name, h, w, tn, extra, extra_specs):
    t, n = h.shape[0], w.shape[1]
    return pl.pallas_call(
        body,
        grid=(t // TM_IN, n // tn),
        in_specs=[_ROW_TILE, _w_block(tn)] + extra_specs,
        out_specs=_out_block(tn),
        out_shape=jax.ShapeDtypeStruct((t, n), BF16),
        compiler_params=_PROJ_PARAMS,
        name=name,
    )(h, w, *extra)


def _split3(v):
    hi = v.astype(BF16)
    r1 = v - hi.astype(F32)
    mid = r1.astype(BF16)
    lo = (r1 - mid.astype(F32)).astype(BF16)
    return hi, mid, lo


def _forget_kernel(h_ref, wt_ref, bf_ref, crow_ref, ccol_ref, carry_scr):
    si = pl.program_id(1)

    @pl.when(si == 0)
    def _():
        carry_scr[...] = jnp.zeros_like(carry_scr)

    ff = lax.dot_general(wt_ref[...], h_ref[...], (((1,), (1,)), ((), ())),
                         preferred_element_type=F32)
    lf = jax.nn.log_sigmoid(ff + bf_ref[...])
    row = lax.broadcasted_iota(jnp.int32, (TS_FF, TS_FF), 0)
    col = lax.broadcasted_iota(jnp.int32, (TS_FF, TS_FF), 1)
    upper = (row <= col).astype(BF16)
    cum = jnp.zeros((FOX_HEADS, TS_FF), F32)
    for piece in _split3(lf):
        cum = cum + jnp.dot(piece, upper, preferred_element_type=F32)
    c = cum + carry_scr[:, 0:1]
    carry_scr[...] = jnp.broadcast_to(c[:, TS_FF - 1:TS_FF], carry_scr.shape)
    crow_ref[0, 0] = c
    padded = jnp.concatenate([c, jnp.zeros((LANES - FOX_HEADS, TS_FF), F32)], axis=0)
    ccol_ref[...] = padded.T


def _forget_cumsum(h, w_ff_t, b_f, batch, seq):
    nt = seq // TS_FF
    return pl.pallas_call(
        _forget_kernel,
        grid=(batch, nt),
        in_specs=[
            pl.BlockSpec((TS_FF, D_MODEL), lambda b, s: (b * nt + s, 0)),
            pl.BlockSpec((FOX_HEADS, D_MODEL), lambda b, s: (0, 0)),
            pl.BlockSpec((FOX_HEADS, 1), lambda b, s: (0, 0)),
        ],
        out_specs=[
            pl.BlockSpec((1, 1, FOX_HEADS, TS_FF), lambda b, s: (b, s, 0, 0)),
            pl.BlockSpec((TS_FF, LANES), lambda b, s: (b * nt + s, 0)),
        ],
        out_shape=[
            jax.ShapeDtypeStruct((batch, nt, FOX_HEADS, TS_FF), F32),
            jax.ShapeDtypeStruct((batch * seq, LANES), F32),
        ],
        scratch_shapes=[pltpu.VMEM((FOX_HEADS, LANES), F32)],
        compiler_params=pltpu.CompilerParams(
            dimension_semantics=("arbitrary", "arbitrary"), vmem_limit_bytes=VMEM_LIMIT),
        name="forget_cumsum",
    )(h, w_ff_t, b_f)


class _Map(NamedTuple):
    slot: int
    blk: int
    key_bias: Optional[Callable[[Any], Any]]
    row_shift: Any


def _attention_maps(maps, qi, qm_scr, k_ref, v_ref, bufs, mb_scr, ls_scr, acc_scr, emit):
    row = lax.broadcasted_iota(jnp.int32, (TQ, TK), 0)
    col = lax.broadcasted_iota(jnp.int32, (TQ, TK), 1)
    causal = col <= row
    n_lane_blocks = TK // LANES

    def lanes(v, c):
        return v[:, c * LANES:(c + 1) * LANES]

    def chunk_rows(ref, mp, ki):
        start = pl.multiple_of(ki * TK, TK)
        return ref[pl.ds(start, TK), mp.blk * LANES:(mp.blk + 1) * LANES]

    def pass1_begin(buf):
        bufs[buf][1][...] = jnp.full((TQ, LANES), -jnp.inf, F32)

    def pass1_chunk(mp, buf, ki, diagonal):
        t_scr, mx_scr = bufs[buf]
        s = lax.dot_general(qm_scr[mp.slot], chunk_rows(k_ref, mp, ki),
                            (((1,), (1,)), ((), ())), preferred_element_type=F32)
        if mp.key_bias is not None:
            s = s - mp.key_bias(ki)
        if diagonal:
            s = jnp.where(causal, s, -jnp.inf)
        t_scr[ki] = s
        m = mx_scr[...]
        for c in range(n_lane_blocks):
            m = jnp.maximum(m, lanes(s, c))
        mx_scr[...] = m

    def pass2_begin(mp, buf):
        m = jnp.max(bufs[buf][1][...], axis=1, keepdims=True)
        if mp.row_shift is not None:
            m = (m + mp.row_shift) - mp.row_shift
        mb_scr[...] = jnp.broadcast_to(m, (TQ, LANES))
        ls_scr[...] = jnp.zeros((TQ, LANES), F32)
        acc_scr[...] = jnp.zeros((TQ, LANES), F32)

    def pass2_chunk(mp, buf, ki):
        t = bufs[buf][0][ki]
        mb = mb_scr[...]
        ps = [jnp.exp(lanes(t, c) - mb) for c in range(n_lane_blocks)]
        ls = ls_scr[...]
        for p in ps:
            ls = ls + p
        ls_scr[...] = ls
        p = jnp.concatenate(ps, axis=1).astype(BF16)
        acc_scr[...] += jnp.dot(p, chunk_rows(v_ref, mp, ki), preferred_element_type=F32)

    def pass2_end(n):
        l = jnp.sum(ls_scr[...], axis=1, keepdims=True)
        emit(n, acc_scr[...] / l)

    def loop(n_chunks, body):
        def step(i, carry):
            body(2 * i)
            body(2 * i + 1)
            return carry
        lax.fori_loop(0, lax.shift_right_logical(n_chunks, 1), step, 0)

        @pl.when(lax.bitwise_and(n_chunks, 1) == 1)
        def _():
            body(n_chunks - 1)

    pass1_begin(0)
    pass1_chunk(maps[0], 0, qi, True)
    loop(qi, lambda ki: pass1_chunk(maps[0], 0, ki, False))
    for n in range(1, len(maps)):
        buf, prev = n % 2, (n - 1) % 2
        pass2_begin(maps[n - 1], prev)
        pass1_begin(buf)
        pass1_chunk(maps[n], buf, qi, True)

        def both(ki, n=n, buf=buf, prev=prev):
            pass2_chunk(maps[n - 1], prev, ki)
            pass1_chunk(maps[n], buf, ki, False)

        loop(qi, both)
        pass2_chunk(maps[n - 1], prev, qi)
        pass2_end(n - 1)
    last = len(maps) - 1
    pass2_begin(maps[last], last % 2)
    loop(qi + 1, lambda ki: pass2_chunk(maps[last], last % 2, ki))
    pass2_end(last)


def _store_masked_queries(q_ref, qm_scr):
    first = lax.broadcasted_iota(jnp.int32, (TQ, LANES), 1) < HEAD_DIM
    for b in range(WIDTH // LANES):
        q = q_ref[:, b * LANES:(b + 1) * LANES]
        zero = jnp.zeros_like(q)
        qm_scr[2 * b] = jnp.where(first, q, zero)
        qm_scr[2 * b + 1] = jnp.where(first, zero, q)


def _fox_kernel(q_ref, k_ref, v_ref, g_ref, crow_ref, ccol_ref, o_ref,
                qm_scr, ta_scr, tb_scr, mxa_scr, mxb_scr, mb_scr, ls_scr, acc_scr, res_scr):
    qi = pl.program_id(1)
    first = lax.broadcasted_iota(jnp.int32, (TQ, LANES), 1) < HEAD_DIM
    _store_masked_queries(q_ref, qm_scr)
    ccol = ccol_ref[...]
    maps = [
        _Map(slot=head, blk=head // 2,
             key_bias=lambda ki, head=head: crow_ref[0, ki, head:head + 1, :],
             row_shift=ccol[:, head:head + 1])
        for head in range(FOX_HEADS)
    ]

    def emit(head, o):
        if head % 2 == 0:
            res_scr[...] = o
        else:
            blk = slice((head // 2) * LANES, (head // 2 + 1) * LANES)
            y = jnp.where(first, res_scr[...], o) * g_ref[:, blk].astype(F32)
            o_ref[:, blk] = y.astype(BF16)

    _attention_maps(maps, qi, qm_scr, k_ref, v_ref, ((ta_scr, mxa_scr), (tb_scr, mxb_scr)),
                    mb_scr, ls_scr, acc_scr, emit)


def _diff_kernel(q_ref, k_ref, v_ref, g_ref, lam_ref, ng_ref, o_ref,
                 qm_scr, ta_scr, tb_scr, mxa_scr, mxb_scr, mb_scr, ls_scr, acc_scr, res_scr,
                 *, lam_init):
    qi = pl.program_id(1)
    _store_masked_queries(q_ref, qm_scr)
    lv = lam_ref[...]
    lam = (jnp.exp(jnp.sum(lv[0:1] * lv[1:2], axis=1, keepdims=True))
           - jnp.exp(jnp.sum(lv[2:3] * lv[3:4], axis=1, keepdims=True)) + lam_init)
    maps = [_Map(slot=n, blk=n // 2, key_bias=None, row_shift=None)
            for n in range(2 * DIFF_HEADS)]

    def emit(n, o):
        if n % 2 == 0:
            res_scr[...] = o
        else:
            blk = slice((n // 2) * LANES, (n // 2 + 1) * LANES)
            d = res_scr[...] - lam * o
            ms = jnp.mean(d * d, axis=-1, keepdims=True)
            y = d * lax.rsqrt(ms + RMS_EPS) * ng_ref[...] * (1.0 - lam_init)
            o_ref[:, blk] = (y * g_ref[:, blk].astype(F32)).astype(BF16)

    _attention_maps(maps, qi, qm_scr, k_ref, v_ref, ((ta_scr, mxa_scr), (tb_scr, mxb_scr)),
                    mb_scr, ls_scr, acc_scr, emit)


def _attn_scratch(seq):
    stat = pltpu.VMEM((TQ, LANES), F32)
    logits = pltpu.VMEM((seq // TK, TQ, TK), F32)
    return [pltpu.VMEM((2 * WIDTH // LANES, TQ, LANES), BF16),
            logits, logits, stat, stat,
            stat, stat, stat, stat]


def _fox_attention(pa, ps, crow, ccol, batch, seq):
    nq = seq // TQ
    return pl.pallas_call(
        _fox_kernel,
        grid=(batch, nq),
        in_specs=[
            pl.BlockSpec((TQ, WIDTH), lambda b, i: (b * nq + i, A_FQ)),
            pl.BlockSpec((seq, WIDTH), lambda b, i: (b, A_FK)),
            pl.BlockSpec((seq, WIDTH), lambda b, i: (b, A_FV)),
            pl.BlockSpec((TQ, WIDTH), lambda b, i: (b * nq + i, S_FG)),
            pl.BlockSpec((1, seq // TK, FOX_HEADS, TK), lambda b, i: (b, 0, 0, 0)),
            pl.BlockSpec((TQ, LANES), lambda b, i: (b * nq + i, 0)),
        ],
        out_specs=pl.BlockSpec((TQ, WIDTH), lambda b, i: (b * nq + i, 0)),
        out_shape=jax.ShapeDtypeStruct((batch * seq, WIDTH), BF16),
        scratch_shapes=_attn_scratch(seq),
        compiler_params=pltpu.CompilerParams(
            dimension_semantics=("arbitrary", "arbitrary"), vmem_limit_bytes=VMEM_LIMIT),
        name="fox_attention",
    )(pa, pa, pa, ps, crow, ccol)


def _diff_attention(pr, pa, ps, lam_vecs, norm_g, lam_init, batch, seq):
    nq = seq // TQ
    return pl.pallas_call(
        functools.partial(_diff_kernel, lam_init=lam_init),
        grid=(batch, nq),
        in_specs=[
            pl.BlockSpec((TQ, WIDTH), lambda b, i: (b * nq + i, R_DQ)),
            pl.BlockSpec((seq, WIDTH), lambda b, i: (b, R_DK)),
            pl.BlockSpec((seq, WIDTH), lambda b, i: (b, A_DV)),
            pl.BlockSpec((TQ, WIDTH), lambda b, i: (b * nq + i, S_DG)),
            pl.BlockSpec((4, HEAD_DIM), lambda b, i: (0, 0)),
            pl.BlockSpec((1, LANES), lambda b, i: (0, 0)),
        ],
        out_specs=pl.BlockSpec((TQ, WIDTH), lambda b, i: (b * nq + i, 0)),
        out_shape=jax.ShapeDtypeStruct((batch * seq, WIDTH), BF16),
        scratch_shapes=_attn_scratch(seq),
        compiler_params=pltpu.CompilerParams(
            dimension_semantics=("arbitrary", "arbitrary"), vmem_limit_bytes=VMEM_LIMIT),
        name="diff_attention",
    )(pr, pr, pa, ps, lam_vecs, norm_g)


def _merge_kernel(x_ref, yf_ref, yd_ref, cb_ref, cc_ref, cx_ref, cg_ref, hc_ref, hx_ref,
                  g0_ref, g1_ref, g2_ref, cw_ref, wf_ref, wd_ref, wc_ref, wo_ref, pg_ref,
                  o_ref, ext_scr, *, tiles_per_seq):
    i = pl.program_id(0)
    halo = SUBLANES
    u = cc_ref[...].astype(F32) * cx_ref[...].astype(F32)
    prev = hc_ref[...].astype(F32) * hx_ref[...].astype(F32)
    prev = jnp.where(i % tiles_per_seq == 0, jnp.zeros_like(prev), prev)
    ext_scr[0:halo, :] = prev
    ext_scr[halo:halo + TM_OUT, :] = u
    cw = cw_ref[...]
    conv = (cw[2:3] * u + cw[1:2] * ext_scr[halo - 1:halo - 1 + TM_OUT, :]
            + cw[0:1] * ext_scr[halo - 2:halo - 2 + TM_OUT, :])
    y_conv = (cb_ref[...].astype(F32) * conv * cg_ref[...].astype(F32)).astype(BF16)

    m = (g0_ref[...].astype(F32) * jnp.dot(yf_ref[...], wf_ref[...], preferred_element_type=F32)
         + g1_ref[...].astype(F32) * jnp.dot(yd_ref[...], wd_ref[...], preferred_element_type=F32)
         + g2_ref[...].astype(F32) * jnp.dot(y_conv, wc_ref[...], preferred_element_type=F32))
    o = jnp.dot(m.astype(BF16), wo_ref[...], preferred_element_type=F32)
    ms = jnp.mean(o * o, axis=-1, keepdims=True)
    o_ref[...] = x_ref[...] + o * lax.rsqrt(ms + RMS_EPS) * pg_ref[...]


def _merge(x2, y_fox, y_diff, pa, ps, pg, conv_w, w_fox, w_diff, w_conv, w_out, post_g, seq):
    t = x2.shape[0]
    tiles_per_seq = seq // TM_OUT
    halo_blocks = TM_OUT // SUBLANES

    def tile(col):
        return pl.BlockSpec((TM_OUT, WIDTH), lambda i: (i, col))

    def halo(col):
        return pl.BlockSpec((SUBLANES, WIDTH),
                            lambda i: (jnp.maximum(i * halo_blocks - 1, 0), col))

    def whole(shape):
        return pl.BlockSpec(shape, lambda i: (0, 0))

    return pl.pallas_call(
        functools.partial(_merge_kernel, tiles_per_seq=tiles_per_seq),
        grid=(t // TM_OUT,),
        in_specs=[
            pl.BlockSpec((TM_OUT, D_MODEL), lambda i: (i, 0)),
            pl.BlockSpec((TM_OUT, WIDTH), lambda i: (i, 0)),
            pl.BlockSpec((TM_OUT, WIDTH), lambda i: (i, 0)),
            tile(A_CB), tile(A_CC), tile(A_CX), tile(S_CG),
            halo(A_CC), halo(A_CX),
            pl.BlockSpec((TM_OUT, D_MODEL), lambda i: (i, 0)),
            pl.BlockSpec((TM_OUT, D_MODEL), lambda i: (i, 1)),
            pl.BlockSpec((TM_OUT, D_MODEL), lambda i: (i, 2)),
            whole((CONV_K, WIDTH)),
            whole((WIDTH, D_MODEL)), whole((WIDTH, D_MODEL)), whole((WIDTH, D_MODEL)),
            whole((D_MODEL, D_MODEL)),
            whole((1, D_MODEL)),
        ],
        out_specs=pl.BlockSpec((TM_OUT, D_MODEL), lambda i: (i, 0)),
        out_shape=jax.ShapeDtypeStruct((t, D_MODEL), F32),
        scratch_shapes=[pltpu.VMEM((TM_OUT + SUBLANES, WIDTH), F32)],
        compiler_params=pltpu.CompilerParams(
            dimension_semantics=("arbitrary",), vmem_limit_bytes=VMEM_LIMIT),
        name="merge",
    )(x2, y_fox, y_diff, pa, pa, pa, ps, pa, pa, pg, pg, pg,
      conv_w, w_fox, w_diff, w_conv, w_out, post_g)


def _rope_lane_tables(positions):
    half = ROT_DIM // 2
    inv_freq = ROPE_THETA ** (-jnp.arange(0, ROT_DIM, 2, dtype=F32) / ROT_DIM)
    ang = positions.astype(F32).reshape(-1, 1) * inv_freq
    cos, sin = jnp.cos(ang), jnp.sin(ang)
    t = ang.shape[0]
    ones = jnp.ones((t, HEAD_DIM - ROT_DIM), F32)
    zeros_rest = jnp.zeros((t, HEAD_DIM - ROT_DIM), F32)
    zeros_half = jnp.zeros((t, half), F32)
    rep = LANES // HEAD_DIM
    c = jnp.concatenate([cos, cos, ones] * rep, axis=1)
    s1 = jnp.concatenate([-sin, zeros_half, zeros_rest] * rep, axis=1)
    s2 = jnp.concatenate([zeros_half, sin, zeros_rest] * rep, axis=1)
    return c, s1, s2


def _gather_cols(w, offsets):
    return jnp.concatenate([w[:, o:o + WIDTH] for o in offsets], axis=1).astype(BF16)


def _scale_vec(n_blocks, scaled_block):
    blk = jnp.arange(n_blocks * WIDTH, dtype=jnp.int32) // WIDTH
    return jnp.where(blk == scaled_block, HEAD_DIM ** -0.5, 1.0).astype(F32).reshape(1, -1)


def kernel(x, positions, pre_norm_g, w_in, b_forget, b_merge, conv_w, lam_q1, lam_k1, lam_q2,
           lam_k2, diff_norm_g, w_br_fox, w_br_diff, w_br_conv, w_out, post_norm_g):
    batch, seq, _ = x.shape
    depth = w_in.shape[0]
    assert seq % TQ == 0 and TQ == TK == TS_FF and (batch * seq) % TM_IN == 0
    rc, rs1, rs2 = _rope_lane_tables(positions)
    x2 = x.reshape(batch * seq, D_MODEL)
    for l in range(depth):
        lam_init = 0.8 - 0.6 * math.exp(-0.3 * l)
        w = w_in[l]
        w_plain = _gather_cols(w, [_OFF_FQ, _OFF_FK, _OFF_FV, _OFF_DV, _OFF_CB, _OFF_CC, _OFF_CX])
        w_silu = _gather_cols(w, [_OFF_FG, _OFF_DG, _OFF_CG])
        w_rot = _gather_cols(w, [_OFF_DQ, _OFF_DK])
        w_gate = w[:, _OFF_MG:_OFF_MG + N_BRANCH * D_MODEL].astype(BF16)
        w_ff_t = w[:, _OFF_FF:_OFF_FF + FOX_HEADS].T.astype(BF16)
        pa, h = _plain_proj(x2, pre_norm_g[l].reshape(1, D_MODEL), w_plain,
                            _scale_vec(7, A_FQ), TN_PLAIN)
        ps = _epilogue_proj(_silu_proj_kernel, "proj_silu", h, w_silu, TN_SILU, [], [])
        pr = _epilogue_proj(_rot_proj_kernel, "proj_rotary", h, w_rot, TN_ROT,
                            [_scale_vec(2, R_DQ), rc, rs1, rs2],
                            [_col_vec(TN_ROT), _ROPE_TILE, _ROPE_TILE, _ROPE_TILE])
        pg = _epilogue_proj(_gate_proj_kernel, "proj_gate", h, w_gate, TN_GATE,
                            [b_merge[l].reshape(1, N_BRANCH * D_MODEL)], [_col_vec(TN_GATE)])
        crow, ccol = _forget_cumsum(h, w_ff_t, b_forget[l].reshape(FOX_HEADS, 1), batch, seq)
        y_fox = _fox_attention(pa, ps, crow, ccol, batch, seq)
        lam_vecs = jnp.stack([lam_q1[l], lam_k1[l], lam_q2[l], lam_k2[l]])
        y_diff = _diff_attention(pr, pa, ps, lam_vecs, diff_norm_g[l].reshape(1, LANES),
                                 lam_init, batch, seq)
        x2 = _merge(x2, y_fox, y_diff, pa, ps, pg, conv_w[l],
                    w_br_fox[l].astype(BF16), w_br_diff[l].astype(BF16),
                    w_br_conv[l].astype(BF16), w_out[l].astype(BF16),
                    post_norm_g[l].reshape(1, D_MODEL), seq)
    return x2.reshape(batch, seq, D_MODEL)
```

```python
import functools
import math
from typing import Any, Callable, NamedTuple, Optional

import jax
import jax.numpy as jnp
from jax import lax
from jax.experimental import pallas as pl
from jax.experimental.pallas import tpu as pltpu

F32 = jnp.float32
BF16 = jnp.bfloat16

D_MODEL = 1024
HEAD_DIM = 64
FOX_HEADS = 8
DIFF_HEADS = 4
WIDTH = 512
CONV_K = 3
N_BRANCH = 3
ROPE_THETA = 500000.0
ROT_DIM = HEAD_DIM // 4
RMS_EPS = 1e-6
LANES = 128
SUBLANES = 8

(A_FQ, A_FK, A_FV, A_DV, A_CB, A_CC, A_CX) = range(7)
(S_FG, S_DG, S_CG) = range(3)
(R_DQ, R_DK) = range(2)
N_GATE_BLOCKS = N_BRANCH * D_MODEL // WIDTH

_OFF_FQ, _OFF_FK, _OFF_FV = 0, 512, 1024
_OFF_FF = 1536
_OFF_FG = 1544
_OFF_DQ, _OFF_DK, _OFF_DV, _OFF_DG = 2056, 2568, 3080, 3592
_OFF_CB, _OFF_CC, _OFF_CX, _OFF_CG = 4104, 4616, 5128, 5640
_OFF_MG = 6152

TM_IN = 1024
TM_EPI = 1024
TN_PLAIN = 1792
TN_SILU = 1536
TN_ROT = 1024
TN_GATE = 1536
TS_FF = 512
TQ = 512
TK = 512
TM_OUT = 512
VMEM_LIMIT = 48 * 1024 * 1024


def _silu(v):
    return v * jax.nn.sigmoid(v)


def _rotary_block(acc, c, s1, s2):
    outs = []
    for i in range(acc.shape[1] // LANES):
        v = acc[:, i * LANES:(i + 1) * LANES]
        outs.append(v * c + pltpu.roll(v, LANES - ROT_DIM // 2, 1) * s1
                    + pltpu.roll(v, ROT_DIM // 2, 1) * s2)
    return jnp.concatenate(outs, axis=1)


def _plain_proj_kernel(x_ref, g_ref, w_ref, sc_ref, o_ref, h_ref, h_scr):
    @pl.when(pl.program_id(1) == 0)
    def _():
        x = x_ref[...]
        ms = jnp.mean(x * x, axis=-1, keepdims=True)
        h = (x * lax.rsqrt(ms + RMS_EPS) * g_ref[...]).astype(BF16)
        h_scr[...] = h
        h_ref[...] = h

    acc = jnp.dot(h_scr[...], w_ref[...], preferred_element_type=F32)
    o_ref[...] = (acc * sc_ref[...]).astype(BF16)


def _silu_proj_kernel(h_ref, w_ref, o_ref):
    acc = jnp.dot(h_ref[...], w_ref[...], preferred_element_type=F32)
    o_ref[...] = _silu(acc).astype(BF16)


def _rot_proj_kernel(h_ref, w_ref, sc_ref, rc_ref, rs1_ref, rs2_ref, o_ref):
    acc = jnp.dot(h_ref[...], w_ref[...], preferred_element_type=F32)
    r = _rotary_block(acc, rc_ref[...], rs1_ref[...], rs2_ref[...])
    o_ref[...] = (r * sc_ref[...]).astype(BF16)


def _gate_proj_kernel(h_ref, w_ref, b_ref, o_ref):
    acc = jnp.dot(h_ref[...], w_ref[...], preferred_element_type=F32)
    o_ref[...] = jax.nn.sigmoid(acc + b_ref[...]).astype(BF16)


_PROJ_PARAMS = pltpu.CompilerParams(
    dimension_semantics=("arbitrary", "arbitrary"), vmem_limit_bytes=VMEM_LIMIT)
def _row_tile(tm, width):
    return pl.BlockSpec((tm, width), lambda i, j: (i, 0))


def _w_block(tn):
    return pl.BlockSpec((D_MODEL, tn), lambda i, j: (0, j))


def _col_vec(tn):
    return pl.BlockSpec((1, tn), lambda i, j: (0, j))


def _out_block(tm, tn):
    return pl.BlockSpec((tm, tn), lambda i, j: (i, j))


def _plain_proj(x2, pre_g, w, col_scale, tn):
    t, n = x2.shape[0], w.shape[1]
    return pl.pallas_call(
        _plain_proj_kernel,
        grid=(t // TM_IN, n // tn),
        in_specs=[_row_tile(TM_IN, D_MODEL), pl.BlockSpec((1, D_MODEL), lambda i, j: (0, 0)),
                  _w_block(tn), _col_vec(tn)],
        out_specs=[_out_block(TM_IN, tn), _row_tile(TM_IN, D_MODEL)],
        out_shape=[jax.ShapeDtypeStruct((t, n), BF16), jax.ShapeDtypeStruct((t, D_MODEL), BF16)],
        scratch_shapes=[pltpu.VMEM((TM_IN, D_MODEL), BF16)],
        compiler_params=_PROJ_PARAMS,
        name="proj_plain",
    )(x2, pre_g, w, col_scale)


def _epilogue_proj(body, name, h, w, tn, extra, extra_specs):
    t, n = h.shape[0], w.shape[1]
    return pl.pallas_call(
        body,
        grid=(t // TM_EPI, n // tn),
        in_specs=[_row_tile(TM_EPI, D_MODEL), _w_block(tn)] + extra_specs,
        out_specs=_out_block(TM_EPI, tn),
        out_shape=jax.ShapeDtypeStruct((t, n), BF16),
        compiler_params=_PROJ_PARAMS,
        name=name,
    )(h, w, *extra)


def _split3(v):
    hi = v.astype(BF16)
    r1 = v - hi.astype(F32)
    mid = r1.astype(BF16)
    lo = (r1 - mid.astype(F32)).astype(BF16)
    return hi, mid, lo


def _forget_kernel(h_ref, wt_ref, bf_ref, crow_ref, ccol_ref, carry_scr):
    si = pl.program_id(1)

    @pl.when(si == 0)
    def _():
        carry_scr[...] = jnp.zeros_like(carry_scr)

    ff = lax.dot_general(wt_ref[...], h_ref[...], (((1,), (1,)), ((), ())),
                         preferred_element_type=F32)
    lf = jax.nn.log_sigmoid(ff + bf_ref[...])
    row = lax.broadcasted_iota(jnp.int32, (TS_FF, TS_FF), 0)
    col = lax.broadcasted_iota(jnp.int32, (TS_FF, TS_FF), 1)
    upper = (row <= col).astype(BF16)
    cum = jnp.zeros((FOX_HEADS, TS_FF), F32)
    for piece in _split3(lf):
        cum = cum + jnp.dot(piece, upper, preferred_element_type=F32)
    c = cum + carry_scr[:, 0:1]
    carry_scr[...] = jnp.broadcast_to(c[:, TS_FF - 1:TS_FF], carry_scr.shape)
    crow_ref[0, 0] = c
    padded = jnp.concatenate([c, jnp.zeros((LANES - FOX_HEADS, TS_FF), F32)], axis=0)
    ccol_ref[...] = padded.T


def _forget_cumsum(h, w_ff_t, b_f, batch, seq):
    nt = seq // TS_FF
    return pl.pallas_call(
        _forget_kernel,
        grid=(batch, nt),
        in_specs=[
            pl.BlockSpec((TS_FF, D_MODEL), lambda b, s: (b * nt + s, 0)),
            pl.BlockSpec((FOX_HEADS, D_MODEL), lambda b, s: (0, 0)),
            pl.BlockSpec((FOX_HEADS, 1), lambda b, s: (0, 0)),
        ],
        out_specs=[
            pl.BlockSpec((1, 1, FOX_HEADS, TS_FF), lambda b, s: (b, s, 0, 0)),
            pl.BlockSpec((TS_FF, LANES), lambda b, s: (b * nt + s, 0)),
        ],
        out_shape=[
            jax.ShapeDtypeStruct((batch, nt, FOX_HEADS, TS_FF), F32),
            jax.ShapeDtypeStruct((batch * seq, LANES), F32),
        ],
        scratch_shapes=[pltpu.VMEM((FOX_HEADS, LANES), F32)],
        compiler_params=pltpu.CompilerParams(
            dimension_semantics=("arbitrary", "arbitrary"), vmem_limit_bytes=VMEM_LIMIT),
        name="forget_cumsum",
    )(h, w_ff_t, b_f)


class _Map(NamedTuple):
    slot: int
    blk: int
    key_bias: Optional[Callable[[Any], Any]]
    row_shift: Any


def _attention_maps(maps, qi, qm_scr, k_ref, v_ref, bufs, mb_scr, ls_scr, acc_scr, emit):
    row = lax.broadcasted_iota(jnp.int32, (TQ, TK), 0)
    col = lax.broadcasted_iota(jnp.int32, (TQ, TK), 1)
    causal = col <= row
    n_lane_blocks = TK // LANES

    def lanes(v, c):
        return v[:, c * LANES:(c + 1) * LANES]

    def chunk_rows(ref, mp, ki):
        start = pl.multiple_of(ki * TK, TK)
        return ref[pl.ds(start, TK), mp.blk * LANES:(mp.blk + 1) * LANES]

    def pass1_begin(buf):
        bufs[buf][1][...] = jnp.full((TQ, LANES), -jnp.inf, F32)

    def pass1_chunk(mp, buf, ki, diagonal):
        t_scr, mx_scr = bufs[buf]
        s = lax.dot_general(qm_scr[mp.slot], chunk_rows(k_ref, mp, ki),
                            (((1,), (1,)), ((), ())), preferred_element_type=F32)
        if mp.key_bias is not None:
            s = s - mp.key_bias(ki)
        if diagonal:
            s = jnp.where(causal, s, -jnp.inf)
        t_scr[ki] = s
        m = mx_scr[...]
        for c in range(n_lane_blocks):
            m = jnp.maximum(m, lanes(s, c))
        mx_scr[...] = m

    def pass2_begin(mp, buf):
        m = jnp.max(bufs[buf][1][...], axis=1, keepdims=True)
        if mp.row_shift is not None:
            m = (m + mp.row_shift) - mp.row_shift
        mb_scr[...] = jnp.broadcast_to(m, (TQ, LANES))
        ls_scr[...] = jnp.zeros((TQ, LANES), F32)
        acc_scr[...] = jnp.zeros((TQ, LANES), F32)

    def pass2_chunk(mp, buf, ki):
        t = bufs[buf][0][ki]
        mb = mb_scr[...]
        ps = [jnp.exp(lanes(t, c) - mb) for c in range(n_lane_blocks)]
        ls = ls_scr[...]
        for p in ps:
            ls = ls + p
        ls_scr[...] = ls
        p = jnp.concatenate(ps, axis=1).astype(BF16)
        acc_scr[...] += jnp.dot(p, chunk_rows(v_ref, mp, ki), preferred_element_type=F32)

    def pass2_end(n):
        l = jnp.sum(ls_scr[...], axis=1, keepdims=True)
        emit(n, acc_scr[...] / l)

    is_odd = lax.bitwise_and(qi, 1) == 1

    def pairs(body):
        def step(i, carry):
            body(2 * i)
            body(2 * i + 1)
            return carry
        lax.fori_loop(0, lax.shift_right_logical(qi, 1), step, 0)

    def between_loops(region):
        pl.when(is_odd)(functools.partial(region, True))
        pl.when(jnp.logical_not(is_odd))(functools.partial(region, False))

    def begin_stage(n):
        pass2_begin(maps[n], n % 2)
        if n + 1 < len(maps):
            pass1_begin((n + 1) % 2)
            pass1_chunk(maps[n + 1], (n + 1) % 2, qi, True)

    pass1_begin(0)
    pass1_chunk(maps[0], 0, qi, True)
    pairs(lambda ki: pass1_chunk(maps[0], 0, ki, False))

    def after_first(odd):
        if odd:
            pass1_chunk(maps[0], 0, qi - 1, False)
        begin_stage(0)

    between_loops(after_first)
    for n in range(1, len(maps) + 1):
        def both(ki, n=n):
            pass2_chunk(maps[n - 1], (n - 1) % 2, ki)
            if n < len(maps):
                pass1_chunk(maps[n], n % 2, ki, False)

        pairs(both)

        def after(odd, n=n, both=both):
            if odd:
                both(qi - 1)
            pass2_chunk(maps[n - 1], (n - 1) % 2, qi)
            pass2_end(n - 1)
            if n < len(maps):
                begin_stage(n)

        between_loops(after)


def _store_masked_queries(q_ref, qm_scr):
    first = lax.broadcasted_iota(jnp.int32, (TQ, LANES), 1) < HEAD_DIM
    for b in range(WIDTH // LANES):
        q = q_ref[:, b * LANES:(b + 1) * LANES]
        zero = jnp.zeros_like(q)
        qm_scr[2 * b] = jnp.where(first, q, zero)
        qm_scr[2 * b + 1] = jnp.where(first, zero, q)


def _fox_kernel(q_ref, k_ref, v_ref, g_ref, crow_ref, ccol_ref, o_ref,
                qm_scr, ta_scr, tb_scr, mxa_scr, mxb_scr, mb_scr, ls_scr, acc_scr, res_scr):
    qi = pl.program_id(1)
    first = lax.broadcasted_iota(jnp.int32, (TQ, LANES), 1) < HEAD_DIM
    _store_masked_queries(q_ref, qm_scr)
    ccol = ccol_ref[...]
    maps = [
        _Map(slot=head, blk=head // 2,
             key_bias=lambda ki, head=head: crow_ref[0, ki, head:head + 1, :],
             row_shift=ccol[:, head:head + 1])
        for head in range(FOX_HEADS)
    ]

    def emit(head, o):
        if head % 2 == 0:
            res_scr[...] = o
        else:
            blk = slice((head // 2) * LANES, (head // 2 + 1) * LANES)
            y = jnp.where(first, res_scr[...], o) * g_ref[:, blk].astype(F32)
            o_ref[:, blk] = y.astype(BF16)

    _attention_maps(maps, qi, qm_scr, k_ref, v_ref, ((ta_scr, mxa_scr), (tb_scr, mxb_scr)),
                    mb_scr, ls_scr, acc_scr, emit)


def _diff_kernel(q_ref, k_ref, v_ref, g_ref, lam_ref, ng_ref, o_ref,
                 qm_scr, ta_scr, tb_scr, mxa_scr, mxb_scr, mb_scr, ls_scr, acc_scr, res_scr,
                 *, lam_init):
    qi = pl.program_id(1)
    _store_masked_queries(q_ref, qm_scr)
    lv = lam_ref[...]
    lam = (jnp.exp(jnp.sum(lv[0:1] * lv[1:2], axis=1, keepdims=True))
           - jnp.exp(jnp.sum(lv[2:3] * lv[3:4], axis=1, keepdims=True)) + lam_init)
    maps = [_Map(slot=n, blk=n // 2, key_bias=None, row_shift=None)
            for n in range(2 * DIFF_HEADS)]

    def emit(n, o):
        if n % 2 == 0:
            res_scr[...] = o
        else:
            blk = slice((n // 2) * LANES, (n // 2 + 1) * LANES)
            d = res_scr[...] - lam * o
            ms = jnp.mean(d * d, axis=-1, keepdims=True)
            y = d * lax.rsqrt(ms + RMS_EPS) * ng_ref[...] * (1.0 - lam_init)
            o_ref[:, blk] = (y * g_ref[:, blk].astype(F32)).astype(BF16)

    _attention_maps(maps, qi, qm_scr, k_ref, v_ref, ((ta_scr, mxa_scr), (tb_scr, mxb_scr)),
                    mb_scr, ls_scr, acc_scr, emit)


def _attn_scratch(seq):
    stat = pltpu.VMEM((TQ, LANES), F32)
    logits = pltpu.VMEM((seq // TK, TQ, TK), F32)
    return [pltpu.VMEM((2 * WIDTH // LANES, TQ, LANES), BF16),
            logits, logits, stat, stat,
            stat, stat, stat, stat]


def _fox_attention(pa, ps, crow, ccol, batch, seq):
    nq = seq // TQ
    return pl.pallas_call(
        _fox_kernel,
        grid=(batch, nq),
        in_specs=[
            pl.BlockSpec((TQ, WIDTH), lambda b, i: (b * nq + i, A_FQ)),
            pl.BlockSpec((seq, WIDTH), lambda b, i: (b, A_FK)),
            pl.BlockSpec((seq, WIDTH), lambda b, i: (b, A_FV)),
            pl.BlockSpec((TQ, WIDTH), lambda b, i: (b * nq + i, S_FG)),
            pl.BlockSpec((1, seq // TK, FOX_HEADS, TK), lambda b, i: (b, 0, 0, 0)),
            pl.BlockSpec((TQ, LANES), lambda b, i: (b * nq + i, 0)),
        ],
        out_specs=pl.BlockSpec((TQ, WIDTH), lambda b, i: (b * nq + i, 0)),
        out_shape=jax.ShapeDtypeStruct((batch * seq, WIDTH), BF16),
        scratch_shapes=_attn_scratch(seq),
        compiler_params=pltpu.CompilerParams(
            dimension_semantics=("arbitrary", "arbitrary"), vmem_limit_bytes=VMEM_LIMIT),
        name="fox_attention",
    )(pa, pa, pa, ps, crow, ccol)


def _diff_attention(pr, pa, ps, lam_vecs, norm_g, lam_init, batch, seq):
    nq = seq // TQ
    return pl.pallas_call(
        functools.partial(_diff_kernel, lam_init=lam_init),
        grid=(batch, nq),
        in_specs=[
            pl.BlockSpec((TQ, WIDTH), lambda b, i: (b * nq + i, R_DQ)),
            pl.BlockSpec((seq, WIDTH), lambda b, i: (b, R_DK)),
            pl.BlockSpec((seq, WIDTH), lambda b, i: (b, A_DV)),
            pl.BlockSpec((TQ, WIDTH), lambda b, i: (b * nq + i, S_DG)),
            pl.BlockSpec((4, HEAD_DIM), lambda b, i: (0, 0)),
            pl.BlockSpec((1, LANES), lambda b, i: (0, 0)),
        ],
        out_specs=pl.BlockSpec((TQ, WIDTH), lambda b, i: (b * nq + i, 0)),
        out_shape=jax.ShapeDtypeStruct((batch * seq, WIDTH), BF16),
        scratch_shapes=_attn_scratch(seq),
        compiler_params=pltpu.CompilerParams(
            dimension_semantics=("arbitrary", "arbitrary"), vmem_limit_bytes=VMEM_LIMIT),
        name="diff_attention",
    )(pr, pr, pa, ps, lam_vecs, norm_g)


def _merge_kernel(x_ref, yf_ref, yd_ref, cb_ref, cc_ref, cx_ref, cg_ref, hc_ref, hx_ref,
                  g0_ref, g1_ref, g2_ref, cw_ref, wf_ref, wd_ref, wc_ref, wo_ref, pg_ref,
                  o_ref, ext_scr, *, tiles_per_seq):
    i = pl.program_id(0)
    halo = SUBLANES
    u = cc_ref[...].astype(F32) * cx_ref[...].astype(F32)
    prev = hc_ref[...].astype(F32) * hx_ref[...].astype(F32)
    prev = jnp.where(i % tiles_per_seq == 0, jnp.zeros_like(prev), prev)
    ext_scr[0:halo, :] = prev
    ext_scr[halo:halo + TM_OUT, :] = u
    cw = cw_ref[...]
    conv = (cw[2:3] * u + cw[1:2] * ext_scr[halo - 1:halo - 1 + TM_OUT, :]
            + cw[0:1] * ext_scr[halo - 2:halo - 2 + TM_OUT, :])
    y_conv = (cb_ref[...].astype(F32) * conv * cg_ref[...].astype(F32)).astype(BF16)

    m = (g0_ref[...].astype(F32) * jnp.dot(yf_ref[...], wf_ref[...], preferred_element_type=F32)
         + g1_ref[...].astype(F32) * jnp.dot(yd_ref[...], wd_ref[...], preferred_element_type=F32)
         + g2_ref[...].astype(F32) * jnp.dot(y_conv, wc_ref[...], preferred_element_type=F32))
    o = jnp.dot(m.astype(BF16), wo_ref[...], preferred_element_type=F32)
    ms = jnp.mean(o * o, axis=-1, keepdims=True)
    o_ref[...] = x_ref[...] + o * lax.rsqrt(ms + RMS_EPS) * pg_ref[...]


def _merge(x2, y_fox, y_diff, pa, ps, pg, conv_w, w_fox, w_diff, w_conv, w_out, post_g, seq):
    t = x2.shape[0]
    tiles_per_seq = seq // TM_OUT
    halo_blocks = TM_OUT // SUBLANES

    def tile(col):
        return pl.BlockSpec((TM_OUT, WIDTH), lambda i: (i, col))

    def halo(col):
        return pl.BlockSpec((SUBLANES, WIDTH),
                            lambda i: (jnp.maximum(i * halo_blocks - 1, 0), col))

    def whole(shape):
        return pl.BlockSpec(shape, lambda i: (0, 0))

    return pl.pallas_call(
        functools.partial(_merge_kernel, tiles_per_seq=tiles_per_seq),
        grid=(t // TM_OUT,),
        in_specs=[
            pl.BlockSpec((TM_OUT, D_MODEL), lambda i: (i, 0)),
            pl.BlockSpec((TM_OUT, WIDTH), lambda i: (i, 0)),
            pl.BlockSpec((TM_OUT, WIDTH), lambda i: (i, 0)),
            tile(A_CB), tile(A_CC), tile(A_CX), tile(S_CG),
            halo(A_CC), halo(A_CX),
            pl.BlockSpec((TM_OUT, D_MODEL), lambda i: (i, 0)),
            pl.BlockSpec((TM_OUT, D_MODEL), lambda i: (i, 1)),
            pl.BlockSpec((TM_OUT, D_MODEL), lambda i: (i, 2)),
            whole((CONV_K, WIDTH)),
            whole((WIDTH, D_MODEL)), whole((WIDTH, D_MODEL)), whole((WIDTH, D_MODEL)),
            whole((D_MODEL, D_MODEL)),
            whole((1, D_MODEL)),
        ],
        out_specs=pl.BlockSpec((TM_OUT, D_MODEL), lambda i: (i, 0)),
        out_shape=jax.ShapeDtypeStruct((t, D_MODEL), F32),
        scratch_shapes=[pltpu.VMEM((TM_OUT + SUBLANES, WIDTH), F32)],
        compiler_params=pltpu.CompilerParams(
            dimension_semantics=("arbitrary",), vmem_limit_bytes=VMEM_LIMIT),
        name="merge",
    )(x2, y_fox, y_diff, pa, pa, pa, ps, pa, pa, pg, pg, pg,
      conv_w, w_fox, w_diff, w_conv, w_out, post_g)


def _rope_lane_tables(positions):
    half = ROT_DIM // 2
    inv_freq = ROPE_THETA ** (-jnp.arange(0, ROT_DIM, 2, dtype=F32) / ROT_DIM)
    ang = positions.astype(F32).reshape(-1, 1) * inv_freq
    cos, sin = jnp.cos(ang), jnp.sin(ang)
    t = ang.shape[0]
    ones = jnp.ones((t, HEAD_DIM - ROT_DIM), F32)
    zeros_rest = jnp.zeros((t, HEAD_DIM - ROT_DIM), F32)
    zeros_half = jnp.zeros((t, half), F32)
    c64 = jnp.concatenate([cos, cos, ones], axis=1)
    s1_64 = jnp.concatenate([-sin, zeros_half, zeros_rest], axis=1)
    s2_64 = jnp.concatenate([zeros_half, sin, zeros_rest], axis=1)
    rep = LANES // HEAD_DIM
    return jnp.tile(c64, (1, rep)), jnp.tile(s1_64, (1, rep)), jnp.tile(s2_64, (1, rep))


def _gather_cols(w, offsets):
    return jnp.concatenate([w[:, o:o + WIDTH] for o in offsets], axis=1).astype(BF16)


def _scale_vec(n_blocks, scaled_block):
    blk = jnp.arange(n_blocks * WIDTH, dtype=jnp.int32) // WIDTH
    return jnp.where(blk == scaled_block, HEAD_DIM ** -0.5, 1.0).astype(F32).reshape(1, -1)


def kernel(x, positions, pre_norm_g, w_in, b_forget, b_merge, conv_w, lam_q1, lam_k1, lam_q2,
           lam_k2, diff_norm_g, w_br_fox, w_br_diff, w_br_conv, w_out, post_norm_g):
    batch, seq, _ = x.shape
    depth = w_in.shape[0]
    assert seq % TQ == 0 and TQ == TK == TS_FF
    assert (batch * seq) % TM_IN == 0 and (batch * seq) % TM_EPI == 0
    rc, rs1, rs2 = _rope_lane_tables(positions)
    x2 = x.reshape(batch * seq, D_MODEL)
    for l in range(depth):
        lam_init = 0.8 - 0.6 * math.exp(-0.3 * l)
        w = w_in[l]
        w_plain = _gather_cols(w, [_OFF_FQ, _OFF_FK, _OFF_FV, _OFF_DV, _OFF_CB, _OFF_CC, _OFF_CX])
        w_silu = _gather_cols(w, [_OFF_FG, _OFF_DG, _OFF_CG])
        w_rot = _gather_cols(w, [_OFF_DQ, _OFF_DK])
        w_gate = w[:, _OFF_MG:_OFF_MG + N_BRANCH * D_MODEL].astype(BF16)
        w_ff_t = w[:, _OFF_FF:_OFF_FF + FOX_HEADS].T.astype(BF16)
        pa, h = _plain_proj(x2, pre_norm_g[l].reshape(1, D_MODEL), w_plain,
                            _scale_vec(7, A_FQ), TN_PLAIN)
        ps = _epilogue_proj(_silu_proj_kernel, "proj_silu", h, w_silu, TN_SILU, [], [])
        pr = _epilogue_proj(_rot_proj_kernel, "proj_rotary", h, w_rot, TN_ROT,
                            [_scale_vec(2, R_DQ), rc, rs1, rs2],
                            [_col_vec(TN_ROT)] + [_row_tile(TM_EPI, LANES)] * 3)
        pg = _epilogue_proj(_gate_proj_kernel, "proj_gate", h, w_gate, TN_GATE,
                            [b_merge[l].reshape(1, N_BRANCH * D_MODEL)], [_col_vec(TN_GATE)])
        crow, ccol = _forget_cumsum(h, w_ff_t, b_forget[l].reshape(FOX_HEADS, 1), batch, seq)
        y_fox = _fox_attention(pa, ps, crow, ccol, batch, seq)
        lam_vecs = jnp.stack([lam_q1[l], lam_k1[l], lam_q2[l], lam_k2[l]])
        y_diff = _diff_attention(pr, pa, ps, lam_vecs, diff_norm_g[l].reshape(1, LANES),
                                 lam_init, batch, seq)
        x2 = _merge(x2, y_fox, y_diff, pa, ps, pg, conv_w[l],
                    w_br_fox[l].astype(BF16), w_br_diff[l].astype(BF16),
                    w_br_conv[l].astype(BF16), w_out[l].astype(BF16),
                    post_norm_g[l].reshape(1, D_MODEL), seq)
    return x2.reshape(batch, seq, D_MODEL)
```

```python
import functools
import math
from typing import Any, Callable, NamedTuple, Optional

import jax
import jax.numpy as jnp
from jax import lax
from jax.experimental import pallas as pl
from jax.experimental.pallas import tpu as pltpu

F32 = jnp.float32
BF16 = jnp.bfloat16

D_MODEL = 1024
HEAD_DIM = 64
FOX_HEADS = 8
DIFF_HEADS = 4
WIDTH = 512
CONV_K = 3
N_BRANCH = 3
ROPE_THETA = 500000.0
ROT_DIM = HEAD_DIM // 4
RMS_EPS = 1e-6
LANES = 128
SUBLANES = 8

(A_FQ, A_FK, A_FV, A_DV, A_CB, A_CC, A_CX) = range(7)
(S_FG, S_DG, S_CG) = range(3)
(R_DQ, R_DK) = range(2)
N_GATE_BLOCKS = N_BRANCH * D_MODEL // WIDTH

_OFF_FQ, _OFF_FK, _OFF_FV = 0, 512, 1024
_OFF_FF = 1536
_OFF_FG = 1544
_OFF_DQ, _OFF_DK, _OFF_DV, _OFF_DG = 2056, 2568, 3080, 3592
_OFF_CB, _OFF_CC, _OFF_CX, _OFF_CG = 4104, 4616, 5128, 5640
_OFF_MG = 6152

TM_IN = 1024
TN_PLAIN = 1792
TN_SILU = 1536
TN_ROT = 1024
TN_GATE = 1536
TS_FF = 512
TQ = 512
TK = 512
TM_OUT = 512
VMEM_LIMIT = 48 * 1024 * 1024


def _silu(v):
    return v * jax.nn.sigmoid(v)


def _rotary_block(acc, c, s1, s2):
    outs = []
    for i in range(acc.shape[1] // LANES):
        v = acc[:, i * LANES:(i + 1) * LANES]
        outs.append(v * c + pltpu.roll(v, LANES - ROT_DIM // 2, 1) * s1
                    + pltpu.roll(v, ROT_DIM // 2, 1) * s2)
    return jnp.concatenate(outs, axis=1)


def _plain_proj_kernel(x_ref, g_ref, w_ref, sc_ref, o_ref, h_ref, h_scr):
    @pl.when(pl.program_id(1) == 0)
    def _():
        x = x_ref[...]
        ms = jnp.mean(x * x, axis=-1, keepdims=True)
        h = (x * lax.rsqrt(ms + RMS_EPS) * g_ref[...]).astype(BF16)
        h_scr[...] = h
        h_ref[...] = h

    acc = jnp.dot(h_scr[...], w_ref[...], preferred_element_type=F32)
    o_ref[...] = (acc * sc_ref[...]).astype(BF16)


def _silu_proj_kernel(h_ref, w_ref, o_ref):
    acc = jnp.dot(h_ref[...], w_ref[...], preferred_element_type=F32)
    o_ref[...] = _silu(acc).astype(BF16)


def _rot_proj_kernel(h_ref, w_ref, sc_ref, rc_ref, rs1_ref, rs2_ref, o_ref):
    acc = jnp.dot(h_ref[...], w_ref[...], preferred_element_type=F32)
    r = _rotary_block(acc, rc_ref[...], rs1_ref[...], rs2_ref[...])
    o_ref[...] = (r * sc_ref[...]).astype(BF16)


def _gate_proj_kernel(h_ref, w_ref, b_ref, o_ref):
    acc = jnp.dot(h_ref[...], w_ref[...], preferred_element_type=F32)
    o_ref[...] = jax.nn.sigmoid(acc + b_ref[...]).astype(BF16)


_PROJ_PARAMS = pltpu.CompilerParams(
    dimension_semantics=("arbitrary", "arbitrary"), vmem_limit_bytes=VMEM_LIMIT)
_ROW_TILE = pl.BlockSpec((TM_IN, D_MODEL), lambda i, j: (i, 0))
_ROPE_TILE = pl.BlockSpec((TM_IN, LANES), lambda i, j: (i, 0))


def _w_block(tn):
    return pl.BlockSpec((D_MODEL, tn), lambda i, j: (0, j))


def _col_vec(tn):
    return pl.BlockSpec((1, tn), lambda i, j: (0, j))


def _out_block(tn):
    return pl.BlockSpec((TM_IN, tn), lambda i, j: (i, j))


def _plain_proj(x2, pre_g, w, col_scale, tn):
    t, n = x2.shape[0], w.shape[1]
    return pl.pallas_call(
        _plain_proj_kernel,
        grid=(t // TM_IN, n // tn),
        in_specs=[_ROW_TILE, pl.BlockSpec((1, D_MODEL), lambda i, j: (0, 0)),
                  _w_block(tn), _col_vec(tn)],
        out_specs=[_out_block(tn), _ROW_TILE],
        out_shape=[jax.ShapeDtypeStruct((t, n), BF16), jax.ShapeDtypeStruct((t, D_MODEL), BF16)],
        scratch_shapes=[pltpu.VMEM((TM_IN, D_MODEL), BF16)],
        compiler_params=_PROJ_PARAMS,
        name="proj_plain",
    )(x2, pre_g, w, col_scale)


def _epilogue_proj(body, name, h, w, tn, extra, extra_specs):
    t, n = h.shape[0], w.shape[1]
    return pl.pallas_call(
        body,
        grid=(t // TM_IN, n // tn),
        in_specs=[_ROW_TILE, _w_block(tn)] + extra_specs,
        out_specs=_out_block(tn),
        out_shape=jax.ShapeDtypeStruct((t, n), BF16),
        compiler_params=_PROJ_PARAMS,
        name=name,
    )(h, w, *extra)


def _split3(v):
    hi = v.astype(BF16)
    r1 = v - hi.astype(F32)
    mid = r1.astype(BF16)
    lo = (r1 - mid.astype(F32)).astype(BF16)
    return hi, mid, lo


def _forget_kernel(h_ref, wt_ref, bf_ref, crow_ref, ccol_ref, carry_scr):
    si = pl.program_id(1)

    @pl.when(si == 0)
    def _():
        carry_scr[...] = jnp.zeros_like(carry_scr)

    ff = lax.dot_general(wt_ref[...], h_ref[...], (((1,), (1,)), ((), ())),
                         preferred_element_type=F32)
    lf = jax.nn.log_sigmoid(ff + bf_ref[...])
    row = lax.broadcasted_iota(jnp.int32, (TS_FF, TS_FF), 0)
    col = lax.broadcasted_iota(jnp.int32, (TS_FF, TS_FF), 1)
    upper = (row <= col).astype(BF16)
    cum = jnp.zeros((FOX_HEADS, TS_FF), F32)
    for piece in _split3(lf):
        cum = cum + jnp.dot(piece, upper, preferred_element_type=F32)
    c = cum + carry_scr[:, 0:1]
    carry_scr[...] = jnp.broadcast_to(c[:, TS_FF - 1:TS_FF], carry_scr.shape)
    crow_ref[0, 0] = c
    padded = jnp.concatenate([c, jnp.zeros((LANES - FOX_HEADS, TS_FF), F32)], axis=0)
    ccol_ref[...] = padded.T


def _forget_cumsum(h, w_ff_t, b_f, batch, seq):
    nt = seq // TS_FF
    return pl.pallas_call(
        _forget_kernel,
        grid=(batch, nt),
        in_specs=[
            pl.BlockSpec((TS_FF, D_MODEL), lambda b, s: (b * nt + s, 0)),
            pl.BlockSpec((FOX_HEADS, D_MODEL), lambda b, s: (0, 0)),
            pl.BlockSpec((FOX_HEADS, 1), lambda b, s: (0, 0)),
        ],
        out_specs=[
            pl.BlockSpec((1, 1, FOX_HEADS, TS_FF), lambda b, s: (b, s, 0, 0)),
            pl.BlockSpec((TS_FF, LANES), lambda b, s: (b * nt + s, 0)),
        ],
        out_shape=[
            jax.ShapeDtypeStruct((batch, nt, FOX_HEADS, TS_FF), F32),
            jax.ShapeDtypeStruct((batch * seq, LANES), F32),
        ],
        scratch_shapes=[pltpu.VMEM((FOX_HEADS, LANES), F32)],
        compiler_params=pltpu.CompilerParams(
            dimension_semantics=("arbitrary", "arbitrary"), vmem_limit_bytes=VMEM_LIMIT),
        name="forget_cumsum",
    )(h, w_ff_t, b_f)


class _Map(NamedTuple):
    slot: int
    blk: int
    key_bias: Optional[Callable[[Any], Any]]
    row_shift: Any


def _attention_maps(maps, qi, qm_scr, k_ref, v_ref, bufs, mb_scr, ls_scr, acc_scr, emit):
    row = lax.broadcasted_iota(jnp.int32, (TQ, TK), 0)
    col = lax.broadcasted_iota(jnp.int32, (TQ, TK), 1)
    causal = col <= row
    n_lane_blocks = TK // LANES

    def lanes(v, c):
        return v[:, c * LANES:(c + 1) * LANES]

    def chunk_rows(ref, mp, ki):
        start = pl.multiple_of(ki * TK, TK)
        return ref[pl.ds(start, TK), mp.blk * LANES:(mp.blk + 1) * LANES]

    def pass1_begin(buf):
        bufs[buf][1][...] = jnp.full((TQ, LANES), -jnp.inf, F32)

    def pass1_chunk(mp, buf, ki, diagonal):
        t_scr, mx_scr = bufs[buf]
        s = lax.dot_general(qm_scr[mp.slot], chunk_rows(k_ref, mp, ki),
                            (((1,), (1,)), ((), ())), preferred_element_type=F32)
        if mp.key_bias is not None:
            s = s - mp.key_bias(ki)
        if diagonal:
            s = jnp.where(causal, s, -jnp.inf)
        t_scr[ki] = s
        m = mx_scr[...]
        for c in range(n_lane_blocks):
            m = jnp.maximum(m, lanes(s, c))
        mx_scr[...] = m

    def pass2_begin(mp, buf):
        m = jnp.max(bufs[buf][1][...], axis=1, keepdims=True)
        if mp.row_shift is not None:
            m = (m + mp.row_shift) - mp.row_shift
        mb_scr[...] = jnp.broadcast_to(m, (TQ, LANES))
        ls_scr[...] = jnp.zeros((TQ, LANES), F32)
        acc_scr[...] = jnp.zeros((TQ, LANES), F32)

    def pass2_chunk(mp, buf, ki):
        t = bufs[buf][0][ki]
        mb = mb_scr[...]
        ps = [jnp.exp(lanes(t, c) - mb) for c in range(n_lane_blocks)]
        ls = ls_scr[...]
        for p in ps:
            ls = ls + p
        ls_scr[...] = ls
        p = jnp.concatenate(ps, axis=1).astype(BF16)
        acc_scr[...] += jnp.dot(p, chunk_rows(v_ref, mp, ki), preferred_element_type=F32)

    def pass2_end(n):
        l = jnp.sum(ls_scr[...], axis=1, keepdims=True)
        emit(n, acc_scr[...] / l)

    def loop(n_chunks, body):
        def step(i, carry):
            body(2 * i)
            body(2 * i + 1)
            return carry
        lax.fori_loop(0, lax.shift_right_logical(n_chunks, 1), step, 0)

        @pl.when(lax.bitwise_and(n_chunks, 1) == 1)
        def _():
            body(n_chunks - 1)

    pass1_begin(0)
    pass1_chunk(maps[0], 0, qi, True)
    loop(qi, lambda ki: pass1_chunk(maps[0], 0, ki, False))
    for n in range(1, len(maps)):
        buf, prev = n % 2, (n - 1) % 2
        pass2_begin(maps[n - 1], prev)
        pass1_begin(buf)
        pass1_chunk(maps[n], buf, qi, True)

        def both(ki, n=n, buf=buf, prev=prev):
            pass2_chunk(maps[n - 1], prev, ki)
            pass1_chunk(maps[n], buf, ki, False)

        loop(qi, both)
        pass2_chunk(maps[n - 1], prev, qi)
        pass2_end(n - 1)
    last = len(maps) - 1
    pass2_begin(maps[last], last % 2)
    loop(qi + 1, lambda ki: pass2_chunk(maps[last], last % 2, ki))
    pass2_end(last)


def _store_masked_queries(q_ref, qm_scr):
    first = lax.broadcasted_iota(jnp.int32, (TQ, LANES), 1) < HEAD_DIM
    for b in range(WIDTH // LANES):
        q = q_ref[:, b * LANES:(b + 1) * LANES]
        zero = jnp.zeros_like(q)
        qm_scr[2 * b] = jnp.where(first, q, zero)
        qm_scr[2 * b + 1] = jnp.where(first, zero, q)


def _fox_kernel(q_ref, k_ref, v_ref, g_ref, crow_ref, ccol_ref, o_ref,
                qm_scr, ta_scr, tb_scr, mxa_scr, mxb_scr, mb_scr, ls_scr, acc_scr, res_scr):
    qi = pl.program_id(1)
    first = lax.broadcasted_iota(jnp.int32, (TQ, LANES), 1) < HEAD_DIM
    _store_masked_queries(q_ref, qm_scr)
    ccol = ccol_ref[...]
    maps = [
        _Map(slot=head, blk=head // 2,
             key_bias=lambda ki, head=head: crow_ref[0, ki, head:head + 1, :],
             row_shift=ccol[:, head:head + 1])
        for head in range(FOX_HEADS)
    ]

    def emit(head, o):
        if head % 2 == 0:
            res_scr[...] = o
        else:
            blk = slice((head // 2) * LANES, (head // 2 + 1) * LANES)
            y = jnp.where(first, res_scr[...], o) * g_ref[:, blk].astype(F32)
            o_ref[:, blk] = y.astype(BF16)

    _attention_maps(maps, qi, qm_scr, k_ref, v_ref, ((ta_scr, mxa_scr), (tb_scr, mxb_scr)),
                    mb_scr, ls_scr, acc_scr, emit)


def _diff_kernel(q_ref, k_ref, v_ref, g_ref, lam_ref, ng_ref, o_ref,
                 qm_scr, ta_scr, tb_scr, mxa_scr, mxb_scr, mb_scr, ls_scr, acc_scr, res_scr,
                 *, lam_init):
    qi = pl.program_id(1)
    _store_masked_queries(q_ref, qm_scr)
    lv = lam_ref[...]
    lam = (jnp.exp(jnp.sum(lv[0:1] * lv[1:2], axis=1, keepdims=True))
           - jnp.exp(jnp.sum(lv[2:3] * lv[3:4], axis=1, keepdims=True)) + lam_init)
    maps = [_Map(slot=n, blk=n // 2, key_bias=None, row_shift=None)
            for n in range(2 * DIFF_HEADS)]

    def emit(n, o):
        if n % 2 == 0:
            res_scr[...] = o
        else:
            blk = slice((n // 2) * LANES, (n // 2 + 1) * LANES)
            d = res_scr[...] - lam * o
            ms = jnp.mean(d * d, axis=-1, keepdims=True)
            y = d * lax.rsqrt(ms + RMS_EPS) * ng_ref[...] * (1.0 - lam_init)
            o_ref[:, blk] = (y * g_ref[:, blk].astype(F32)).astype(BF16)

    _attention_maps(maps, qi, qm_scr, k_ref, v_ref, ((ta_scr, mxa_scr), (tb_scr, mxb_scr)),
                    mb_scr, ls_scr, acc_scr, emit)


def _attn_scratch(seq):
    stat = pltpu.VMEM((TQ, LANES), F32)
    logits = pltpu.VMEM((seq // TK, TQ, TK), F32)
    return [pltpu.VMEM((2 * WIDTH // LANES, TQ, LANES), BF16),
            logits, logits, stat, stat,
            stat, stat, stat, stat]


def _fox_attention(pa, ps, crow, ccol, batch, seq):
    nq = seq // TQ
    return pl.pallas_call(
        _fox_kernel,
        grid=(batch, nq),
        in_specs=[
            pl.BlockSpec((TQ, WIDTH), lambda b, i: (b * nq + i, A_FQ)),
            pl.BlockSpec((seq, WIDTH), lambda b, i: (b, A_FK)),
            pl.BlockSpec((seq, WIDTH), lambda b, i: (b, A_FV)),
            pl.BlockSpec((TQ, WIDTH), lambda b, i: (b * nq + i, S_FG)),
            pl.BlockSpec((1, seq // TK, FOX_HEADS, TK), lambda b, i: (b, 0, 0, 0)),
            pl.BlockSpec((TQ, LANES), lambda b, i: (b * nq + i, 0)),
        ],
        out_specs=pl.BlockSpec((TQ, WIDTH), lambda b, i: (b * nq + i, 0)),
        out_shape=jax.ShapeDtypeStruct((batch * seq, WIDTH), BF16),
        scratch_shapes=_attn_scratch(seq),
        compiler_params=pltpu.CompilerParams(
            dimension_semantics=("arbitrary", "arbitrary"), vmem_limit_bytes=VMEM_LIMIT),
        name="fox_attention",
    )(pa, pa, pa, ps, crow, ccol)


def _diff_attention(pr, pa, ps, lam_vecs, norm_g, lam_init, batch, seq):
    nq = seq // TQ
    return pl.pallas_call(
        functools.partial(_diff_kernel, lam_init=lam_init),
        grid=(batch, nq),
        in_specs=[
            pl.BlockSpec((TQ, WIDTH), lambda b, i: (b * nq + i, R_DQ)),
            pl.BlockSpec((seq, WIDTH), lambda b, i: (b, R_DK)),
            pl.BlockSpec((seq, WIDTH), lambda b, i: (b, A_DV)),
            pl.BlockSpec((TQ, WIDTH), lambda b, i: (b * nq + i, S_DG)),
            pl.BlockSpec((4, HEAD_DIM), lambda b, i: (0, 0)),
            pl.BlockSpec((1, LANES), lambda b, i: (0, 0)),
        ],
        out_specs=pl.BlockSpec((TQ, WIDTH), lambda b, i: (b * nq + i, 0)),
        out_shape=jax.ShapeDtypeStruct((batch * seq, WIDTH), BF16),
        scratch_shapes=_attn_scratch(seq),
        compiler_params=pltpu.CompilerParams(
            dimension_semantics=("arbitrary", "arbitrary"), vmem_limit_bytes=VMEM_LIMIT),
        name="diff_attention",
    )(pr, pr, pa, ps, lam_vecs, norm_g)


def _merge_kernel(x_ref, yf_ref, yd_ref, cb_ref, cc_ref, cx_ref, cg_ref, hc_ref, hx_ref,
                  g0_ref, g1_ref, g2_ref, cw_ref, wf_ref, wd_ref, wc_ref, wo_ref, pg_ref,
                  o_ref, ext_scr, *, tiles_per_seq):
    i = pl.program_id(0)
    halo = SUBLANES
    u = cc_ref[...].astype(F32) * cx_ref[...].astype(F32)
    prev = hc_ref[...].astype(F32) * hx_ref[...].astype(F32)
    prev = jnp.where(i % tiles_per_seq == 0, jnp.zeros_like(prev), prev)
    ext_scr[0:halo, :] = prev
    ext_scr[halo:halo + TM_OUT, :] = u
    cw = cw_ref[...]
    conv = (cw[2:3] * u + cw[1:2] * ext_scr[halo - 1:halo - 1 + TM_OUT, :]
            + cw[0:1] * ext_scr[halo - 2:halo - 2 + TM_OUT, :])
    y_conv = (cb_ref[...].astype(F32) * conv * cg_ref[...].astype(F32)).astype(BF16)

    m = (g0_ref[...].astype(F32) * jnp.dot(yf_ref[...], wf_ref[...], preferred_element_type=F32)
         + g1_ref[...].astype(F32) * jnp.dot(yd_ref[...], wd_ref[...], preferred_element_type=F32)
         + g2_ref[...].astype(F32) * jnp.dot(y_conv, wc_ref[...], preferred_element_type=F32))
    o = jnp.dot(m.astype(BF16), wo_ref[...], preferred_element_type=F32)
    ms = jnp.mean(o * o, axis=-1, keepdims=True)
    o_ref[...] = x_ref[...] + o * lax.rsqrt(ms + RMS_EPS) * pg_ref[...]


def _merge(x2, y_fox, y_diff, pa, ps, pg, conv_w, w_fox, w_diff, w_conv, w_out, post_g, seq):
    t = x2.shape[0]
    tiles_per_seq = seq // TM_OUT
    halo_blocks = TM_OUT // SUBLANES

    def tile(col):
        return pl.BlockSpec((TM_OUT, WIDTH), lambda i: (i, col))

    def halo(col):
        return pl.BlockSpec((SUBLANES, WIDTH),
                            lambda i: (jnp.maximum(i * halo_blocks - 1, 0), col))

    def whole(shape):
        return pl.BlockSpec(shape, lambda i: (0, 0))

    return pl.pallas_call(
        functools.partial(_merge_kernel, tiles_per_seq=tiles_per_seq),
        grid=(t // TM_OUT,),
        in_specs=[
            pl.BlockSpec((TM_OUT, D_MODEL), lambda i: (i, 0)),
            pl.BlockSpec((TM_OUT, WIDTH), lambda i: (i, 0)),
            pl.BlockSpec((TM_OUT, WIDTH), lambda i: (i, 0)),
            tile(A_CB), tile(A_CC), tile(A_CX), tile(S_CG),
            halo(A_CC), halo(A_CX),
            pl.BlockSpec((TM_OUT, D_MODEL), lambda i: (i, 0)),
            pl.BlockSpec((TM_OUT, D_MODEL), lambda i: (i, 1)),
            pl.BlockSpec((TM_OUT, D_MODEL), lambda i: (i, 2)),
            whole((CONV_K, WIDTH)),
            whole((WIDTH, D_MODEL)), whole((WIDTH, D_MODEL)), whole((WIDTH, D_MODEL)),
            whole((D_MODEL, D_MODEL)),
            whole((1, D_MODEL)),
        ],
        out_specs=pl.BlockSpec((TM_OUT, D_MODEL), lambda i: (i, 0)),
        out_shape=jax.ShapeDtypeStruct((t, D_MODEL), F32),
        scratch_shapes=[pltpu.VMEM((TM_OUT + SUBLANES, WIDTH), F32)],
        compiler_params=pltpu.CompilerParams(
            dimension_semantics=("arbitrary",), vmem_limit_bytes=VMEM_LIMIT),
        name="merge",
    )(x2, y_fox, y_diff, pa, pa, pa, ps, pa, pa, pg, pg, pg,
      conv_w, w_fox, w_diff, w_conv, w_out, post_g)


def _rope_lane_tables(positions):
    half = ROT_DIM // 2
    inv_freq = ROPE_THETA ** (-jnp.arange(0, ROT_DIM, 2, dtype=F32) / ROT_DIM)
    ang = positions.astype(F32).reshape(-1, 1) * inv_freq
    cos, sin = jnp.cos(ang), jnp.sin(ang)
    t = ang.shape[0]
    ones = jnp.ones((t, HEAD_DIM - ROT_DIM), F32)
    zeros_rest = jnp.zeros((t, HEAD_DIM - ROT_DIM), F32)
    zeros_half = jnp.zeros((t, half), F32)
    c64 = jnp.concatenate([cos, cos, ones], axis=1)
    s1_64 = jnp.concatenate([-sin, zeros_half, zeros_rest], axis=1)
    s2_64 = jnp.concatenate([zeros_half, sin, zeros_rest], axis=1)
    rep = LANES // HEAD_DIM
    return jnp.tile(c64, (1, rep)), jnp.tile(s1_64, (1, rep)), jnp.tile(s2_64, (1, rep))


def _gather_cols(w, offsets):
    return jnp.concatenate([w[:, o:o + WIDTH] for o in offsets], axis=1).astype(BF16)


def _scale_vec(n_blocks, scaled_block):
    blk = jnp.arange(n_blocks * WIDTH, dtype=jnp.int32) // WIDTH
    return jnp.where(blk == scaled_block, HEAD_DIM ** -0.5, 1.0).astype(F32).reshape(1, -1)


def kernel(x, positions, pre_norm_g, w_in, b_forget, b_merge, conv_w, lam_q1, lam_k1, lam_q2,
           lam_k2, diff_norm_g, w_br_fox, w_br_diff, w_br_conv, w_out, post_norm_g):
    batch, seq, _ = x.shape
    depth = w_in.shape[0]
    assert seq % TQ == 0 and TQ == TK == TS_FF and (batch * seq) % TM_IN == 0
    rc, rs1, rs2 = _rope_lane_tables(positions)
    x2 = x.reshape(batch * seq, D_MODEL)
    for l in range(depth):
        lam_init = 0.8 - 0.6 * math.exp(-0.3 * l)
        w = w_in[l]
        w_plain = _gather_cols(w, [_OFF_FQ, _OFF_FK, _OFF_FV, _OFF_DV, _OFF_CB, _OFF_CC, _OFF_CX])
        w_silu = _gather_cols(w, [_OFF_FG, _OFF_DG, _OFF_CG])
        w_rot = _gather_cols(w, [_OFF_DQ, _OFF_DK])
        w_gate = w[:, _OFF_MG:_OFF_MG + N_BRANCH * D_MODEL].astype(BF16)
        w_ff_t = w[:, _OFF_FF:_OFF_FF + FOX_HEADS].T.astype(BF16)
        pa, h = _plain_proj(x2, pre_norm_g[l].reshape(1, D_MODEL), w_plain,
                            _scale_vec(7, A_FQ), TN_PLAIN)
        ps = _epilogue_proj(_silu_proj_kernel, "proj_silu", h, w_silu, TN_SILU, [], [])
        pr = _epilogue_proj(_rot_proj_kernel, "proj_rotary", h, w_rot, TN_ROT,
                            [_scale_vec(2, R_DQ), rc, rs1, rs2],
                            [_col_vec(TN_ROT), _ROPE_TILE, _ROPE_TILE, _ROPE_TILE])
        pg = _epilogue_proj(_gate_proj_kernel, "proj_gate", h, w_gate, TN_GATE,
                            [b_merge[l].reshape(1, N_BRANCH * D_MODEL)], [_col_vec(TN_GATE)])
        crow, ccol = _forget_cumsum(h, w_ff_t, b_forget[l].reshape(FOX_HEADS, 1), batch, seq)
        y_fox = _fox_attention(pa, ps, crow, ccol, batch, seq)
        lam_vecs = jnp.stack([lam_q1[l], lam_k1[l], lam_q2[l], lam_k2[l]])
        y_diff = _diff_attention(pr, pa, ps, lam_vecs, diff_norm_g[l].reshape(1, LANES),
                                 lam_init, batch, seq)
        x2 = _merge(x2, y_fox, y_diff, pa, ps, pg, conv_w[l],
                    w_br_fox[l].astype(BF16), w_br_diff[l].astype(BF16),
                    w_br_conv[l].astype(BF16), w_out[l].astype(BF16),
                    post_norm_g[l].reshape(1, D_MODEL), seq)
    return x2.reshape(batch, seq, D_MODEL)
```

```python
import functools
import math
from typing import Any, Callable, NamedTuple, Optional

import jax
import jax.numpy as jnp
from jax import lax
from jax.experimental import pallas as pl
from jax.experimental.pallas import tpu as pltpu

F32 = jnp.float32
BF16 = jnp.bfloat16

D_MODEL = 1024
HEAD_DIM = 64
FOX_HEADS = 8
DIFF_HEADS = 4
WIDTH = 512
CONV_K = 3
N_BRANCH = 3
ROPE_THETA = 500000.0
ROT_DIM = HEAD_DIM // 4
RMS_EPS = 1e-6
LANES = 128
SUBLANES = 8

(A_FQ, A_FK, A_FV, A_DV, A_CB, A_CC, A_CX) = range(7)
(S_FG, S_DG, S_CG) = range(3)
(R_DQ, R_DK) = range(2)
N_GATE_BLOCKS = N_BRANCH * D_MODEL // WIDTH

_OFF_FQ, _OFF_FK, _OFF_FV = 0, 512, 1024
_OFF_FF = 1536
_OFF_FG = 1544
_OFF_DQ, _OFF_DK, _OFF_DV, _OFF_DG = 2056, 2568, 3080, 3592
_OFF_CB, _OFF_CC, _OFF_CX, _OFF_CG = 4104, 4616, 5128, 5640
_OFF_MG = 6152

TM_IN = 1024
TN_PLAIN = 1792
TN_SILU = 1536
TN_ROT = 1024
TN_GATE = 1536
TS_FF = 512
TQ = 512
TK = 512
TM_OUT = 512
PAIR_ROWS = 2 * TQ
ATTN_VMEM_LIMIT = 56 * 1024 * 1024
VMEM_LIMIT = 48 * 1024 * 1024


def _silu(v):
    return v * jax.nn.sigmoid(v)


def _rotary_block(acc, c, s1, s2):
    outs = []
    for i in range(acc.shape[1] // LANES):
        v = acc[:, i * LANES:(i + 1) * LANES]
        outs.append(v * c + pltpu.roll(v, LANES - ROT_DIM // 2, 1) * s1
                    + pltpu.roll(v, ROT_DIM // 2, 1) * s2)
    return jnp.concatenate(outs, axis=1)


def _plain_proj_kernel(x_ref, g_ref, w_ref, sc_ref, o_ref, h_ref, h_scr):
    @pl.when(pl.program_id(1) == 0)
    def _():
        x = x_ref[...]
        ms = jnp.mean(x * x, axis=-1, keepdims=True)
        h = (x * lax.rsqrt(ms + RMS_EPS) * g_ref[...]).astype(BF16)
        h_scr[...] = h
        h_ref[...] = h

    acc = jnp.dot(h_scr[...], w_ref[...], preferred_element_type=F32)
    o_ref[...] = (acc * sc_ref[...]).astype(BF16)


def _silu_proj_kernel(h_ref, w_ref, o_ref):
    acc = jnp.dot(h_ref[...], w_ref[...], preferred_element_type=F32)
    o_ref[...] = _silu(acc).astype(BF16)


def _rot_proj_kernel(h_ref, w_ref, sc_ref, rc_ref, rs1_ref, rs2_ref, o_ref):
    acc = jnp.dot(h_ref[...], w_ref[...], preferred_element_type=F32)
    r = _rotary_block(acc, rc_ref[...], rs1_ref[...], rs2_ref[...])
    o_ref[...] = (r * sc_ref[...]).astype(BF16)


def _gate_proj_kernel(h_ref, w_ref, b_ref, o_ref):
    acc = jnp.dot(h_ref[...], w_ref[...], preferred_element_type=F32)
    o_ref[...] = jax.nn.sigmoid(acc + b_ref[...]).astype(BF16)


_PROJ_PARAMS = pltpu.CompilerParams(
    dimension_semantics=("arbitrary", "arbitrary"), vmem_limit_bytes=VMEM_LIMIT)
_ROW_TILE = pl.BlockSpec((TM_IN, D_MODEL), lambda i, j: (i, 0))
_ROPE_TILE = pl.BlockSpec((TM_IN, LANES), lambda i, j: (i, 0))


def _w_block(tn):
    return pl.BlockSpec((D_MODEL, tn), lambda i, j: (0, j))


def _col_vec(tn):
    return pl.BlockSpec((1, tn), lambda i, j: (0, j))


def _out_block(tn):
    return pl.BlockSpec((TM_IN, tn), lambda i, j: (i, j))


def _plain_proj(x2, pre_g, w, col_scale, tn):
    t, n = x2.shape[0], w.shape[1]
    return pl.pallas_call(
        _plain_proj_kernel,
        grid=(t // TM_IN, n // tn),
        in_specs=[_ROW_TILE, pl.BlockSpec((1, D_MODEL), lambda i, j: (0, 0)),
                  _w_block(tn), _col_vec(tn)],
        out_specs=[_out_block(tn), _ROW_TILE],
        out_shape=[jax.ShapeDtypeStruct((t, n), BF16), jax.ShapeDtypeStruct((t, D_MODEL), BF16)],
        scratch_shapes=[pltpu.VMEM((TM_IN, D_MODEL), BF16)],
        compiler_params=_PROJ_PARAMS,
        name="proj_plain",
    )(x2, pre_g, w, col_scale)


def _epilogue_proj(body, name, h, w, tn, extra, extra_specs):
    t, n = h.shape[0], w.shape[1]
    return pl.pallas_call(
        body,
        grid=(t // TM_IN, n // tn),
        in_specs=[_ROW_TILE, _w_block(tn)] + extra_specs,
        out_specs=_out_block(tn),
        out_shape=jax.ShapeDtypeStruct((t, n), BF16),
        compiler_params=_PROJ_PARAMS,
        name=name,
    )(h, w, *extra)


def _split3(v):
    hi = v.astype(BF16)
    r1 = v - hi.astype(F32)
    mid = r1.astype(BF16)
    lo = (r1 - mid.astype(F32)).astype(BF16)
    return hi, mid, lo


def _forget_kernel(h_ref, wt_ref, bf_ref, crow_ref, ccol_ref, carry_scr):
    si = pl.program_id(1)

    @pl.when(si == 0)
    def _():
        carry_scr[...] = jnp.zeros_like(carry_scr)

    ff = lax.dot_general(wt_ref[...], h_ref[...], (((1,), (1,)), ((), ())),
                         preferred_element_type=F32)
    lf = jax.nn.log_sigmoid(ff + bf_ref[...])
    row = lax.broadcasted_iota(jnp.int32, (TS_FF, TS_FF), 0)
    col = lax.broadcasted_iota(jnp.int32, (TS_FF, TS_FF), 1)
    upper = (row <= col).astype(BF16)
    cum = jnp.zeros((FOX_HEADS, TS_FF), F32)
    for piece in _split3(lf):
        cum = cum + jnp.dot(piece, upper, preferred_element_type=F32)
    c = cum + carry_scr[:, 0:1]
    carry_scr[...] = jnp.broadcast_to(c[:, TS_FF - 1:TS_FF], carry_scr.shape)
    crow_ref[0, 0] = c
    padded = jnp.concatenate([c, jnp.zeros((LANES - FOX_HEADS, TS_FF), F32)], axis=0)
    ccol_ref[...] = padded.T


def _forget_cumsum(h, w_ff_t, b_f, batch, seq):
    nt = seq // TS_FF
    return pl.pallas_call(
        _forget_kernel,
        grid=(batch, nt),
        in_specs=[
            pl.BlockSpec((TS_FF, D_MODEL), lambda b, s: (b * nt + s, 0)),
            pl.BlockSpec((FOX_HEADS, D_MODEL), lambda b, s: (0, 0)),
            pl.BlockSpec((FOX_HEADS, 1), lambda b, s: (0, 0)),
        ],
        out_specs=[
            pl.BlockSpec((1, 1, FOX_HEADS, TS_FF), lambda b, s: (b, s, 0, 0)),
            pl.BlockSpec((TS_FF, LANES), lambda b, s: (b * nt + s, 0)),
        ],
        out_shape=[
            jax.ShapeDtypeStruct((batch, nt, FOX_HEADS, TS_FF), F32),
            jax.ShapeDtypeStruct((batch * seq, LANES), F32),
        ],
        scratch_shapes=[pltpu.VMEM((FOX_HEADS, LANES), F32)],
        compiler_params=pltpu.CompilerParams(
            dimension_semantics=("arbitrary", "arbitrary"), vmem_limit_bytes=VMEM_LIMIT),
        name="forget_cumsum",
    )(h, w_ff_t, b_f)


class _MapPair(NamedTuple):
    blk: int
    key_bias: Optional[Callable[[Any], Any]]
    row_shift: Any


def _attention_maps(maps, qi, qm_scr, k_ref, v_ref, bufs, mb_scr, ls_scr, acc_scr, emit):
    row = lax.broadcasted_iota(jnp.int32, (TQ, TK), 0)
    col = lax.broadcasted_iota(jnp.int32, (TQ, TK), 1)
    causal = col <= row
    n_lane_blocks = TK // LANES

    def lanes(v, c):
        return v[:, c * LANES:(c + 1) * LANES]

    def chunk_rows(ref, mp, ki):
        start = pl.multiple_of(ki * TK, TK)
        return ref[pl.ds(start, TK), mp.blk * LANES:(mp.blk + 1) * LANES]

    def pass1_begin(buf):
        bufs[buf][1][...] = jnp.full((PAIR_ROWS, LANES), -jnp.inf, F32)

    def pass1_chunk(mp, buf, ki, diagonal):
        t_scr, mx_scr = bufs[buf]
        s2 = lax.dot_general(qm_scr[mp.blk], chunk_rows(k_ref, mp, ki),
                             (((1,), (1,)), ((), ())), preferred_element_type=F32)
        biases = mp.key_bias(ki) if mp.key_bias is not None else (None, None)
        for half, bias in enumerate(biases):
            rows = slice(half * TQ, (half + 1) * TQ)
            s = s2[rows, :]
            if bias is not None:
                s = s - bias
            if diagonal:
                s = jnp.where(causal, s, -jnp.inf)
            t_scr[ki, rows, :] = s
            m = mx_scr[rows, :]
            for c in range(n_lane_blocks):
                m = jnp.maximum(m, lanes(s, c))
            mx_scr[rows, :] = m

    def pass2_begin(mp, buf):
        m = jnp.max(bufs[buf][1][...], axis=1, keepdims=True)
        if mp.row_shift is not None:
            m = (m + mp.row_shift) - mp.row_shift
        mb_scr[...] = jnp.broadcast_to(m, (PAIR_ROWS, LANES))
        ls_scr[...] = jnp.zeros((PAIR_ROWS, LANES), F32)
        acc_scr[...] = jnp.zeros((PAIR_ROWS, LANES), F32)

    def pass2_chunk(mp, buf, ki):
        t = bufs[buf][0][ki]
        mb = mb_scr[...]
        ps = [jnp.exp(lanes(t, c) - mb) for c in range(n_lane_blocks)]
        ls = ls_scr[...]
        for p in ps:
            ls = ls + p
        ls_scr[...] = ls
        p = jnp.concatenate(ps, axis=1).astype(BF16)
        acc_scr[...] += jnp.dot(p, chunk_rows(v_ref, mp, ki), preferred_element_type=F32)

    def pass2_end(n):
        l = jnp.sum(ls_scr[...], axis=1, keepdims=True)
        emit(n, acc_scr[...] / l)

    def loop(n_chunks, body):
        def step(i, carry):
            body(2 * i)
            body(2 * i + 1)
            return carry
        lax.fori_loop(0, lax.shift_right_logical(n_chunks, 1), step, 0)

        @pl.when(lax.bitwise_and(n_chunks, 1) == 1)
        def _():
            body(n_chunks - 1)

    pass1_begin(0)
    pass1_chunk(maps[0], 0, qi, True)
    loop(qi, lambda ki: pass1_chunk(maps[0], 0, ki, False))
    for n in range(1, len(maps)):
        buf, prev = n % 2, (n - 1) % 2
        pass2_begin(maps[n - 1], prev)
        pass1_begin(buf)
        pass1_chunk(maps[n], buf, qi, True)

        def both(ki, n=n, buf=buf, prev=prev):
            pass2_chunk(maps[n - 1], prev, ki)
            pass1_chunk(maps[n], buf, ki, False)

        loop(qi, both)
        pass2_chunk(maps[n - 1], prev, qi)
        pass2_end(n - 1)
    last = len(maps) - 1
    pass2_begin(maps[last], last % 2)
    loop(qi + 1, lambda ki: pass2_chunk(maps[last], last % 2, ki))
    pass2_end(last)


def _store_masked_queries(q_ref, qm_scr):
    first = lax.broadcasted_iota(jnp.int32, (TQ, LANES), 1) < HEAD_DIM
    for b in range(WIDTH // LANES):
        q = q_ref[:, b * LANES:(b + 1) * LANES]
        zero = jnp.zeros_like(q)
        qm_scr[b, 0:TQ, :] = jnp.where(first, q, zero)
        qm_scr[b, TQ:PAIR_ROWS, :] = jnp.where(first, zero, q)


def _fox_kernel(q_ref, k_ref, v_ref, g_ref, crow_ref, ccol_ref, o_ref,
                qm_scr, ta_scr, tb_scr, mxa_scr, mxb_scr, mb_scr, ls_scr, acc_scr):
    qi = pl.program_id(1)
    first = lax.broadcasted_iota(jnp.int32, (TQ, LANES), 1) < HEAD_DIM
    _store_masked_queries(q_ref, qm_scr)
    ccol = ccol_ref[...]
    maps = [
        _MapPair(blk=b,
                 key_bias=lambda ki, b=b: (crow_ref[0, ki, 2 * b:2 * b + 1, :],
                                           crow_ref[0, ki, 2 * b + 1:2 * b + 2, :]),
                 row_shift=jnp.concatenate([ccol[:, 2 * b:2 * b + 1],
                                            ccol[:, 2 * b + 1:2 * b + 2]], axis=0))
        for b in range(FOX_HEADS // 2)
    ]

    def emit(b, o):
        blk = slice(b * LANES, (b + 1) * LANES)
        y = jnp.where(first, o[0:TQ, :], o[TQ:PAIR_ROWS, :]) * g_ref[:, blk].astype(F32)
        o_ref[:, blk] = y.astype(BF16)

    _attention_maps(maps, qi, qm_scr, k_ref, v_ref, ((ta_scr, mxa_scr), (tb_scr, mxb_scr)),
                    mb_scr, ls_scr, acc_scr, emit)


def _diff_kernel(q_ref, k_ref, v_ref, g_ref, lam_ref, ng_ref, o_ref,
                 qm_scr, ta_scr, tb_scr, mxa_scr, mxb_scr, mb_scr, ls_scr, acc_scr,
                 *, lam_init):
    qi = pl.program_id(1)
    _store_masked_queries(q_ref, qm_scr)
    lv = lam_ref[...]
    lam = (jnp.exp(jnp.sum(lv[0:1] * lv[1:2], axis=1, keepdims=True))
           - jnp.exp(jnp.sum(lv[2:3] * lv[3:4], axis=1, keepdims=True)) + lam_init)
    maps = [_MapPair(blk=h, key_bias=None, row_shift=None) for h in range(DIFF_HEADS)]

    def emit(h, o):
        blk = slice(h * LANES, (h + 1) * LANES)
        d = o[0:TQ, :] - lam * o[TQ:PAIR_ROWS, :]
        ms = jnp.mean(d * d, axis=-1, keepdims=True)
        y = d * lax.rsqrt(ms + RMS_EPS) * ng_ref[...] * (1.0 - lam_init)
        o_ref[:, blk] = (y * g_ref[:, blk].astype(F32)).astype(BF16)

    _attention_maps(maps, qi, qm_scr, k_ref, v_ref, ((ta_scr, mxa_scr), (tb_scr, mxb_scr)),
                    mb_scr, ls_scr, acc_scr, emit)


def _attn_scratch(seq):
    stat = pltpu.VMEM((PAIR_ROWS, LANES), F32)
    logits = pltpu.VMEM((seq // TK, PAIR_ROWS, TK), F32)
    return [pltpu.VMEM((WIDTH // LANES, PAIR_ROWS, LANES), BF16),
            logits, logits, stat, stat,
            stat, stat, stat]


def _per_sequence(block_shape, col):
    return pl.BlockSpec(block_shape, lambda b, i: (b, col), pipeline_mode=pl.Buffered(1))


def _fox_attention(pa, ps, crow, ccol, batch, seq):
    nq = seq // TQ
    return pl.pallas_call(
        _fox_kernel,
        grid=(batch, nq),
        in_specs=[
            pl.BlockSpec((TQ, WIDTH), lambda b, i: (b * nq + i, A_FQ)),
            _per_sequence((seq, WIDTH), A_FK),
            _per_sequence((seq, WIDTH), A_FV),
            pl.BlockSpec((TQ, WIDTH), lambda b, i: (b * nq + i, S_FG)),
            pl.BlockSpec((1, seq // TK, FOX_HEADS, TK), lambda b, i: (b, 0, 0, 0)),
            pl.BlockSpec((TQ, LANES), lambda b, i: (b * nq + i, 0)),
        ],
        out_specs=pl.BlockSpec((TQ, WIDTH), lambda b, i: (b * nq + i, 0)),
        out_shape=jax.ShapeDtypeStruct((batch * seq, WIDTH), BF16),
        scratch_shapes=_attn_scratch(seq),
        compiler_params=pltpu.CompilerParams(
            dimension_semantics=("arbitrary", "arbitrary"), vmem_limit_bytes=ATTN_VMEM_LIMIT),
        name="fox_attention",
    )(pa, pa, pa, ps, crow, ccol)


def _diff_attention(pr, pa, ps, lam_vecs, norm_g, lam_init, batch, seq):
    nq = seq // TQ
    return pl.pallas_call(
        functools.partial(_diff_kernel, lam_init=lam_init),
        grid=(batch, nq),
        in_specs=[
            pl.BlockSpec((TQ, WIDTH), lambda b, i: (b * nq + i, R_DQ)),
            _per_sequence((seq, WIDTH), R_DK),
            _per_sequence((seq, WIDTH), A_DV),
            pl.BlockSpec((TQ, WIDTH), lambda b, i: (b * nq + i, S_DG)),
            pl.BlockSpec((4, HEAD_DIM), lambda b, i: (0, 0)),
            pl.BlockSpec((1, LANES), lambda b, i: (0, 0)),
        ],
        out_specs=pl.BlockSpec((TQ, WIDTH), lambda b, i: (b * nq + i, 0)),
        out_shape=jax.ShapeDtypeStruct((batch * seq, WIDTH), BF16),
        scratch_shapes=_attn_scratch(seq),
        compiler_params=pltpu.CompilerParams(
            dimension_semantics=("arbitrary", "arbitrary"), vmem_limit_bytes=ATTN_VMEM_LIMIT),
        name="diff_attention",
    )(pr, pr, pa, ps, lam_vecs, norm_g)


def _merge_kernel(x_ref, yf_ref, yd_ref, cb_ref, cc_ref, cx_ref, cg_ref, hc_ref, hx_ref,
                  g0_ref, g1_ref, g2_ref, cw_ref, wf_ref, wd_ref, wc_ref, wo_ref, pg_ref,
                  o_ref, ext_scr, *, tiles_per_seq):
    i = pl.program_id(0)
    halo = SUBLANES
    u = cc_ref[...].astype(F32) * cx_ref[...].astype(F32)
    prev = hc_ref[...].astype(F32) * hx_ref[...].astype(F32)
    prev = jnp.where(i % tiles_per_seq == 0, jnp.zeros_like(prev), prev)
    ext_scr[0:halo, :] = prev
    ext_scr[halo:halo + TM_OUT, :] = u
    cw = cw_ref[...]
    conv = (cw[2:3] * u + cw[1:2] * ext_scr[halo - 1:halo - 1 + TM_OUT, :]
            + cw[0:1] * ext_scr[halo - 2:halo - 2 + TM_OUT, :])
    y_conv = (cb_ref[...].astype(F32) * conv * cg_ref[...].astype(F32)).astype(BF16)

    m = (g0_ref[...].astype(F32) * jnp.dot(yf_ref[...], wf_ref[...], preferred_element_type=F32)
         + g1_ref[...].astype(F32) * jnp.dot(yd_ref[...], wd_ref[...], preferred_element_type=F32)
         + g2_ref[...].astype(F32) * jnp.dot(y_conv, wc_ref[...], preferred_element_type=F32))
    o = jnp.dot(m.astype(BF16), wo_ref[...], preferred_element_type=F32)
    ms = jnp.mean(o * o, axis=-1, keepdims=True)
    o_ref[...] = x_ref[...] + o * lax.rsqrt(ms + RMS_EPS) * pg_ref[...]


def _merge(x2, y_fox, y_diff, pa, ps, pg, conv_w, w_fox, w_diff, w_conv, w_out, post_g, seq):
    t = x2.shape[0]
    tiles_per_seq = seq // TM_OUT
    halo_blocks = TM_OUT // SUBLANES

    def tile(col):
        return pl.BlockSpec((TM_OUT, WIDTH), lambda i: (i, col))

    def halo(col):
        return pl.BlockSpec((SUBLANES, WIDTH),
                            lambda i: (jnp.maximum(i * halo_blocks - 1, 0), col))

    def whole(shape):
        return pl.BlockSpec(shape, lambda i: (0, 0))

    return pl.pallas_call(
        functools.partial(_merge_kernel, tiles_per_seq=tiles_per_seq),
        grid=(t // TM_OUT,),
        in_specs=[
            pl.BlockSpec((TM_OUT, D_MODEL), lambda i: (i, 0)),
            pl.BlockSpec((TM_OUT, WIDTH), lambda i: (i, 0)),
            pl.BlockSpec((TM_OUT, WIDTH), lambda i: (i, 0)),
            tile(A_CB), tile(A_CC), tile(A_CX), tile(S_CG),
            halo(A_CC), halo(A_CX),
            pl.BlockSpec((TM_OUT, D_MODEL), lambda i: (i, 0)),
            pl.BlockSpec((TM_OUT, D_MODEL), lambda i: (i, 1)),
            pl.BlockSpec((TM_OUT, D_MODEL), lambda i: (i, 2)),
            whole((CONV_K, WIDTH)),
            whole((WIDTH, D_MODEL)), whole((WIDTH, D_MODEL)), whole((WIDTH, D_MODEL)),
            whole((D_MODEL, D_MODEL)),
            whole((1, D_MODEL)),
        ],
        out_specs=pl.BlockSpec((TM_OUT, D_MODEL), lambda i: (i, 0)),
        out_shape=jax.ShapeDtypeStruct((t, D_MODEL), F32),
        scratch_shapes=[pltpu.VMEM((TM_OUT + SUBLANES, WIDTH), F32)],
        compiler_params=pltpu.CompilerParams(
            dimension_semantics=("arbitrary",), vmem_limit_bytes=VMEM_LIMIT),
        name="merge",
    )(x2, y_fox, y_diff, pa, pa, pa, ps, pa, pa, pg, pg, pg,
      conv_w, w_fox, w_diff, w_conv, w_out, post_g)


def _rope_lane_tables(positions):
    half = ROT_DIM // 2
    inv_freq = ROPE_THETA ** (-jnp.arange(0, ROT_DIM, 2, dtype=F32) / ROT_DIM)
    ang = positions.astype(F32).reshape(-1, 1) * inv_freq
    cos, sin = jnp.cos(ang), jnp.sin(ang)
    t = ang.shape[0]
    ones = jnp.ones((t, HEAD_DIM - ROT_DIM), F32)
    zeros_rest = jnp.zeros((t, HEAD_DIM - ROT_DIM), F32)
    zeros_half = jnp.zeros((t, half), F32)
    c64 = jnp.concatenate([cos, cos, ones], axis=1)
    s1_64 = jnp.concatenate([-sin, zeros_half, zeros_rest], axis=1)
    s2_64 = jnp.concatenate([zeros_half, sin, zeros_rest], axis=1)
    rep = LANES // HEAD_DIM
    return jnp.tile(c64, (1, rep)), jnp.tile(s1_64, (1, rep)), jnp.tile(s2_64, (1, rep))


def _gather_cols(w, offsets):
    return jnp.concatenate([w[:, o:o + WIDTH] for o in offsets], axis=1).astype(BF16)


def _scale_vec(n_blocks, scaled_block):
    blk = jnp.arange(n_blocks * WIDTH, dtype=jnp.int32) // WIDTH
    return jnp.where(blk == scaled_block, HEAD_DIM ** -0.5, 1.0).astype(F32).reshape(1, -1)


def kernel(x, positions, pre_norm_g, w_in, b_forget, b_merge, conv_w, lam_q1, lam_k1, lam_q2,
           lam_k2, diff_norm_g, w_br_fox, w_br_diff, w_br_conv, w_out, post_norm_g):
    batch, seq, _ = x.shape
    depth = w_in.shape[0]
    assert seq % TQ == 0 and TQ == TK == TS_FF and (batch * seq) % TM_IN == 0
    rc, rs1, rs2 = _rope_lane_tables(positions)
    x2 = x.reshape(batch * seq, D_MODEL)
    for l in range(depth):
        lam_init = 0.8 - 0.6 * math.exp(-0.3 * l)
        w = w_in[l]
        w_plain = _gather_cols(w, [_OFF_FQ, _OFF_FK, _OFF_FV, _OFF_DV, _OFF_CB, _OFF_CC, _OFF_CX])
        w_silu = _gather_cols(w, [_OFF_FG, _OFF_DG, _OFF_CG])
        w_rot = _gather_cols(w, [_OFF_DQ, _OFF_DK])
        w_gate = w[:, _OFF_MG:_OFF_MG + N_BRANCH * D_MODEL].astype(BF16)
        w_ff_t = w[:, _OFF_FF:_OFF_FF + FOX_HEADS].T.astype(BF16)
        pa, h = _plain_proj(x2, pre_norm_g[l].reshape(1, D_MODEL), w_plain,
                            _scale_vec(7, A_FQ), TN_PLAIN)
        ps = _epilogue_proj(_silu_proj_kernel, "proj_silu", h, w_silu, TN_SILU, [], [])
        pr = _epilogue_proj(_rot_proj_kernel, "proj_rotary", h, w_rot, TN_ROT,
                            [_scale_vec(2, R_DQ), rc, rs1, rs2],
                            [_col_vec(TN_ROT), _ROPE_TILE, _ROPE_TILE, _ROPE_TILE])
        pg = _epilogue_proj(_gate_proj_kernel, "proj_gate", h, w_gate, TN_GATE,
                            [b_merge[l].reshape(1, N_BRANCH * D_MODEL)], [_col_vec(TN_GATE)])
        crow, ccol = _forget_cumsum(h, w_ff_t, b_forget[l].reshape(FOX_HEADS, 1), batch, seq)
        y_fox = _fox_attention(pa, ps, crow, ccol, batch, seq)
        lam_vecs = jnp.stack([lam_q1[l], lam_k1[l], lam_q2[l], lam_k2[l]])
        y_diff = _diff_attention(pr, pa, ps, lam_vecs, diff_norm_g[l].reshape(1, LANES),
                                 lam_init, batch, seq)
        x2 = _merge(x2, y_fox, y_diff, pa, ps, pg, conv_w[l],
                    w_br_fox[l].astype(BF16), w_br_diff[l].astype(BF16),
                    w_br_conv[l].astype(BF16), w_out[l].astype(BF16),
                    post_norm_g[l].reshape(1, D_MODEL), seq)
    return x2.reshape(batch, seq, D_MODEL)
```

```python
import functools
import math
from typing import Any, Callable, NamedTuple, Optional

import jax
import jax.numpy as jnp
from jax import lax
from jax.experimental import pallas as pl
from jax.experimental.pallas import tpu as pltpu

F32 = jnp.float32
BF16 = jnp.bfloat16

D_MODEL = 1024
HEAD_DIM = 64
FOX_HEADS = 8
DIFF_HEADS = 4
WIDTH = 512
CONV_K = 3
N_BRANCH = 3
ROPE_THETA = 500000.0
ROT_DIM = HEAD_DIM // 4
RMS_EPS = 1e-6
LANES = 128
SUBLANES = 8
LOG2E = math.log2(math.e)

(A_FQ, A_FK, A_FV, A_DV, A_CB, A_CC, A_CX) = range(7)
(S_FG, S_DG, S_CG) = range(3)
(R_DQ, R_DK) = range(2)
N_GATE_BLOCKS = N_BRANCH * D_MODEL // WIDTH

_OFF_FQ, _OFF_FK, _OFF_FV = 0, 512, 1024
_OFF_FF = 1536
_OFF_FG = 1544
_OFF_DQ, _OFF_DK, _OFF_DV, _OFF_DG = 2056, 2568, 3080, 3592
_OFF_CB, _OFF_CC, _OFF_CX, _OFF_CG = 4104, 4616, 5128, 5640
_OFF_MG = 6152

TM_IN = 1024
TN_PLAIN = 1792
TN_SILU = 1536
TN_ROT = 1024
TN_GATE = 3072
TS_FF = 512
TQ = 512
TK = 512
TM_OUT = 512
PAIR_ROWS = 2 * TQ
ATTN_VMEM_LIMIT = 56 * 1024 * 1024
VMEM_LIMIT = 48 * 1024 * 1024


def _silu(v):
    return v * jax.nn.sigmoid(v)


def _rotary_block(acc, c, s1, s2):
    outs = []
    for i in range(acc.shape[1] // LANES):
        v = acc[:, i * LANES:(i + 1) * LANES]
        outs.append(v * c + pltpu.roll(v, LANES - ROT_DIM // 2, 1) * s1
                    + pltpu.roll(v, ROT_DIM // 2, 1) * s2)
    return jnp.concatenate(outs, axis=1)


def _plain_proj_kernel(x_ref, g_ref, w_ref, sc_ref, o_ref, h_ref, h_scr):
    @pl.when(pl.program_id(1) == 0)
    def _():
        x = x_ref[...]
        ms = jnp.mean(x * x, axis=-1, keepdims=True)
        h = (x * lax.rsqrt(ms + RMS_EPS) * g_ref[...]).astype(BF16)
        h_scr[...] = h
        h_ref[...] = h

    acc = jnp.dot(h_scr[...], w_ref[...], preferred_element_type=F32)
    o_ref[...] = (acc * sc_ref[...]).astype(BF16)


def _silu_proj_kernel(h_ref, w_ref, o_ref):
    acc = jnp.dot(h_ref[...], w_ref[...], preferred_element_type=F32)
    o_ref[...] = _silu(acc).astype(BF16)


def _rot_proj_kernel(h_ref, w_ref, sc_ref, rc_ref, rs1_ref, rs2_ref, o_ref):
    acc = jnp.dot(h_ref[...], w_ref[...], preferred_element_type=F32)
    r = _rotary_block(acc, rc_ref[...], rs1_ref[...], rs2_ref[...])
    o_ref[...] = (r * sc_ref[...]).astype(BF16)


def _gate_proj_kernel(h_ref, w_ref, b_ref, o_ref):
    acc = jnp.dot(h_ref[...], w_ref[...], preferred_element_type=F32)
    o_ref[...] = jax.nn.sigmoid(acc + b_ref[...]).astype(BF16)


_PROJ_PARAMS = pltpu.CompilerParams(
    dimension_semantics=("arbitrary", "arbitrary"), vmem_limit_bytes=VMEM_LIMIT)
_ROW_TILE = pl.BlockSpec((TM_IN, D_MODEL), lambda i, j: (i, 0))
_ROPE_TILE = pl.BlockSpec((TM_IN, LANES), lambda i, j: (i, 0))


def _w_block(tn):
    return pl.BlockSpec((D_MODEL, tn), lambda i, j: (0, j))


def _col_vec(tn):
    return pl.BlockSpec((1, tn), lambda i, j: (0, j))


def _out_block(tn):
    return pl.BlockSpec((TM_IN, tn), lambda i, j: (i, j))


def _plain_proj(x2, pre_g, w, col_scale, tn):
    t, n = x2.shape[0], w.shape[1]
    return pl.pallas_call(
        _plain_proj_kernel,
        grid=(t // TM_IN, n // tn),
        in_specs=[_ROW_TILE, pl.BlockSpec((1, D_MODEL), lambda i, j: (0, 0)),
                  _w_block(tn), _col_vec(tn)],
        out_specs=[_out_block(tn), _ROW_TILE],
        out_shape=[jax.ShapeDtypeStruct((t, n), BF16), jax.ShapeDtypeStruct((t, D_MODEL), BF16)],
        scratch_shapes=[pltpu.VMEM((TM_IN, D_MODEL), BF16)],
        compiler_params=_PROJ_PARAMS,
        name="proj_plain",
    )(x2, pre_g, w, col_scale)


def _epilogue_proj(body, name, h, w, tn, extra, extra_specs):
    t, n = h.shape[0], w.shape[1]
    return pl.pallas_call(
        body,
        grid=(t // TM_IN, n // tn),
        in_specs=[_ROW_TILE, _w_block(tn)] + extra_specs,
        out_specs=_out_block(tn),
        out_shape=jax.ShapeDtypeStruct((t, n), BF16),
        compiler_params=_PROJ_PARAMS,
        name=name,
    )(h, w, *extra)


def _split3(v):
    hi = v.astype(BF16)
    r1 = v - hi.astype(F32)
    mid = r1.astype(BF16)
    lo = (r1 - mid.astype(F32)).astype(BF16)
    return hi, mid, lo


def _forget_kernel(h_ref, wt_ref, bf_ref, crow_ref, ccol_ref, carry_scr):
    si = pl.program_id(1)

    @pl.when(si == 0)
    def _():
        carry_scr[...] = jnp.zeros_like(carry_scr)

    ff = lax.dot_general(wt_ref[...], h_ref[...], (((1,), (1,)), ((), ())),
                         preferred_element_type=F32)
    lf = jax.nn.log_sigmoid(ff + bf_ref[...])
    row = lax.broadcasted_iota(jnp.int32, (TS_FF, TS_FF), 0)
    col = lax.broadcasted_iota(jnp.int32, (TS_FF, TS_FF), 1)
    upper = (row <= col).astype(BF16)
    cum = jnp.zeros((FOX_HEADS, TS_FF), F32)
    for piece in _split3(lf):
        cum = cum + jnp.dot(piece, upper, preferred_element_type=F32)
    c = cum + carry_scr[:, 0:1]
    carry_scr[...] = jnp.broadcast_to(c[:, TS_FF - 1:TS_FF], carry_scr.shape)
    c2 = c * LOG2E
    crow_ref[0, 0] = c2
    padded = jnp.concatenate([c2, jnp.zeros((LANES - FOX_HEADS, TS_FF), F32)], axis=0)
    ccol_ref[...] = padded.T


def _forget_cumsum(h, w_ff_t, b_f, batch, seq):
    nt = seq // TS_FF
    return pl.pallas_call(
        _forget_kernel,
        grid=(batch, nt),
        in_specs=[
            pl.BlockSpec((TS_FF, D_MODEL), lambda b, s: (b * nt + s, 0)),
            pl.BlockSpec((FOX_HEADS, D_MODEL), lambda b, s: (0, 0)),
            pl.BlockSpec((FOX_HEADS, 1), lambda b, s: (0, 0)),
        ],
        out_specs=[
            pl.BlockSpec((1, 1, FOX_HEADS, TS_FF), lambda b, s: (b, s, 0, 0)),
            pl.BlockSpec((TS_FF, LANES), lambda b, s: (b * nt + s, 0)),
        ],
        out_shape=[
            jax.ShapeDtypeStruct((batch, nt, FOX_HEADS, TS_FF), F32),
            jax.ShapeDtypeStruct((batch * seq, LANES), F32),
        ],
        scratch_shapes=[pltpu.VMEM((FOX_HEADS, LANES), F32)],
        compiler_params=pltpu.CompilerParams(
            dimension_semantics=("arbitrary", "arbitrary"), vmem_limit_bytes=VMEM_LIMIT),
        name="forget_cumsum",
    )(h, w_ff_t, b_f)


class _MapPair(NamedTuple):
    blk: int
    key_bias: Optional[Callable[[Any], Any]]
    row_shift: Any


def _attention_maps(maps, qi, qm_scr, k_ref, v_ref, bufs, mb_scr, acc_scr, ls_scr, emit):
    mxu_row_sums = ls_scr is None
    row = lax.broadcasted_iota(jnp.int32, (TQ, TK), 0)
    col = lax.broadcasted_iota(jnp.int32, (TQ, TK), 1)
    causal = col <= row
    n_lane_blocks = TK // LANES

    def lanes(v, c):
        return v[:, c * LANES:(c + 1) * LANES]

    def chunk_rows(ref, mp, ki):
        start = pl.multiple_of(ki * TK, TK)
        return ref[pl.ds(start, TK), mp.blk * LANES:(mp.blk + 1) * LANES]

    def pass1_begin(buf):
        bufs[buf][1][...] = jnp.full((PAIR_ROWS, LANES), -jnp.inf, F32)

    def pass1_chunk(mp, buf, ki, diagonal):
        t_scr, mx_scr = bufs[buf]
        s2 = lax.dot_general(qm_scr[mp.blk], chunk_rows(k_ref, mp, ki),
                             (((1,), (1,)), ((), ())), preferred_element_type=F32)
        biases = mp.key_bias(ki) if mp.key_bias is not None else (None, None)
        for half, bias in enumerate(biases):
            rows = slice(half * TQ, (half + 1) * TQ)
            s = s2[rows, :]
            if bias is not None:
                s = s - bias
            if diagonal:
                s = jnp.where(causal, s, -jnp.inf)
            t_scr[ki, rows, :] = s
            m = mx_scr[rows, :]
            for c in range(n_lane_blocks):
                m = jnp.maximum(m, lanes(s, c))
            mx_scr[rows, :] = m

    def pass2_begin(mp, buf):
        m = jnp.max(bufs[buf][1][...], axis=1, keepdims=True)
        if mp.row_shift is not None:
            m = (m + mp.row_shift) - mp.row_shift
        mb_scr[...] = jnp.broadcast_to(m, (PAIR_ROWS, LANES))
        acc_scr[...] = jnp.zeros(acc_scr.shape, F32)
        if not mxu_row_sums:
            ls_scr[...] = jnp.zeros((PAIR_ROWS, LANES), F32)

    def pass2_chunk(mp, buf, ki):
        t = bufs[buf][0][ki]
        mb = mb_scr[...]
        ps = [jnp.exp2(lanes(t, c) - mb) for c in range(n_lane_blocks)]
        v = chunk_rows(v_ref, mp, ki)
        if mxu_row_sums:
            v = jnp.concatenate([v, jnp.ones((TK, LANES), BF16)], axis=1)
        else:
            ls = ls_scr[...]
            for p in ps:
                ls = ls + p
            ls_scr[...] = ls
        p = jnp.concatenate(ps, axis=1).astype(BF16)
        acc_scr[...] += jnp.dot(p, v, preferred_element_type=F32)

    def pass2_end(n):
        acc = acc_scr[...]
        if mxu_row_sums:
            l = acc[:, LANES:LANES + 1]
        else:
            l = jnp.sum(ls_scr[...], axis=1, keepdims=True)
        emit(n, acc[:, 0:LANES] / l)

    def loop(n_chunks, body):
        def step(i, carry):
            body(2 * i)
            body(2 * i + 1)
            return carry
        lax.fori_loop(0, lax.shift_right_logical(n_chunks, 1), step, 0)

        @pl.when(lax.bitwise_and(n_chunks, 1) == 1)
        def _():
            body(n_chunks - 1)

    pass1_begin(0)
    pass1_chunk(maps[0], 0, qi, True)
    loop(qi, lambda ki: pass1_chunk(maps[0], 0, ki, False))
    for n in range(1, len(maps)):
        buf, prev = n % 2, (n - 1) % 2
        pass2_begin(maps[n - 1], prev)
        pass1_begin(buf)
        pass1_chunk(maps[n], buf, qi, True)

        def both(ki, n=n, buf=buf, prev=prev):
            pass2_chunk(maps[n - 1], prev, ki)
            pass1_chunk(maps[n], buf, ki, False)

        loop(qi, both)
        pass2_chunk(maps[n - 1], prev, qi)
        pass2_end(n - 1)
    last = len(maps) - 1
    pass2_begin(maps[last], last % 2)
    loop(qi + 1, lambda ki: pass2_chunk(maps[last], last % 2, ki))
    pass2_end(last)


def _store_masked_queries(q_ref, qm_scr):
    first = lax.broadcasted_iota(jnp.int32, (TQ, LANES), 1) < HEAD_DIM
    for b in range(WIDTH // LANES):
        q = q_ref[:, b * LANES:(b + 1) * LANES]
        zero = jnp.zeros_like(q)
        qm_scr[b, 0:TQ, :] = jnp.where(first, q, zero)
        qm_scr[b, TQ:PAIR_ROWS, :] = jnp.where(first, zero, q)


def _fox_kernel(q_ref, k_ref, v_ref, g_ref, crow_ref, ccol_ref, o_ref,
                qm_scr, ta_scr, tb_scr, mxa_scr, mxb_scr, mb_scr, acc_scr):
    qi = pl.program_id(1)
    first = lax.broadcasted_iota(jnp.int32, (TQ, LANES), 1) < HEAD_DIM
    _store_masked_queries(q_ref, qm_scr)
    ccol = ccol_ref[...]
    maps = [
        _MapPair(blk=b,
                 key_bias=lambda ki, b=b: (crow_ref[0, ki, 2 * b:2 * b + 1, :],
                                           crow_ref[0, ki, 2 * b + 1:2 * b + 2, :]),
                 row_shift=jnp.concatenate([ccol[:, 2 * b:2 * b + 1],
                                            ccol[:, 2 * b + 1:2 * b + 2]], axis=0))
        for b in range(FOX_HEADS // 2)
    ]

    def emit(b, o):
        blk = slice(b * LANES, (b + 1) * LANES)
        y = jnp.where(first, o[0:TQ, :], o[TQ:PAIR_ROWS, :]) * g_ref[:, blk].astype(F32)
        o_ref[:, blk] = y.astype(BF16)

    _attention_maps(maps, qi, qm_scr, k_ref, v_ref, ((ta_scr, mxa_scr), (tb_scr, mxb_scr)),
                    mb_scr, acc_scr, None, emit)


def _diff_kernel(q_ref, k_ref, v_ref, g_ref, lam_ref, ng_ref, o_ref,
                 qm_scr, ta_scr, tb_scr, mxa_scr, mxb_scr, mb_scr, acc_scr, ls_scr,
                 *, lam_init):
    qi = pl.program_id(1)
    _store_masked_queries(q_ref, qm_scr)
    lv = lam_ref[...]
    lam = (jnp.exp(jnp.sum(lv[0:1] * lv[1:2], axis=1, keepdims=True))
           - jnp.exp(jnp.sum(lv[2:3] * lv[3:4], axis=1, keepdims=True)) + lam_init)
    maps = [_MapPair(blk=h, key_bias=None, row_shift=None) for h in range(DIFF_HEADS)]

    def emit(h, o):
        blk = slice(h * LANES, (h + 1) * LANES)
        d = o[0:TQ, :] - lam * o[TQ:PAIR_ROWS, :]
        ms = jnp.mean(d * d, axis=-1, keepdims=True)
        y = d * lax.rsqrt(ms + RMS_EPS) * ng_ref[...] * (1.0 - lam_init)
        o_ref[:, blk] = (y * g_ref[:, blk].astype(F32)).astype(BF16)

    _attention_maps(maps, qi, qm_scr, k_ref, v_ref, ((ta_scr, mxa_scr), (tb_scr, mxb_scr)),
                    mb_scr, acc_scr, ls_scr, emit)


def _attn_scratch(seq, mxu_row_sums):
    stat = pltpu.VMEM((PAIR_ROWS, LANES), F32)
    logits = pltpu.VMEM((seq // TK, PAIR_ROWS, TK), F32)
    common = [pltpu.VMEM((WIDTH // LANES, PAIR_ROWS, LANES), BF16),
              logits, logits, stat, stat,
              stat]
    if mxu_row_sums:
        return common + [pltpu.VMEM((PAIR_ROWS, 2 * LANES), F32)]
    return common + [stat, stat]


def _per_sequence(block_shape, col):
    return pl.BlockSpec(block_shape, lambda b, i: (b, col), pipeline_mode=pl.Buffered(1))


def _fox_attention(pa, ps, crow, ccol, batch, seq):
    nq = seq // TQ
    return pl.pallas_call(
        _fox_kernel,
        grid=(batch, nq),
        in_specs=[
            pl.BlockSpec((TQ, WIDTH), lambda b, i: (b * nq + i, A_FQ)),
            _per_sequence((seq, WIDTH), A_FK),
            _per_sequence((seq, WIDTH), A_FV),
            pl.BlockSpec((TQ, WIDTH), lambda b, i: (b * nq + i, S_FG)),
            pl.BlockSpec((1, seq // TK, FOX_HEADS, TK), lambda b, i: (b, 0, 0, 0)),
            pl.BlockSpec((TQ, LANES), lambda b, i: (b * nq + i, 0)),
        ],
        out_specs=pl.BlockSpec((TQ, WIDTH), lambda b, i: (b * nq + i, 0)),
        out_shape=jax.ShapeDtypeStruct((batch * seq, WIDTH), BF16),
        scratch_shapes=_attn_scratch(seq, mxu_row_sums=True),
        compiler_params=pltpu.CompilerParams(
            dimension_semantics=("arbitrary", "arbitrary"), vmem_limit_bytes=ATTN_VMEM_LIMIT),
        name="fox_attention",
    )(pa, pa, pa, ps, crow, ccol)


def _diff_attention(pr, pa, ps, lam_vecs, norm_g, lam_init, batch, seq):
    nq = seq // TQ
    return pl.pallas_call(
        functools.partial(_diff_kernel, lam_init=lam_init),
        grid=(batch, nq),
        in_specs=[
            pl.BlockSpec((TQ, WIDTH), lambda b, i: (b * nq + i, R_DQ)),
            _per_sequence((seq, WIDTH), R_DK),
            _per_sequence((seq, WIDTH), A_DV),
            pl.BlockSpec((TQ, WIDTH), lambda b, i: (b * nq + i, S_DG)),
            pl.BlockSpec((4, HEAD_DIM), lambda b, i: (0, 0)),
            pl.BlockSpec((1, LANES), lambda b, i: (0, 0)),
        ],
        out_specs=pl.BlockSpec((TQ, WIDTH), lambda b, i: (b * nq + i, 0)),
        out_shape=jax.ShapeDtypeStruct((batch * seq, WIDTH), BF16),
        scratch_shapes=_attn_scratch(seq, mxu_row_sums=False),
        compiler_params=pltpu.CompilerParams(
            dimension_semantics=("arbitrary", "arbitrary"), vmem_limit_bytes=ATTN_VMEM_LIMIT),
        name="diff_attention",
    )(pr, pr, pa, ps, lam_vecs, norm_g)


def _merge_kernel(x_ref, yf_ref, yd_ref, cb_ref, cc_ref, cx_ref, cg_ref, hc_ref, hx_ref,
                  g0_ref, g1_ref, g2_ref, cw_ref, wf_ref, wd_ref, wc_ref, wo_ref, pg_ref,
                  o_ref, ext_scr, *, tiles_per_seq):
    i = pl.program_id(0)
    halo = SUBLANES
    u = cc_ref[...].astype(F32) * cx_ref[...].astype(F32)
    prev = hc_ref[...].astype(F32) * hx_ref[...].astype(F32)
    prev = jnp.where(i % tiles_per_seq == 0, jnp.zeros_like(prev), prev)
    ext_scr[0:halo, :] = prev
    ext_scr[halo:halo + TM_OUT, :] = u
    cw = cw_ref[...]
    conv = (cw[2:3] * u + cw[1:2] * ext_scr[halo - 1:halo - 1 + TM_OUT, :]
            + cw[0:1] * ext_scr[halo - 2:halo - 2 + TM_OUT, :])
    y_conv = (cb_ref[...].astype(F32) * conv * cg_ref[...].astype(F32)).astype(BF16)

    m = (g0_ref[...].astype(F32) * jnp.dot(yf_ref[...], wf_ref[...], preferred_element_type=F32)
         + g1_ref[...].astype(F32) * jnp.dot(yd_ref[...], wd_ref[...], preferred_element_type=F32)
         + g2_ref[...].astype(F32) * jnp.dot(y_conv, wc_ref[...], preferred_element_type=F32))
    o = jnp.dot(m.astype(BF16), wo_ref[...], preferred_element_type=F32)
    ms = jnp.mean(o * o, axis=-1, keepdims=True)
    o_ref[...] = x_ref[...] + o * lax.rsqrt(ms + RMS_EPS) * pg_ref[...]


def _merge(x2, y_fox, y_diff, pa, ps, pg, conv_w, w_fox, w_diff, w_conv, w_out, post_g, seq):
    t = x2.shape[0]
    tiles_per_seq = seq // TM_OUT
    halo_blocks = TM_OUT // SUBLANES

    def tile(col):
        return pl.BlockSpec((TM_OUT, WIDTH), lambda i: (i, col))

    def halo(col):
        return pl.BlockSpec((SUBLANES, WIDTH),
                            lambda i: (jnp.maximum(i * halo_blocks - 1, 0), col))

    def whole(shape):
        return pl.BlockSpec(shape, lambda i: (0, 0))

    return pl.pallas_call(
        functools.partial(_merge_kernel, tiles_per_seq=tiles_per_seq),
        grid=(t // TM_OUT,),
        in_specs=[
            pl.BlockSpec((TM_OUT, D_MODEL), lambda i: (i, 0)),
            pl.BlockSpec((TM_OUT, WIDTH), lambda i: (i, 0)),
            pl.BlockSpec((TM_OUT, WIDTH), lambda i: (i, 0)),
            tile(A_CB), tile(A_CC), tile(A_CX), tile(S_CG),
            halo(A_CC), halo(A_CX),
            pl.BlockSpec((TM_OUT, D_MODEL), lambda i: (i, 0)),
            pl.BlockSpec((TM_OUT, D_MODEL), lambda i: (i, 1)),
            pl.BlockSpec((TM_OUT, D_MODEL), lambda i: (i, 2)),
            whole((CONV_K, WIDTH)),
            whole((WIDTH, D_MODEL)), whole((WIDTH, D_MODEL)), whole((WIDTH, D_MODEL)),
            whole((D_MODEL, D_MODEL)),
            whole((1, D_MODEL)),
        ],
        out_specs=pl.BlockSpec((TM_OUT, D_MODEL), lambda i: (i, 0)),
        out_shape=jax.ShapeDtypeStruct((t, D_MODEL), F32),
        scratch_shapes=[pltpu.VMEM((TM_OUT + SUBLANES, WIDTH), F32)],
        compiler_params=pltpu.CompilerParams(
            dimension_semantics=("arbitrary",), vmem_limit_bytes=VMEM_LIMIT),
        name="merge",
    )(x2, y_fox, y_diff, pa, pa, pa, ps, pa, pa, pg, pg, pg,
      conv_w, w_fox, w_diff, w_conv, w_out, post_g)


def _rope_lane_tables(positions):
    half = ROT_DIM // 2
    inv_freq = ROPE_THETA ** (-jnp.arange(0, ROT_DIM, 2, dtype=F32) / ROT_DIM)
    ang = positions.astype(F32).reshape(-1, 1) * inv_freq
    cos, sin = jnp.cos(ang), jnp.sin(ang)
    t = ang.shape[0]
    ones = jnp.ones((t, HEAD_DIM - ROT_DIM), F32)
    zeros_rest = jnp.zeros((t, HEAD_DIM - ROT_DIM), F32)
    zeros_half = jnp.zeros((t, half), F32)
    c64 = jnp.concatenate([cos, cos, ones], axis=1)
    s1_64 = jnp.concatenate([-sin, zeros_half, zeros_rest], axis=1)
    s2_64 = jnp.concatenate([zeros_half, sin, zeros_rest], axis=1)
    rep = LANES // HEAD_DIM
    return jnp.tile(c64, (1, rep)), jnp.tile(s1_64, (1, rep)), jnp.tile(s2_64, (1, rep))


def _gather_cols(w, offsets):
    return jnp.concatenate([w[:, o:o + WIDTH] for o in offsets], axis=1).astype(BF16)


def _scale_vec(n_blocks, scaled_block):
    blk = jnp.arange(n_blocks * WIDTH, dtype=jnp.int32) // WIDTH
    q_scale = HEAD_DIM ** -0.5 * LOG2E
    return jnp.where(blk == scaled_block, q_scale, 1.0).astype(F32).reshape(1, -1)


def kernel(x, positions, pre_norm_g, w_in, b_forget, b_merge, conv_w, lam_q1, lam_k1, lam_q2,
           lam_k2, diff_norm_g, w_br_fox, w_br_diff, w_br_conv, w_out, post_norm_g):
    batch, seq, _ = x.shape
    depth = w_in.shape[0]
    assert seq % TQ == 0 and TQ == TK == TS_FF and (batch * seq) % TM_IN == 0
    rc, rs1, rs2 = _rope_lane_tables(positions)
    x2 = x.reshape(batch * seq, D_MODEL)
    for l in range(depth):
        lam_init = 0.8 - 0.6 * math.exp(-0.3 * l)
        w = w_in[l]
        w_plain = _gather_cols(w, [_OFF_FQ, _OFF_FK, _OFF_FV, _OFF_DV, _OFF_CB, _OFF_CC, _OFF_CX])
        w_silu = _gather_cols(w, [_OFF_FG, _OFF_DG, _OFF_CG])
        w_rot = _gather_cols(w, [_OFF_DQ, _OFF_DK])
        w_gate = w[:, _OFF_MG:_OFF_MG + N_BRANCH * D_MODEL].astype(BF16)
        w_ff_t = w[:, _OFF_FF:_OFF_FF + FOX_HEADS].T.astype(BF16)
        pa, h = _plain_proj(x2, pre_norm_g[l].reshape(1, D_MODEL), w_plain,
                            _scale_vec(7, A_FQ), TN_PLAIN)
        ps = _epilogue_proj(_silu_proj_kernel, "proj_silu", h, w_silu, TN_SILU, [], [])
        pr = _epilogue_proj(_rot_proj_kernel, "proj_rotary", h, w_rot, TN_ROT,
                            [_scale_vec(2, R_DQ), rc, rs1, rs2],
                            [_col_vec(TN_ROT), _ROPE_TILE, _ROPE_TILE, _ROPE_TILE])
        pg = _epilogue_proj(_gate_proj_kernel, "proj_gate", h, w_gate, TN_GATE,
                            [b_merge[l].reshape(1, N_BRANCH * D_MODEL)], [_col_vec(TN_GATE)])
        crow, ccol = _forget_cumsum(h, w_ff_t, b_forget[l].reshape(FOX_HEADS, 1), batch, seq)
        y_fox = _fox_attention(pa, ps, crow, ccol, batch, seq)
        lam_vecs = jnp.stack([lam_q1[l], lam_k1[l], lam_q2[l], lam_k2[l]])
        y_diff = _diff_attention(pr, pa, ps, lam_vecs, diff_norm_g[l].reshape(1, LANES),
                                 lam_init, batch, seq)
        x2 = _merge(x2, y_fox, y_diff, pa, ps, pg, conv_w[l],
                    w_br_fox[l].astype(BF16), w_br_diff[l].astype(BF16),
                    w_br_conv[l].astype(BF16), w_out[l].astype(BF16),
                    post_norm_g[l].reshape(1, D_MODEL), seq)
    return x2.reshape(batch, seq, D_MODEL)
```

```python
import functools
import math
from typing import Any, Callable, NamedTuple, Optional

import jax
import jax.numpy as jnp
from jax import lax
from jax.experimental import pallas as pl
from jax.experimental.pallas import tpu as pltpu

F32 = jnp.float32
BF16 = jnp.bfloat16

D_MODEL = 1024
HEAD_DIM = 64
FOX_HEADS = 8
DIFF_HEADS = 4
WIDTH = 512
CONV_K = 3
N_BRANCH = 3
ROPE_THETA = 500000.0
ROT_DIM = HEAD_DIM // 4
RMS_EPS = 1e-6
LANES = 128
SUBLANES = 8
LOG2E = math.log2(math.e)

(A_FQ, A_FK, A_FV, A_DV, A_CB, A_CC, A_CX) = range(7)
(S_FG, S_DG, S_CG) = range(3)
(R_DQ, R_DK) = range(2)
N_GATE_BLOCKS = N_BRANCH * D_MODEL // WIDTH

_OFF_FQ, _OFF_FK, _OFF_FV = 0, 512, 1024
_OFF_FF = 1536
_OFF_FG = 1544
_OFF_DQ, _OFF_DK, _OFF_DV, _OFF_DG = 2056, 2568, 3080, 3592
_OFF_CB, _OFF_CC, _OFF_CX, _OFF_CG = 4104, 4616, 5128, 5640
_OFF_MG = 6152

TM_IN = 1024
TN_PLAIN = 1792
TN_SILU = 1536
TN_ROT = 1024
TN_GATE = 3072
TS_FF = 512
TQ = 512
TK = 512
TM_OUT = 512
PAIR_ROWS = 2 * TQ
ATTN_VMEM_LIMIT = 58 * 1024 * 1024
VMEM_LIMIT = 48 * 1024 * 1024


def _silu(v):
    return v * jax.nn.sigmoid(v)


def _rotary_block(acc, c, s1, s2):
    outs = []
    for i in range(acc.shape[1] // LANES):
        v = acc[:, i * LANES:(i + 1) * LANES]
        outs.append(v * c + pltpu.roll(v, LANES - ROT_DIM // 2, 1) * s1
                    + pltpu.roll(v, ROT_DIM // 2, 1) * s2)
    return jnp.concatenate(outs, axis=1)


def _plain_proj_kernel(x_ref, g_ref, w_ref, sc_ref, o_ref, h_ref, h_scr):
    @pl.when(pl.program_id(1) == 0)
    def _():
        x = x_ref[...]
        ms = jnp.mean(x * x, axis=-1, keepdims=True)
        h = (x * lax.rsqrt(ms + RMS_EPS) * g_ref[...]).astype(BF16)
        h_scr[...] = h
        h_ref[...] = h

    acc = jnp.dot(h_scr[...], w_ref[...], preferred_element_type=F32)
    o_ref[...] = (acc * sc_ref[...]).astype(BF16)


def _silu_proj_kernel(h_ref, w_ref, o_ref):
    acc = jnp.dot(h_ref[...], w_ref[...], preferred_element_type=F32)
    o_ref[...] = _silu(acc).astype(BF16)


def _rot_proj_kernel(h_ref, w_ref, sc_ref, rc_ref, rs1_ref, rs2_ref, o_ref):
    acc = jnp.dot(h_ref[...], w_ref[...], preferred_element_type=F32)
    r = _rotary_block(acc, rc_ref[...], rs1_ref[...], rs2_ref[...])
    o_ref[...] = (r * sc_ref[...]).astype(BF16)


def _gate_proj_kernel(h_ref, w_ref, b_ref, o_ref):
    acc = jnp.dot(h_ref[...], w_ref[...], preferred_element_type=F32)
    o_ref[...] = jax.nn.sigmoid(acc + b_ref[...]).astype(BF16)


_PROJ_PARAMS = pltpu.CompilerParams(
    dimension_semantics=("arbitrary", "arbitrary"), vmem_limit_bytes=VMEM_LIMIT)
_ROW_TILE = pl.BlockSpec((TM_IN, D_MODEL), lambda i, j: (i, 0))
_ROPE_TILE = pl.BlockSpec((TM_IN, LANES), lambda i, j: (i, 0))


def _w_block(tn):
    return pl.BlockSpec((D_MODEL, tn), lambda i, j: (0, j))


def _col_vec(tn):
    return pl.BlockSpec((1, tn), lambda i, j: (0, j))


def _out_block(tn):
    return pl.BlockSpec((TM_IN, tn), lambda i, j: (i, j))


def _plain_proj(x2, pre_g, w, col_scale, tn):
    t, n = x2.shape[0], w.shape[1]
    return pl.pallas_call(
        _plain_proj_kernel,
        grid=(t // TM_IN, n // tn),
        in_specs=[_ROW_TILE, pl.BlockSpec((1, D_MODEL), lambda i, j: (0, 0)),
                  _w_block(tn), _col_vec(tn)],
        out_specs=[_out_block(tn), _ROW_TILE],
        out_shape=[jax.ShapeDtypeStruct((t, n), BF16), jax.ShapeDtypeStruct((t, D_MODEL), BF16)],
        scratch_shapes=[pltpu.VMEM((TM_IN, D_MODEL), BF16)],
        compiler_params=_PROJ_PARAMS,
        name="proj_plain",
    )(x2, pre_g, w, col_scale)


def _epilogue_proj(body, name, h, w, tn, extra, extra_specs):
    t, n = h.shape[0], w.shape[1]
    return pl.pallas_call(
        body,
        grid=(t // TM_IN, n // tn),
        in_specs=[_ROW_TILE, _w_block(tn)] + extra_specs,
        out_specs=_out_block(tn),
        out_shape=jax.ShapeDtypeStruct((t, n), BF16),
        compiler_params=_PROJ_PARAMS,
        name=name,
    )(h, w, *extra)


def _split3(v):
    hi = v.astype(BF16)
    r1 = v - hi.astype(F32)
    mid = r1.astype(BF16)
    lo = (r1 - mid.astype(F32)).astype(BF16)
    return hi, mid, lo


def _forget_kernel(h_ref, wt_ref, bf_ref, crow_ref, ccol_ref, carry_scr):
    si = pl.program_id(1)

    @pl.when(si == 0)
    def _():
        carry_scr[...] = jnp.zeros_like(carry_scr)

    ff = lax.dot_general(wt_ref[...], h_ref[...], (((1,), (1,)), ((), ())),
                         preferred_element_type=F32)
    lf = jax.nn.log_sigmoid(ff + bf_ref[...])
    row = lax.broadcasted_iota(jnp.int32, (TS_FF, TS_FF), 0)
    col = lax.broadcasted_iota(jnp.int32, (TS_FF, TS_FF), 1)
    upper = (row <= col).astype(BF16)
    cum = jnp.zeros((FOX_HEADS, TS_FF), F32)
    for piece in _split3(lf):
        cum = cum + jnp.dot(piece, upper, preferred_element_type=F32)
    c = cum + carry_scr[:, 0:1]
    carry_scr[...] = jnp.broadcast_to(c[:, TS_FF - 1:TS_FF], carry_scr.shape)
    c2 = c * LOG2E
    crow_ref[0, 0] = c2
    padded = jnp.concatenate([c2, jnp.zeros((LANES - FOX_HEADS, TS_FF), F32)], axis=0)
    ccol_ref[...] = padded.T


def _forget_cumsum(h, w_ff_t, b_f, batch, seq):
    nt = seq // TS_FF
    return pl.pallas_call(
        _forget_kernel,
        grid=(batch, nt),
        in_specs=[
            pl.BlockSpec((TS_FF, D_MODEL), lambda b, s: (b * nt + s, 0)),
            pl.BlockSpec((FOX_HEADS, D_MODEL), lambda b, s: (0, 0)),
            pl.BlockSpec((FOX_HEADS, 1), lambda b, s: (0, 0)),
        ],
        out_specs=[
            pl.BlockSpec((1, 1, FOX_HEADS, TS_FF), lambda b, s: (b, s, 0, 0)),
            pl.BlockSpec((TS_FF, LANES), lambda b, s: (b * nt + s, 0)),
        ],
        out_shape=[
            jax.ShapeDtypeStruct((batch, nt, FOX_HEADS, TS_FF), F32),
            jax.ShapeDtypeStruct((batch * seq, LANES), F32),
        ],
        scratch_shapes=[pltpu.VMEM((FOX_HEADS, LANES), F32)],
        compiler_params=pltpu.CompilerParams(
            dimension_semantics=("arbitrary", "arbitrary"), vmem_limit_bytes=VMEM_LIMIT),
        name="forget_cumsum",
    )(h, w_ff_t, b_f)


class _MapPair(NamedTuple):
    blk: int
    key_bias: Optional[Callable[[Any], Any]]
    row_shift: Any


def _attention_maps(maps, qi, qm_scr, k_ref, v_ref, bufs, mb_scr, acc_scr, ls_scr, emit):
    mxu_row_sums = ls_scr is None
    row = lax.broadcasted_iota(jnp.int32, (TQ, TK), 0)
    col = lax.broadcasted_iota(jnp.int32, (TQ, TK), 1)
    causal = col <= row
    n_lane_blocks = TK // LANES

    def lanes(v, c):
        return v[:, c * LANES:(c + 1) * LANES]

    def chunk_rows(ref, mp, ki):
        start = pl.multiple_of(ki * TK, TK)
        return ref[pl.ds(start, TK), mp.blk * LANES:(mp.blk + 1) * LANES]

    def pass1_begin(buf):
        bufs[buf][1][...] = jnp.full((PAIR_ROWS, LANES), -jnp.inf, F32)

    def pass1_chunk(mp, buf, ki, diagonal):
        t_scr, mx_scr = bufs[buf]
        s2 = lax.dot_general(qm_scr[mp.blk], chunk_rows(k_ref, mp, ki),
                             (((1,), (1,)), ((), ())), preferred_element_type=F32)
        biases = mp.key_bias(ki) if mp.key_bias is not None else (None, None)
        for half, bias in enumerate(biases):
            rows = slice(half * TQ, (half + 1) * TQ)
            s = s2[rows, :]
            if bias is not None:
                s = s - bias
            if diagonal:
                s = jnp.where(causal, s, -jnp.inf)
            t_scr[ki, rows, :] = s
            m = mx_scr[rows, :]
            for c in range(n_lane_blocks):
                m = jnp.maximum(m, lanes(s, c))
            mx_scr[rows, :] = m

    def pass2_begin(mp, buf):
        m = jnp.max(bufs[buf][1][...], axis=1, keepdims=True)
        if mp.row_shift is not None:
            m = (m + mp.row_shift) - mp.row_shift
        mb_scr[...] = jnp.broadcast_to(m, (PAIR_ROWS, LANES))
        acc_scr[...] = jnp.zeros(acc_scr.shape, F32)
        if not mxu_row_sums:
            ls_scr[...] = jnp.zeros((PAIR_ROWS, LANES), F32)

    def pass2_chunk(mp, buf, ki):
        t = bufs[buf][0][ki]
        mb = mb_scr[...]
        ps = [jnp.exp2(lanes(t, c) - mb) for c in range(n_lane_blocks)]
        v = chunk_rows(v_ref, mp, ki)
        if mxu_row_sums:
            v = jnp.concatenate([v, jnp.ones((TK, LANES), BF16)], axis=1)
        else:
            ls = ls_scr[...]
            for p in ps:
                ls = ls + p
            ls_scr[...] = ls
        p = jnp.concatenate(ps, axis=1).astype(BF16)
        acc_scr[...] += jnp.dot(p, v, preferred_element_type=F32)

    def pass2_end(n):
        acc = acc_scr[...]
        if mxu_row_sums:
            l = acc[:, LANES:LANES + 1]
        else:
            l = jnp.sum(ls_scr[...], axis=1, keepdims=True)
        emit(n, acc[:, 0:LANES] / l)

    def loop(n_chunks, body):
        def step(i, carry):
            body(2 * i)
            body(2 * i + 1)
            return carry
        lax.fori_loop(0, lax.shift_right_logical(n_chunks, 1), step, 0)

        @pl.when(lax.bitwise_and(n_chunks, 1) == 1)
        def _():
            body(n_chunks - 1)

    pass1_begin(0)
    pass1_chunk(maps[0], 0, qi, True)
    loop(qi, lambda ki: pass1_chunk(maps[0], 0, ki, False))
    for n in range(1, len(maps)):
        buf, prev = n % 2, (n - 1) % 2
        pass2_begin(maps[n - 1], prev)
        pass1_begin(buf)
        pass1_chunk(maps[n], buf, qi, True)

        def both(ki, n=n, buf=buf, prev=prev):
            pass2_chunk(maps[n - 1], prev, ki)
            pass1_chunk(maps[n], buf, ki, False)

        loop(qi, both)
        pass2_chunk(maps[n - 1], prev, qi)
        pass2_end(n - 1)
    last = len(maps) - 1
    pass2_begin(maps[last], last % 2)
    loop(qi + 1, lambda ki: pass2_chunk(maps[last], last % 2, ki))
    pass2_end(last)


def _store_masked_queries(q_ref, qm_scr):
    first = lax.broadcasted_iota(jnp.int32, (TQ, LANES), 1) < HEAD_DIM
    for b in range(WIDTH // LANES):
        q = q_ref[:, b * LANES:(b + 1) * LANES]
        zero = jnp.zeros_like(q)
        qm_scr[b, 0:TQ, :] = jnp.where(first, q, zero)
        qm_scr[b, TQ:PAIR_ROWS, :] = jnp.where(first, zero, q)


def _fox_kernel(q_ref, k_ref, v_ref, g_ref, crow_ref, ccol_ref, o_ref,
                qm_scr, ta_scr, tb_scr, mxa_scr, mxb_scr, mb_scr, acc_scr):
    qi = pl.program_id(1)
    first = lax.broadcasted_iota(jnp.int32, (TQ, LANES), 1) < HEAD_DIM
    _store_masked_queries(q_ref, qm_scr)
    ccol = ccol_ref[...]
    maps = [
        _MapPair(blk=b,
                 key_bias=lambda ki, b=b: (crow_ref[0, ki, 2 * b:2 * b + 1, :],
                                           crow_ref[0, ki, 2 * b + 1:2 * b + 2, :]),
                 row_shift=jnp.concatenate([ccol[:, 2 * b:2 * b + 1],
                                            ccol[:, 2 * b + 1:2 * b + 2]], axis=0))
        for b in range(FOX_HEADS // 2)
    ]

    def emit(b, o):
        blk = slice(b * LANES, (b + 1) * LANES)
        y = jnp.where(first, o[0:TQ, :], o[TQ:PAIR_ROWS, :]) * g_ref[:, blk].astype(F32)
        o_ref[:, blk] = y.astype(BF16)

    _attention_maps(maps, qi, qm_scr, k_ref, v_ref, ((ta_scr, mxa_scr), (tb_scr, mxb_scr)),
                    mb_scr, acc_scr, None, emit)


def _diff_kernel(q_ref, k_ref, v_ref, g_ref, lam_ref, ng_ref, o_ref,
                 qm_scr, ta_scr, tb_scr, mxa_scr, mxb_scr, mb_scr, acc_scr, ls_scr,
                 *, lam_init):
    qi = pl.program_id(1)
    _store_masked_queries(q_ref, qm_scr)
    lv = lam_ref[...]
    lam = (jnp.exp(jnp.sum(lv[0:1] * lv[1:2], axis=1, keepdims=True))
           - jnp.exp(jnp.sum(lv[2:3] * lv[3:4], axis=1, keepdims=True)) + lam_init)
    maps = [_MapPair(blk=h, key_bias=None, row_shift=None) for h in range(DIFF_HEADS)]

    def emit(h, o):
        blk = slice(h * LANES, (h + 1) * LANES)
        d = o[0:TQ, :] - lam * o[TQ:PAIR_ROWS, :]
        ms = jnp.mean(d * d, axis=-1, keepdims=True)
        y = d * lax.rsqrt(ms + RMS_EPS) * ng_ref[...] * (1.0 - lam_init)
        o_ref[:, blk] = (y * g_ref[:, blk].astype(F32)).astype(BF16)

    _attention_maps(maps, qi, qm_scr, k_ref, v_ref, ((ta_scr, mxa_scr), (tb_scr, mxb_scr)),
                    mb_scr, acc_scr, ls_scr, emit)


def _attn_scratch(seq, mxu_row_sums):
    stat = pltpu.VMEM((PAIR_ROWS, LANES), F32)
    logits = pltpu.VMEM((seq // TK, PAIR_ROWS, TK), F32)
    common = [pltpu.VMEM((WIDTH // LANES, PAIR_ROWS, LANES), BF16),
              logits, logits, stat, stat,
              stat]
    if mxu_row_sums:
        return common + [pltpu.VMEM((PAIR_ROWS, 2 * LANES), F32)]
    return common + [stat, stat]


def _per_sequence(block_shape, col):
    return pl.BlockSpec(block_shape, lambda b, i: (b, col))


def _fox_attention(pa, ps, crow, ccol, batch, seq):
    nq = seq // TQ
    return pl.pallas_call(
        _fox_kernel,
        grid=(batch, nq),
        in_specs=[
            pl.BlockSpec((TQ, WIDTH), lambda b, i: (b * nq + i, A_FQ)),
            _per_sequence((seq, WIDTH), A_FK),
            _per_sequence((seq, WIDTH), A_FV),
            pl.BlockSpec((TQ, WIDTH), lambda b, i: (b * nq + i, S_FG)),
            pl.BlockSpec((1, seq // TK, FOX_HEADS, TK), lambda b, i: (b, 0, 0, 0)),
            pl.BlockSpec((TQ, LANES), lambda b, i: (b * nq + i, 0)),
        ],
        out_specs=pl.BlockSpec((TQ, WIDTH), lambda b, i: (b * nq + i, 0)),
        out_shape=jax.ShapeDtypeStruct((batch * seq, WIDTH), BF16),
        scratch_shapes=_attn_scratch(seq, mxu_row_sums=True),
        compiler_params=pltpu.CompilerParams(
            dimension_semantics=("arbitrary", "arbitrary"), vmem_limit_bytes=ATTN_VMEM_LIMIT),
        name="fox_attention",
    )(pa, pa, pa, ps, crow, ccol)


def _diff_attention(pr, pa, ps, lam_vecs, norm_g, lam_init, batch, seq):
    nq = seq // TQ
    return pl.pallas_call(
        functools.partial(_diff_kernel, lam_init=lam_init),
        grid=(batch, nq),
        in_specs=[
            pl.BlockSpec((TQ, WIDTH), lambda b, i: (b * nq + i, R_DQ)),
            _per_sequence((seq, WIDTH), R_DK),
            _per_sequence((seq, WIDTH), A_DV),
            pl.BlockSpec((TQ, WIDTH), lambda b, i: (b * nq + i, S_DG)),
            pl.BlockSpec((4, HEAD_DIM), lambda b, i: (0, 0)),
            pl.BlockSpec((1, LANES), lambda b, i: (0, 0)),
        ],
        out_specs=pl.BlockSpec((TQ, WIDTH), lambda b, i: (b * nq + i, 0)),
        out_shape=jax.ShapeDtypeStruct((batch * seq, WIDTH), BF16),
        scratch_shapes=_attn_scratch(seq, mxu_row_sums=False),
        compiler_params=pltpu.CompilerParams(
            dimension_semantics=("arbitrary", "arbitrary"), vmem_limit_bytes=ATTN_VMEM_LIMIT),
        name="diff_attention",
    )(pr, pr, pa, ps, lam_vecs, norm_g)


def _merge_kernel(x_ref, yf_ref, yd_ref, cb_ref, cc_ref, cx_ref, cg_ref, hc_ref, hx_ref,
                  g0_ref, g1_ref, g2_ref, cw_ref, wf_ref, wd_ref, wc_ref, wo_ref, pg_ref,
                  o_ref, ext_scr, *, tiles_per_seq):
    i = pl.program_id(0)
    halo = SUBLANES
    u = cc_ref[...].astype(F32) * cx_ref[...].astype(F32)
    prev = hc_ref[...].astype(F32) * hx_ref[...].astype(F32)
    prev = jnp.where(i % tiles_per_seq == 0, jnp.zeros_like(prev), prev)
    ext_scr[0:halo, :] = prev
    ext_scr[halo:halo + TM_OUT, :] = u
    cw = cw_ref[...]
    conv = (cw[2:3] * u + cw[1:2] * ext_scr[halo - 1:halo - 1 + TM_OUT, :]
            + cw[0:1] * ext_scr[halo - 2:halo - 2 + TM_OUT, :])
    y_conv = (cb_ref[...].astype(F32) * conv * cg_ref[...].astype(F32)).astype(BF16)

    m = (g0_ref[...].astype(F32) * jnp.dot(yf_ref[...], wf_ref[...], preferred_element_type=F32)
         + g1_ref[...].astype(F32) * jnp.dot(yd_ref[...], wd_ref[...], preferred_element_type=F32)
         + g2_ref[...].astype(F32) * jnp.dot(y_conv, wc_ref[...], preferred_element_type=F32))
    o = jnp.dot(m.astype(BF16), wo_ref[...], preferred_element_type=F32)
    ms = jnp.mean(o * o, axis=-1, keepdims=True)
    o_ref[...] = x_ref[...] + o * lax.rsqrt(ms + RMS_EPS) * pg_ref[...]


def _merge(x2, y_fox, y_diff, pa, ps, pg, conv_w, w_fox, w_diff, w_conv, w_out, post_g, seq):
    t = x2.shape[0]
    tiles_per_seq = seq // TM_OUT
    halo_blocks = TM_OUT // SUBLANES

    def tile(col):
        return pl.BlockSpec((TM_OUT, WIDTH), lambda i: (i, col))

    def halo(col):
        return pl.BlockSpec((SUBLANES, WIDTH),
                            lambda i: (jnp.maximum(i * halo_blocks - 1, 0), col))

    def whole(shape):
        return pl.BlockSpec(shape, lambda i: (0, 0))

    return pl.pallas_call(
        functools.partial(_merge_kernel, tiles_per_seq=tiles_per_seq),
        grid=(t // TM_OUT,),
        in_specs=[
            pl.BlockSpec((TM_OUT, D_MODEL), lambda i: (i, 0)),
            pl.BlockSpec((TM_OUT, WIDTH), lambda i: (i, 0)),
            pl.BlockSpec((TM_OUT, WIDTH), lambda i: (i, 0)),
            tile(A_CB), tile(A_CC), tile(A_CX), tile(S_CG),
            halo(A_CC), halo(A_CX),
            pl.BlockSpec((TM_OUT, D_MODEL), lambda i: (i, 0)),
            pl.BlockSpec((TM_OUT, D_MODEL), lambda i: (i, 1)),
            pl.BlockSpec((TM_OUT, D_MODEL), lambda i: (i, 2)),
            whole((CONV_K, WIDTH)),
            whole((WIDTH, D_MODEL)), whole((WIDTH, D_MODEL)), whole((WIDTH, D_MODEL)),
            whole((D_MODEL, D_MODEL)),
            whole((1, D_MODEL)),
        ],
        out_specs=pl.BlockSpec((TM_OUT, D_MODEL), lambda i: (i, 0)),
        out_shape=jax.ShapeDtypeStruct((t, D_MODEL), F32),
        scratch_shapes=[pltpu.VMEM((TM_OUT + SUBLANES, WIDTH), F32)],
        compiler_params=pltpu.CompilerParams(
            dimension_semantics=("arbitrary",), vmem_limit_bytes=VMEM_LIMIT),
        name="merge",
    )(x2, y_fox, y_diff, pa, pa, pa, ps, pa, pa, pg, pg, pg,
      conv_w, w_fox, w_diff, w_conv, w_out, post_g)


def _rope_lane_tables(positions):
    half = ROT_DIM // 2
    inv_freq = ROPE_THETA ** (-jnp.arange(0, ROT_DIM, 2, dtype=F32) / ROT_DIM)
    ang = positions.astype(F32).reshape(-1, 1) * inv_freq
    cos, sin = jnp.cos(ang), jnp.sin(ang)
    t = ang.shape[0]
    ones = jnp.ones((t, HEAD_DIM - ROT_DIM), F32)
    zeros_rest = jnp.zeros((t, HEAD_DIM - ROT_DIM), F32)
    zeros_half = jnp.zeros((t, half), F32)
    c64 = jnp.concatenate([cos, cos, ones], axis=1)
    s1_64 = jnp.concatenate([-sin, zeros_half, zeros_rest], axis=1)
    s2_64 = jnp.concatenate([zeros_half, sin, zeros_rest], axis=1)
    rep = LANES // HEAD_DIM
    return jnp.tile(c64, (1, rep)), jnp.tile(s1_64, (1, rep)), jnp.tile(s2_64, (1, rep))


def _gather_cols(w, offsets):
    return jnp.concatenate([w[:, o:o + WIDTH] for o in offsets], axis=1).astype(BF16)


def _scale_vec(n_blocks, scaled_block):
    blk = jnp.arange(n_blocks * WIDTH, dtype=jnp.int32) // WIDTH
    q_scale = HEAD_DIM ** -0.5 * LOG2E
    return jnp.where(blk == scaled_block, q_scale, 1.0).astype(F32).reshape(1, -1)


def kernel(x, positions, pre_norm_g, w_in, b_forget, b_merge, conv_w, lam_q1, lam_k1, lam_q2,
           lam_k2, diff_norm_g, w_br_fox, w_br_diff, w_br_conv, w_out, post_norm_g):
    batch, seq, _ = x.shape
    depth = w_in.shape[0]
    assert seq % TQ == 0 and TQ == TK == TS_FF and (batch * seq) % TM_IN == 0
    rc, rs1, rs2 = _rope_lane_tables(positions)
    x2 = x.reshape(batch * seq, D_MODEL)
    for l in range(depth):
        lam_init = 0.8 - 0.6 * math.exp(-0.3 * l)
        w = w_in[l]
        w_plain = _gather_cols(w, [_OFF_FQ, _OFF_FK, _OFF_FV, _OFF_DV, _OFF_CB, _OFF_CC, _OFF_CX])
        w_silu = _gather_cols(w, [_OFF_FG, _OFF_DG, _OFF_CG])
        w_rot = _gather_cols(w, [_OFF_DQ, _OFF_DK])
        w_gate = w[:, _OFF_MG:_OFF_MG + N_BRANCH * D_MODEL].astype(BF16)
        w_ff_t = w[:, _OFF_FF:_OFF_FF + FOX_HEADS].T.astype(BF16)
        pa, h = _plain_proj(x2, pre_norm_g[l].reshape(1, D_MODEL), w_plain,
                            _scale_vec(7, A_FQ), TN_PLAIN)
        ps = _epilogue_proj(_silu_proj_kernel, "proj_silu", h, w_silu, TN_SILU, [], [])
        pr = _epilogue_proj(_rot_proj_kernel, "proj_rotary", h, w_rot, TN_ROT,
                            [_scale_vec(2, R_DQ), rc, rs1, rs2],
                            [_col_vec(TN_ROT), _ROPE_TILE, _ROPE_TILE, _ROPE_TILE])
        pg = _epilogue_proj(_gate_proj_kernel, "proj_gate", h, w_gate, TN_GATE,
                            [b_merge[l].reshape(1, N_BRANCH * D_MODEL)], [_col_vec(TN_GATE)])
        crow, ccol = _forget_cumsum(h, w_ff_t, b_forget[l].reshape(FOX_HEADS, 1), batch, seq)
        y_fox = _fox_attention(pa, ps, crow, ccol, batch, seq)
        lam_vecs = jnp.stack([lam_q1[l], lam_k1[l], lam_q2[l], lam_k2[l]])
        y_diff = _diff_attention(pr, pa, ps, lam_vecs, diff_norm_g[l].reshape(1, LANES),
                                 lam_init, batch, seq)
        x2 = _merge(x2, y_fox, y_diff, pa, ps, pg, conv_w[l],
                    w_br_fox[l].astype(BF16), w_br_diff[l].astype(BF16),
                    w_br_conv[l].astype(BF16), w_out[l].astype(BF16),
                    post_norm_g[l].reshape(1, D_MODEL), seq)
    return x2.reshape(batch, seq, D_MODEL)
```

```python
import functools
import math
from typing import Any, Callable, NamedTuple, Optional

import jax
import jax.numpy as jnp
from jax import lax
from jax.experimental import pallas as pl
from jax.experimental.pallas import tpu as pltpu

F32 = jnp.float32
BF16 = jnp.bfloat16

D_MODEL = 1024
HEAD_DIM = 64
FOX_HEADS = 8
DIFF_HEADS = 4
WIDTH = 512
CONV_K = 3
N_BRANCH = 3
ROPE_THETA = 500000.0
ROT_DIM = HEAD_DIM // 4
RMS_EPS = 1e-6
LANES = 128
SUBLANES = 8
LOG2E = math.log2(math.e)

(A_FQ, A_FK, A_FV, A_DV, A_CB, A_CC, A_CX) = range(7)
(S_FG, S_DG, S_CG) = range(3)
(R_DQ, R_DK) = range(2)
N_GATE_BLOCKS = N_BRANCH * D_MODEL // WIDTH

_OFF_FQ, _OFF_FK, _OFF_FV = 0, 512, 1024
_OFF_FF = 1536
_OFF_FG = 1544
_OFF_DQ, _OFF_DK, _OFF_DV, _OFF_DG = 2056, 2568, 3080, 3592
_OFF_CB, _OFF_CC, _OFF_CX, _OFF_CG = 4104, 4616, 5128, 5640
_OFF_MG = 6152

TM_IN = 1024
TN_PLAIN = 1792
TN_SILU = 1536
TN_ROT = 1024
TN_GATE = 3072
TS_FF = 512
TQ = 512
TK = 512
TM_OUT = 512
PAIR_ROWS = 2 * TQ
ATTN_VMEM_LIMIT = 58 * 1024 * 1024
VMEM_LIMIT = 48 * 1024 * 1024


def _silu(v):
    return v * jax.nn.sigmoid(v)


def _rotary_block(acc, c, s):
    half = ROT_DIM // 2
    lane = lax.broadcasted_iota(jnp.int32, (acc.shape[0], LANES), 1)
    takes_upper = lax.bitwise_and(lane, ROT_DIM - 1) < half
    outs = []
    for i in range(acc.shape[1] // LANES):
        v = acc[:, i * LANES:(i + 1) * LANES]
        partner = jnp.where(takes_upper, pltpu.roll(v, LANES - half, 1), pltpu.roll(v, half, 1))
        outs.append(v * c + partner * s)
    return jnp.concatenate(outs, axis=1)


def _plain_proj_kernel(x_ref, g_ref, w_ref, sc_ref, o_ref, h_ref, h_scr):
    @pl.when(pl.program_id(1) == 0)
    def _():
        x = x_ref[...]
        ms = jnp.mean(x * x, axis=-1, keepdims=True)
        h = (x * lax.rsqrt(ms + RMS_EPS) * g_ref[...]).astype(BF16)
        h_scr[...] = h
        h_ref[...] = h

    acc = jnp.dot(h_scr[...], w_ref[...], preferred_element_type=F32)
    o_ref[...] = (acc * sc_ref[...]).astype(BF16)


def _silu_proj_kernel(h_ref, w_ref, o_ref):
    acc = jnp.dot(h_ref[...], w_ref[...], preferred_element_type=F32)
    o_ref[...] = _silu(acc).astype(BF16)


def _rot_proj_kernel(h_ref, w_ref, sc_ref, rc_ref, rs_ref, o_ref):
    acc = jnp.dot(h_ref[...], w_ref[...], preferred_element_type=F32)
    r = _rotary_block(acc, rc_ref[...], rs_ref[...])
    o_ref[...] = (r * sc_ref[...]).astype(BF16)


def _gate_proj_kernel(h_ref, w_ref, b_ref, o_ref):
    acc = jnp.dot(h_ref[...], w_ref[...], preferred_element_type=F32)
    o_ref[...] = jax.nn.sigmoid(acc + b_ref[...]).astype(BF16)


_PROJ_PARAMS = pltpu.CompilerParams(
    dimension_semantics=("arbitrary", "arbitrary"), vmem_limit_bytes=VMEM_LIMIT)
_ROW_TILE = pl.BlockSpec((TM_IN, D_MODEL), lambda i, j: (i, 0))
_ROPE_TILE = pl.BlockSpec((TM_IN, LANES), lambda i, j: (i, 0))


def _w_block(tn):
    return pl.BlockSpec((D_MODEL, tn), lambda i, j: (0, j))


def _col_vec(tn):
    return pl.BlockSpec((1, tn), lambda i, j: (0, j))


def _out_block(tn):
    return pl.BlockSpec((TM_IN, tn), lambda i, j: (i, j))


def _plain_proj(x2, pre_g, w, col_scale, tn):
    t, n = x2.shape[0], w.shape[1]
    return pl.pallas_call(
        _plain_proj_kernel,
        grid=(t // TM_IN, n // tn),
        in_specs=[_ROW_TILE, pl.BlockSpec((1, D_MODEL), lambda i, j: (0, 0)),
                  _w_block(tn), _col_vec(tn)],
        out_specs=[_out_block(tn), _ROW_TILE],
        out_shape=[jax.ShapeDtypeStruct((t, n), BF16), jax.ShapeDtypeStruct((t, D_MODEL), BF16)],
        scratch_shapes=[pltpu.VMEM((TM_IN, D_MODEL), BF16)],
        compiler_params=_PROJ_PARAMS,
        name="proj_plain",
    )(x2, pre_g, w, col_scale)


def _epilogue_proj(body, name, h, w, tn, extra, extra_specs):
    t, n = h.shape[0], w.shape[1]
    return pl.pallas_call(
        body,
        grid=(t // TM_IN, n // tn),
        in_specs=[_ROW_TILE, _w_block(tn)] + extra_specs,
        out_specs=_out_block(tn),
        out_shape=jax.ShapeDtypeStruct((t, n), BF16),
        compiler_params=_PROJ_PARAMS,
        name=name,
    )(h, w, *extra)


def _split3(v):
    hi = v.astype(BF16)
    r1 = v - hi.astype(F32)
    mid = r1.astype(BF16)
    lo = (r1 - mid.astype(F32)).astype(BF16)
    return hi, mid, lo


def _forget_kernel(h_ref, wt_ref, bf_ref, crow_ref, ccol_ref, carry_scr):
    si = pl.program_id(1)

    @pl.when(si == 0)
    def _():
        carry_scr[...] = jnp.zeros_like(carry_scr)

    ff = lax.dot_general(wt_ref[...], h_ref[...], (((1,), (1,)), ((), ())),
                         preferred_element_type=F32)
    lf = jax.nn.log_sigmoid(ff + bf_ref[...])
    row = lax.broadcasted_iota(jnp.int32, (TS_FF, TS_FF), 0)
    col = lax.broadcasted_iota(jnp.int32, (TS_FF, TS_FF), 1)
    upper = (row <= col).astype(BF16)
    cum = jnp.zeros((FOX_HEADS, TS_FF), F32)
    for piece in _split3(lf):
        cum = cum + jnp.dot(piece, upper, preferred_element_type=F32)
    c = cum + carry_scr[:, 0:1]
    carry_scr[...] = jnp.broadcast_to(c[:, TS_FF - 1:TS_FF], carry_scr.shape)
    c2 = c * LOG2E
    crow_ref[0, 0] = c2
    padded = jnp.concatenate([c2, jnp.zeros((LANES - FOX_HEADS, TS_FF), F32)], axis=0)
    ccol_ref[...] = padded.T


def _forget_cumsum(h, w_ff_t, b_f, batch, seq):
    nt = seq // TS_FF
    return pl.pallas_call(
        _forget_kernel,
        grid=(batch, nt),
        in_specs=[
            pl.BlockSpec((TS_FF, D_MODEL), lambda b, s: (b * nt + s, 0)),
            pl.BlockSpec((FOX_HEADS, D_MODEL), lambda b, s: (0, 0)),
            pl.BlockSpec((FOX_HEADS, 1), lambda b, s: (0, 0)),
        ],
        out_specs=[
            pl.BlockSpec((1, 1, FOX_HEADS, TS_FF), lambda b, s: (b, s, 0, 0)),
            pl.BlockSpec((TS_FF, LANES), lambda b, s: (b * nt + s, 0)),
        ],
        out_shape=[
            jax.ShapeDtypeStruct((batch, nt, FOX_HEADS, TS_FF), F32),
            jax.ShapeDtypeStruct((batch * seq, LANES), F32),
        ],
        scratch_shapes=[pltpu.VMEM((FOX_HEADS, LANES), F32)],
        compiler_params=pltpu.CompilerParams(
            dimension_semantics=("arbitrary", "arbitrary"), vmem_limit_bytes=VMEM_LIMIT),
        name="forget_cumsum",
    )(h, w_ff_t, b_f)


class _MapPair(NamedTuple):
    blk: int
    key_bias: Optional[Callable[[Any], Any]]
    row_shift: Any


def _attention_maps(maps, qi, qm_scr, k_ref, v_ref, bufs, mb_scr, acc_scr, ls_scr, emit):
    mxu_row_sums = ls_scr is None
    row = lax.broadcasted_iota(jnp.int32, (TQ, TK), 0)
    col = lax.broadcasted_iota(jnp.int32, (TQ, TK), 1)
    causal = col <= row
    n_lane_blocks = TK // LANES

    def lanes(v, c):
        return v[:, c * LANES:(c + 1) * LANES]

    def chunk_rows(ref, mp, ki):
        start = pl.multiple_of(ki * TK, TK)
        return ref[pl.ds(start, TK), mp.blk * LANES:(mp.blk + 1) * LANES]

    def pass1_begin(buf):
        bufs[buf][1][...] = jnp.full((PAIR_ROWS, LANES), -jnp.inf, F32)

    def pass1_chunk(mp, buf, ki, diagonal):
        t_scr, mx_scr = bufs[buf]
        s2 = lax.dot_general(qm_scr[mp.blk], chunk_rows(k_ref, mp, ki),
                             (((1,), (1,)), ((), ())), preferred_element_type=F32)
        biases = mp.key_bias(ki) if mp.key_bias is not None else (None, None)
        for half, bias in enumerate(biases):
            rows = slice(half * TQ, (half + 1) * TQ)
            s = s2[rows, :]
            if bias is not None:
                s = s - bias
            if diagonal:
                s = jnp.where(causal, s, -jnp.inf)
            t_scr[ki, rows, :] = s
            m = mx_scr[rows, :]
            for c in range(n_lane_blocks):
                m = jnp.maximum(m, lanes(s, c))
            mx_scr[rows, :] = m

    def pass2_begin(mp, buf):
        m = jnp.max(bufs[buf][1][...], axis=1, keepdims=True)
        if mp.row_shift is not None:
            m = (m + mp.row_shift) - mp.row_shift
        mb_scr[...] = jnp.broadcast_to(m, (PAIR_ROWS, LANES))
        acc_scr[...] = jnp.zeros(acc_scr.shape, F32)
        if not mxu_row_sums:
            ls_scr[...] = jnp.zeros((PAIR_ROWS, LANES), F32)

    def pass2_chunk(mp, buf, ki):
        t = bufs[buf][0][ki]
        mb = mb_scr[...]
        ps = [jnp.exp2(lanes(t, c) - mb) for c in range(n_lane_blocks)]
        v = chunk_rows(v_ref, mp, ki)
        if mxu_row_sums:
            v = jnp.concatenate([v, jnp.ones((TK, LANES), BF16)], axis=1)
        else:
            ls = ls_scr[...]
            for p in ps:
                ls = ls + p
            ls_scr[...] = ls
        p = jnp.concatenate(ps, axis=1).astype(BF16)
        acc_scr[...] += jnp.dot(p, v, preferred_element_type=F32)

    def pass2_end(n):
        acc = acc_scr[...]
        if mxu_row_sums:
            l = acc[:, LANES:LANES + 1]
        else:
            l = jnp.sum(ls_scr[...], axis=1, keepdims=True)
        emit(n, acc[:, 0:LANES] / l)

    def loop(n_chunks, body):
        def step(i, carry):
            body(2 * i)
            body(2 * i + 1)
            return carry
        lax.fori_loop(0, lax.shift_right_logical(n_chunks, 1), step, 0)

        @pl.when(lax.bitwise_and(n_chunks, 1) == 1)
        def _():
            body(n_chunks - 1)

    pass1_begin(0)
    pass1_chunk(maps[0], 0, qi, True)
    loop(qi, lambda ki: pass1_chunk(maps[0], 0, ki, False))
    for n in range(1, len(maps)):
        buf, prev = n % 2, (n - 1) % 2
        pass2_begin(maps[n - 1], prev)
        pass1_begin(buf)
        pass1_chunk(maps[n], buf, qi, True)

        def both(ki, n=n, buf=buf, prev=prev):
            pass2_chunk(maps[n - 1], prev, ki)
            pass1_chunk(maps[n], buf, ki, False)

        loop(qi, both)
        pass2_chunk(maps[n - 1], prev, qi)
        pass2_end(n - 1)
    last = len(maps) - 1
    pass2_begin(maps[last], last % 2)
    loop(qi + 1, lambda ki: pass2_chunk(maps[last], last % 2, ki))
    pass2_end(last)


def _store_masked_queries(q_ref, qm_scr):
    first = lax.broadcasted_iota(jnp.int32, (TQ, LANES), 1) < HEAD_DIM
    for b in range(WIDTH // LANES):
        q = q_ref[:, b * LANES:(b + 1) * LANES]
        zero = jnp.zeros_like(q)
        qm_scr[b, 0:TQ, :] = jnp.where(first, q, zero)
        qm_scr[b, TQ:PAIR_ROWS, :] = jnp.where(first, zero, q)


def _fox_kernel(q_ref, k_ref, v_ref, g_ref, crow_ref, ccol_ref, o_ref,
                qm_scr, ta_scr, tb_scr, mxa_scr, mxb_scr, mb_scr, acc_scr):
    qi = pl.program_id(1)
    first = lax.broadcasted_iota(jnp.int32, (TQ, LANES), 1) < HEAD_DIM
    _store_masked_queries(q_ref, qm_scr)
    ccol = ccol_ref[...]
    maps = [
        _MapPair(blk=b,
                 key_bias=lambda ki, b=b: (crow_ref[0, ki, 2 * b:2 * b + 1, :],
                                           crow_ref[0, ki, 2 * b + 1:2 * b + 2, :]),
                 row_shift=jnp.concatenate([ccol[:, 2 * b:2 * b + 1],
                                            ccol[:, 2 * b + 1:2 * b + 2]], axis=0))
        for b in range(FOX_HEADS // 2)
    ]

    def emit(b, o):
        blk = slice(b * LANES, (b + 1) * LANES)
        y = jnp.where(first, o[0:TQ, :], o[TQ:PAIR_ROWS, :]) * g_ref[:, blk].astype(F32)
        o_ref[:, blk] = y.astype(BF16)

    _attention_maps(maps, qi, qm_scr, k_ref, v_ref, ((ta_scr, mxa_scr), (tb_scr, mxb_scr)),
                    mb_scr, acc_scr, None, emit)


def _diff_kernel(q_ref, k_ref, v_ref, g_ref, lam_ref, ng_ref, o_ref,
                 qm_scr, ta_scr, tb_scr, mxa_scr, mxb_scr, mb_scr, acc_scr, ls_scr,
                 *, lam_init):
    qi = pl.program_id(1)
    _store_masked_queries(q_ref, qm_scr)
    lv = lam_ref[...]
    lam = (jnp.exp(jnp.sum(lv[0:1] * lv[1:2], axis=1, keepdims=True))
           - jnp.exp(jnp.sum(lv[2:3] * lv[3:4], axis=1, keepdims=True)) + lam_init)
    maps = [_MapPair(blk=h, key_bias=None, row_shift=None) for h in range(DIFF_HEADS)]

    def emit(h, o):
        blk = slice(h * LANES, (h + 1) * LANES)
        d = o[0:TQ, :] - lam * o[TQ:PAIR_ROWS, :]
        ms = jnp.mean(d * d, axis=-1, keepdims=True)
        y = d * lax.rsqrt(ms + RMS_EPS) * ng_ref[...] * (1.0 - lam_init)
        o_ref[:, blk] = (y * g_ref[:, blk].astype(F32)).astype(BF16)

    _attention_maps(maps, qi, qm_scr, k_ref, v_ref, ((ta_scr, mxa_scr), (tb_scr, mxb_scr)),
                    mb_scr, acc_scr, ls_scr, emit)


def _attn_scratch(seq, mxu_row_sums):
    stat = pltpu.VMEM((PAIR_ROWS, LANES), F32)
    logits = pltpu.VMEM((seq // TK, PAIR_ROWS, TK), F32)
    common = [pltpu.VMEM((WIDTH // LANES, PAIR_ROWS, LANES), BF16),
              logits, logits, stat, stat,
              stat]
    if mxu_row_sums:
        return common + [pltpu.VMEM((PAIR_ROWS, 2 * LANES), F32)]
    return common + [stat, stat]


def _per_sequence(block_shape, col):
    return pl.BlockSpec(block_shape, lambda b, i: (b, col))


def _fox_attention(pa, ps, crow, ccol, batch, seq):
    nq = seq // TQ
    return pl.pallas_call(
        _fox_kernel,
        grid=(batch, nq),
        in_specs=[
            pl.BlockSpec((TQ, WIDTH), lambda b, i: (b * nq + i, A_FQ)),
            _per_sequence((seq, WIDTH), A_FK),
            _per_sequence((seq, WIDTH), A_FV),
            pl.BlockSpec((TQ, WIDTH), lambda b, i: (b * nq + i, S_FG)),
            pl.BlockSpec((1, seq // TK, FOX_HEADS, TK), lambda b, i: (b, 0, 0, 0)),
            pl.BlockSpec((TQ, LANES), lambda b, i: (b * nq + i, 0)),
        ],
        out_specs=pl.BlockSpec((TQ, WIDTH), lambda b, i: (b * nq + i, 0)),
        out_shape=jax.ShapeDtypeStruct((batch * seq, WIDTH), BF16),
        scratch_shapes=_attn_scratch(seq, mxu_row_sums=True),
        compiler_params=pltpu.CompilerParams(
            dimension_semantics=("arbitrary", "arbitrary"), vmem_limit_bytes=ATTN_VMEM_LIMIT),
        name="fox_attention",
    )(pa, pa, pa, ps, crow, ccol)


def _diff_attention(pr, pa, ps, lam_vecs, norm_g, lam_init, batch, seq):
    nq = seq // TQ
    return pl.pallas_call(
        functools.partial(_diff_kernel, lam_init=lam_init),
        grid=(batch, nq),
        in_specs=[
            pl.BlockSpec((TQ, WIDTH), lambda b, i: (b * nq + i, R_DQ)),
            _per_sequence((seq, WIDTH), R_DK),
            _per_sequence((seq, WIDTH), A_DV),
            pl.BlockSpec((TQ, WIDTH), lambda b, i: (b * nq + i, S_DG)),
            pl.BlockSpec((4, HEAD_DIM), lambda b, i: (0, 0)),
            pl.BlockSpec((1, LANES), lambda b, i: (0, 0)),
        ],
        out_specs=pl.BlockSpec((TQ, WIDTH), lambda b, i: (b * nq + i, 0)),
        out_shape=jax.ShapeDtypeStruct((batch * seq, WIDTH), BF16),
        scratch_shapes=_attn_scratch(seq, mxu_row_sums=False),
        compiler_params=pltpu.CompilerParams(
            dimension_semantics=("arbitrary", "arbitrary"), vmem_limit_bytes=ATTN_VMEM_LIMIT),
        name="diff_attention",
    )(pr, pr, pa, ps, lam_vecs, norm_g)


def _merge_kernel(x_ref, yf_ref, yd_ref, cb_ref, cc_ref, cx_ref, cg_ref, hc_ref, hx_ref,
                  g0_ref, g1_ref, g2_ref, cw_ref, wf_ref, wd_ref, wc_ref, wo_ref, pg_ref,
                  o_ref, ext_scr, *, tiles_per_seq):
    i = pl.program_id(0)
    halo = SUBLANES
    u = cc_ref[...].astype(F32) * cx_ref[...].astype(F32)
    prev = hc_ref[...].astype(F32) * hx_ref[...].astype(F32)
    prev = jnp.where(i % tiles_per_seq == 0, jnp.zeros_like(prev), prev)
    ext_scr[0:halo, :] = prev
    ext_scr[halo:halo + TM_OUT, :] = u
    cw = cw_ref[...]
    conv = (cw[2:3] * u + cw[1:2] * ext_scr[halo - 1:halo - 1 + TM_OUT, :]
            + cw[0:1] * ext_scr[halo - 2:halo - 2 + TM_OUT, :])
    y_conv = (cb_ref[...].astype(F32) * conv * cg_ref[...].astype(F32)).astype(BF16)

    m = (g0_ref[...].astype(F32) * jnp.dot(yf_ref[...], wf_ref[...], preferred_element_type=F32)
         + g1_ref[...].astype(F32) * jnp.dot(yd_ref[...], wd_ref[...], preferred_element_type=F32)
         + g2_ref[...].astype(F32) * jnp.dot(y_conv, wc_ref[...], preferred_element_type=F32))
    o = jnp.dot(m.astype(BF16), wo_ref[...], preferred_element_type=F32)
    ms = jnp.mean(o * o, axis=-1, keepdims=True)
    o_ref[...] = x_ref[...] + o * lax.rsqrt(ms + RMS_EPS) * pg_ref[...]


def _merge(x2, y_fox, y_diff, pa, ps, pg, conv_w, w_fox, w_diff, w_conv, w_out, post_g, seq):
    t = x2.shape[0]
    tiles_per_seq = seq // TM_OUT
    halo_blocks = TM_OUT // SUBLANES

    def tile(col):
        return pl.BlockSpec((TM_OUT, WIDTH), lambda i: (i, col))

    def halo(col):
        return pl.BlockSpec((SUBLANES, WIDTH),
                            lambda i: (jnp.maximum(i * halo_blocks - 1, 0), col))

    def whole(shape):
        return pl.BlockSpec(shape, lambda i: (0, 0))

    return pl.pallas_call(
        functools.partial(_merge_kernel, tiles_per_seq=tiles_per_seq),
        grid=(t // TM_OUT,),
        in_specs=[
            pl.BlockSpec((TM_OUT, D_MODEL), lambda i: (i, 0)),
            pl.BlockSpec((TM_OUT, WIDTH), lambda i: (i, 0)),
            pl.BlockSpec((TM_OUT, WIDTH), lambda i: (i, 0)),
            tile(A_CB), tile(A_CC), tile(A_CX), tile(S_CG),
            halo(A_CC), halo(A_CX),
            pl.BlockSpec((TM_OUT, D_MODEL), lambda i: (i, 0)),
            pl.BlockSpec((TM_OUT, D_MODEL), lambda i: (i, 1)),
            pl.BlockSpec((TM_OUT, D_MODEL), lambda i: (i, 2)),
            whole((CONV_K, WIDTH)),
            whole((WIDTH, D_MODEL)), whole((WIDTH, D_MODEL)), whole((WIDTH, D_MODEL)),
            whole((D_MODEL, D_MODEL)),
            whole((1, D_MODEL)),
        ],
        out_specs=pl.BlockSpec((TM_OUT, D_MODEL), lambda i: (i, 0)),
        out_shape=jax.ShapeDtypeStruct((t, D_MODEL), F32),
        scratch_shapes=[pltpu.VMEM((TM_OUT + SUBLANES, WIDTH), F32)],
        compiler_params=pltpu.CompilerParams(
            dimension_semantics=("arbitrary",), vmem_limit_bytes=VMEM_LIMIT),
        name="merge",
    )(x2, y_fox, y_diff, pa, pa, pa, ps, pa, pa, pg, pg, pg,
      conv_w, w_fox, w_diff, w_conv, w_out, post_g)


def _rope_lane_tables(positions):
    inv_freq = ROPE_THETA ** (-jnp.arange(0, ROT_DIM, 2, dtype=F32) / ROT_DIM)
    ang = positions.astype(F32).reshape(-1, 1) * inv_freq
    cos, sin = jnp.cos(ang), jnp.sin(ang)
    t = ang.shape[0]
    ones = jnp.ones((t, HEAD_DIM - ROT_DIM), F32)
    zeros = jnp.zeros((t, HEAD_DIM - ROT_DIM), F32)
    c64 = jnp.concatenate([cos, cos, ones], axis=1)
    s64 = jnp.concatenate([-sin, sin, zeros], axis=1)
    rep = LANES // HEAD_DIM
    return jnp.tile(c64, (1, rep)), jnp.tile(s64, (1, rep))


def _gather_cols(w, offsets):
    return jnp.concatenate([w[:, o:o + WIDTH] for o in offsets], axis=1).astype(BF16)


def _scale_vec(n_blocks, scaled_block):
    blk = jnp.arange(n_blocks * WIDTH, dtype=jnp.int32) // WIDTH
    q_scale = HEAD_DIM ** -0.5 * LOG2E
    return jnp.where(blk == scaled_block, q_scale, 1.0).astype(F32).reshape(1, -1)


def kernel(x, positions, pre_norm_g, w_in, b_forget, b_merge, conv_w, lam_q1, lam_k1, lam_q2,
           lam_k2, diff_norm_g, w_br_fox, w_br_diff, w_br_conv, w_out, post_norm_g):
    batch, seq, _ = x.shape
    depth = w_in.shape[0]
    assert seq % TQ == 0 and TQ == TK == TS_FF and (batch * seq) % TM_IN == 0
    rc, rs = _rope_lane_tables(positions)
    x2 = x.reshape(batch * seq, D_MODEL)
    for l in range(depth):
        lam_init = 0.8 - 0.6 * math.exp(-0.3 * l)
        w = w_in[l]
        w_plain = _gather_cols(w, [_OFF_FQ, _OFF_FK, _OFF_FV, _OFF_DV, _OFF_CB, _OFF_CC, _OFF_CX])
        w_silu = _gather_cols(w, [_OFF_FG, _OFF_DG, _OFF_CG])
        w_rot = _gather_cols(w, [_OFF_DQ, _OFF_DK])
        w_gate = w[:, _OFF_MG:_OFF_MG + N_BRANCH * D_MODEL].astype(BF16)
        w_ff_t = w[:, _OFF_FF:_OFF_FF + FOX_HEADS].T.astype(BF16)
        pa, h = _plain_proj(x2, pre_norm_g[l].reshape(1, D_MODEL), w_plain,
                            _scale_vec(7, A_FQ), TN_PLAIN)
        ps = _epilogue_proj(_silu_proj_kernel, "proj_silu", h, w_silu, TN_SILU, [], [])
        pr = _epilogue_proj(_rot_proj_kernel, "proj_rotary", h, w_rot, TN_ROT,
                            [_scale_vec(2, R_DQ), rc, rs],
                            [_col_vec(TN_ROT), _ROPE_TILE, _ROPE_TILE])
        pg = _epilogue_proj(_gate_proj_kernel, "proj_gate", h, w_gate, TN_GATE,
                            [b_merge[l].reshape(1, N_BRANCH * D_MODEL)], [_col_vec(TN_GATE)])
        crow, ccol = _forget_cumsum(h, w_ff_t, b_forget[l].reshape(FOX_HEADS, 1), batch, seq)
        y_fox = _fox_attention(pa, ps, crow, ccol, batch, seq)
        lam_vecs = jnp.stack([lam_q1[l], lam_k1[l], lam_q2[l], lam_k2[l]])
        y_diff = _diff_attention(pr, pa, ps, lam_vecs, diff_norm_g[l].reshape(1, LANES),
                                 lam_init, batch, seq)
        x2 = _merge(x2, y_fox, y_diff, pa, ps, pg, conv_w[l],
                    w_br_fox[l].astype(BF16), w_br_diff[l].astype(BF16),
                    w_br_conv[l].astype(BF16), w_out[l].astype(BF16),
                    post_norm_g[l].reshape(1, D_MODEL), seq)
    return x2.reshape(batch, seq, D_MODEL)
```

```python
import functools
import math
from typing import Any, Callable, NamedTuple, Optional

import jax
import jax.numpy as jnp
from jax import lax
from jax.experimental import pallas as pl
from jax.experimental.pallas import tpu as pltpu

F32 = jnp.float32
BF16 = jnp.bfloat16

D_MODEL = 1024
HEAD_DIM = 64
FOX_HEADS = 8
DIFF_HEADS = 4
WIDTH = 512
CONV_K = 3
N_BRANCH = 3
ROPE_THETA = 500000.0
ROT_DIM = HEAD_DIM // 4
RMS_EPS = 1e-6
LANES = 128
SUBLANES = 8
LOG2E = math.log2(math.e)

(A_FQ, A_FK, A_FV, A_DV, A_CB, A_CC, A_CX) = range(7)
(S_FG, S_DG, S_CG) = range(3)
(R_DQ, R_DK) = range(2)
N_GATE_BLOCKS = N_BRANCH * D_MODEL // WIDTH

_OFF_FQ, _OFF_FK, _OFF_FV = 0, 512, 1024
_OFF_FF = 1536
_OFF_FG = 1544
_OFF_DQ, _OFF_DK, _OFF_DV, _OFF_DG = 2056, 2568, 3080, 3592
_OFF_CB, _OFF_CC, _OFF_CX, _OFF_CG = 4104, 4616, 5128, 5640
_OFF_MG = 6152

TM_IN = 1024
TN_PLAIN = 1792
TN_SILU = 1536
TN_ROT = 1024
TN_GATE = 3072
TS_FF = 512
TQ = 512
TK = 512
TM_OUT = 512
PAIR_ROWS = 2 * TQ
ATTN_VMEM_LIMIT = 58 * 1024 * 1024
VMEM_LIMIT = 48 * 1024 * 1024


def _silu(v):
    return v * jax.nn.sigmoid(v)


def _rotary_block(acc, c_head, s_head):
    half = ROT_DIM // 2
    heads_per_block = LANES // HEAD_DIM
    c = jnp.concatenate([c_head] * heads_per_block, axis=1)
    s = jnp.concatenate([s_head] * heads_per_block, axis=1)
    lane = lax.broadcasted_iota(jnp.int32, (acc.shape[0], LANES), 1)
    takes_upper = lax.bitwise_and(lane, ROT_DIM - 1) < half
    outs = []
    for i in range(acc.shape[1] // LANES):
        v = acc[:, i * LANES:(i + 1) * LANES]
        partner = jnp.where(takes_upper, pltpu.roll(v, LANES - half, 1), pltpu.roll(v, half, 1))
        outs.append(v * c + partner * s)
    return jnp.concatenate(outs, axis=1)


def _plain_proj_kernel(x_ref, g_ref, w_ref, sc_ref, o_ref, h_ref, h_scr):
    @pl.when(pl.program_id(1) == 0)
    def _():
        x = x_ref[...]
        ms = jnp.mean(x * x, axis=-1, keepdims=True)
        h = (x * lax.rsqrt(ms + RMS_EPS) * g_ref[...]).astype(BF16)
        h_scr[...] = h
        h_ref[...] = h

    acc = jnp.dot(h_scr[...], w_ref[...], preferred_element_type=F32)
    o_ref[...] = (acc * sc_ref[...]).astype(BF16)


def _silu_proj_kernel(h_ref, w_ref, o_ref):
    acc = jnp.dot(h_ref[...], w_ref[...], preferred_element_type=F32)
    o_ref[...] = _silu(acc).astype(BF16)


def _rot_proj_kernel(h_ref, w_ref, sc_ref, rc_ref, rs_ref, o_ref):
    acc = jnp.dot(h_ref[...], w_ref[...], preferred_element_type=F32)
    r = _rotary_block(acc, rc_ref[...], rs_ref[...])
    o_ref[...] = (r * sc_ref[...]).astype(BF16)


def _gate_proj_kernel(h_ref, w_ref, b_ref, o_ref):
    acc = jnp.dot(h_ref[...], w_ref[...], preferred_element_type=F32)
    o_ref[...] = jax.nn.sigmoid(acc + b_ref[...]).astype(BF16)


_PROJ_PARAMS = pltpu.CompilerParams(
    dimension_semantics=("arbitrary", "arbitrary"), vmem_limit_bytes=VMEM_LIMIT)
_ROW_TILE = pl.BlockSpec((TM_IN, D_MODEL), lambda i, j: (i, 0))
_ROPE_TILE = pl.BlockSpec((TM_IN, HEAD_DIM), lambda i, j: (i, 0))


def _w_block(tn):
    return pl.BlockSpec((D_MODEL, tn), lambda i, j: (0, j))


def _col_vec(tn):
    return pl.BlockSpec((1, tn), lambda i, j: (0, j))


def _out_block(tn):
    return pl.BlockSpec((TM_IN, tn), lambda i, j: (i, j))


def _plain_proj(x2, pre_g, w, col_scale, tn):
    t, n = x2.shape[0], w.shape[1]
    return pl.pallas_call(
        _plain_proj_kernel,
        grid=(t // TM_IN, n // tn),
        in_specs=[_ROW_TILE, pl.BlockSpec((1, D_MODEL), lambda i, j: (0, 0)),
                  _w_block(tn), _col_vec(tn)],
        out_specs=[_out_block(tn), _ROW_TILE],
        out_shape=[jax.ShapeDtypeStruct((t, n), BF16), jax.ShapeDtypeStruct((t, D_MODEL), BF16)],
        scratch_shapes=[pltpu.VMEM((TM_IN, D_MODEL), BF16)],
        compiler_params=_PROJ_PARAMS,
        name="proj_plain",
    )(x2, pre_g, w, col_scale)


def _epilogue_proj(body, name, h, w, tn, extra, extra_specs):
    t, n = h.shape[0], w.shape[1]
    return pl.pallas_call(
        body,
        grid=(t // TM_IN, n // tn),
        in_specs=[_ROW_TILE, _w_block(tn)] + extra_specs,
        out_specs=_out_block(tn),
        out_shape=jax.ShapeDtypeStruct((t, n), BF16),
        compiler_params=_PROJ_PARAMS,
        name=name,
    )(h, w, *extra)


def _split3(v):
    hi = v.astype(BF16)
    r1 = v - hi.astype(F32)
    mid = r1.astype(BF16)
    lo = (r1 - mid.astype(F32)).astype(BF16)
    return hi, mid, lo


def _forget_kernel(h_ref, wt_ref, bf_ref, crow_ref, ccol_ref, carry_scr):
    si = pl.program_id(1)

    @pl.when(si == 0)
    def _():
        carry_scr[...] = jnp.zeros_like(carry_scr)

    ff = lax.dot_general(wt_ref[...], h_ref[...], (((1,), (1,)), ((), ())),
                         preferred_element_type=F32)
    lf = jax.nn.log_sigmoid(ff + bf_ref[...])
    row = lax.broadcasted_iota(jnp.int32, (TS_FF, TS_FF), 0)
    col = lax.broadcasted_iota(jnp.int32, (TS_FF, TS_FF), 1)
    upper = (row <= col).astype(BF16)
    cum = jnp.zeros((FOX_HEADS, TS_FF), F32)
    for piece in _split3(lf):
        cum = cum + jnp.dot(piece, upper, preferred_element_type=F32)
    c = cum + carry_scr[:, 0:1]
    carry_scr[...] = jnp.broadcast_to(c[:, TS_FF - 1:TS_FF], carry_scr.shape)
    c2 = c * LOG2E
    crow_ref[0, 0] = c2
    padded = jnp.concatenate([c2, jnp.zeros((LANES - FOX_HEADS, TS_FF), F32)], axis=0)
    ccol_ref[...] = padded.T


def _forget_cumsum(h, w_ff_t, b_f, batch, seq):
    nt = seq // TS_FF
    return pl.pallas_call(
        _forget_kernel,
        grid=(batch, nt),
        in_specs=[
            pl.BlockSpec((TS_FF, D_MODEL), lambda b, s: (b * nt + s, 0)),
            pl.BlockSpec((FOX_HEADS, D_MODEL), lambda b, s: (0, 0)),
            pl.BlockSpec((FOX_HEADS, 1), lambda b, s: (0, 0)),
        ],
        out_specs=[
            pl.BlockSpec((1, 1, FOX_HEADS, TS_FF), lambda b, s: (b, s, 0, 0)),
            pl.BlockSpec((TS_FF, LANES), lambda b, s: (b * nt + s, 0)),
        ],
        out_shape=[
            jax.ShapeDtypeStruct((batch, nt, FOX_HEADS, TS_FF), F32),
            jax.ShapeDtypeStruct((batch * seq, LANES), F32),
        ],
        scratch_shapes=[pltpu.VMEM((FOX_HEADS, LANES), F32)],
        compiler_params=pltpu.CompilerParams(
            dimension_semantics=("arbitrary", "arbitrary"), vmem_limit_bytes=VMEM_LIMIT),
        name="forget_cumsum",
    )(h, w_ff_t, b_f)


class _MapPair(NamedTuple):
    blk: int
    key_bias: Optional[Callable[[Any], Any]]
    row_shift: Any


def _attention_maps(maps, qi, qm_scr, k_ref, v_ref, bufs, mb_scr, acc_scr, ls_scr, emit):
    mxu_row_sums = ls_scr is None
    row = lax.broadcasted_iota(jnp.int32, (TQ, TK), 0)
    col = lax.broadcasted_iota(jnp.int32, (TQ, TK), 1)
    causal = col <= row
    n_lane_blocks = TK // LANES

    def lanes(v, c):
        return v[:, c * LANES:(c + 1) * LANES]

    def chunk_rows(ref, mp, ki):
        start = pl.multiple_of(ki * TK, TK)
        return ref[pl.ds(start, TK), mp.blk * LANES:(mp.blk + 1) * LANES]

    def pass1_begin(buf):
        bufs[buf][1][...] = jnp.full((PAIR_ROWS, LANES), -jnp.inf, F32)

    def pass1_chunk(mp, buf, ki, diagonal):
        t_scr, mx_scr = bufs[buf]
        s2 = lax.dot_general(qm_scr[mp.blk], chunk_rows(k_ref, mp, ki),
                             (((1,), (1,)), ((), ())), preferred_element_type=F32)
        biases = mp.key_bias(ki) if mp.key_bias is not None else (None, None)
        for half, bias in enumerate(biases):
            rows = slice(half * TQ, (half + 1) * TQ)
            s = s2[rows, :]
            if bias is not None:
                s = s - bias
            if diagonal:
                s = jnp.where(causal, s, -jnp.inf)
            t_scr[ki, rows, :] = s
            m = mx_scr[rows, :]
            for c in range(n_lane_blocks):
                m = jnp.maximum(m, lanes(s, c))
            mx_scr[rows, :] = m

    def pass2_begin(mp, buf):
        m = jnp.max(bufs[buf][1][...], axis=1, keepdims=True)
        if mp.row_shift is not None:
            m = (m + mp.row_shift) - mp.row_shift
        mb_scr[...] = jnp.broadcast_to(m, (PAIR_ROWS, LANES))
        acc_scr[...] = jnp.zeros(acc_scr.shape, F32)
        if not mxu_row_sums:
            ls_scr[...] = jnp.zeros((PAIR_ROWS, LANES), F32)

    def pass2_chunk(mp, buf, ki):
        t = bufs[buf][0][ki]
        mb = mb_scr[...]
        ps = [jnp.exp2(lanes(t, c) - mb) for c in range(n_lane_blocks)]
        v = chunk_rows(v_ref, mp, ki)
        if mxu_row_sums:
            v = jnp.concatenate([v, jnp.ones((TK, LANES), BF16)], axis=1)
        else:
            ls = ls_scr[...]
            for p in ps:
                ls = ls + p
            ls_scr[...] = ls
        p = jnp.concatenate(ps, axis=1).astype(BF16)
        acc_scr[...] += jnp.dot(p, v, preferred_element_type=F32)

    def pass2_end(n):
        acc = acc_scr[...]
        if mxu_row_sums:
            l = acc[:, LANES:LANES + 1]
        else:
            l = jnp.sum(ls_scr[...], axis=1, keepdims=True)
        emit(n, acc[:, 0:LANES] / l)

    def loop(n_chunks, body):
        def step(i, carry):
            body(2 * i)
            body(2 * i + 1)
            return carry
        lax.fori_loop(0, lax.shift_right_logical(n_chunks, 1), step, 0)

        @pl.when(lax.bitwise_and(n_chunks, 1) == 1)
        def _():
            body(n_chunks - 1)

    pass1_begin(0)
    pass1_chunk(maps[0], 0, qi, True)
    loop(qi, lambda ki: pass1_chunk(maps[0], 0, ki, False))
    for n in range(1, len(maps)):
        buf, prev = n % 2, (n - 1) % 2
        pass2_begin(maps[n - 1], prev)
        pass1_begin(buf)
        pass1_chunk(maps[n], buf, qi, True)

        def both(ki, n=n, buf=buf, prev=prev):
            pass2_chunk(maps[n - 1], prev, ki)
            pass1_chunk(maps[n], buf, ki, False)

        loop(qi, both)
        pass2_chunk(maps[n - 1], prev, qi)
        pass2_end(n - 1)
    last = len(maps) - 1
    pass2_begin(maps[last], last % 2)
    loop(qi + 1, lambda ki: pass2_chunk(maps[last], last % 2, ki))
    pass2_end(last)


def _store_masked_queries(q_ref, qm_scr):
    first = lax.broadcasted_iota(jnp.int32, (TQ, LANES), 1) < HEAD_DIM
    for b in range(WIDTH // LANES):
        q = q_ref[:, b * LANES:(b + 1) * LANES]
        zero = jnp.zeros_like(q)
        qm_scr[b, 0:TQ, :] = jnp.where(first, q, zero)
        qm_scr[b, TQ:PAIR_ROWS, :] = jnp.where(first, zero, q)


def _fox_kernel(q_ref, k_ref, v_ref, g_ref, crow_ref, ccol_ref, o_ref,
                qm_scr, ta_scr, tb_scr, mxa_scr, mxb_scr, mb_scr, acc_scr):
    qi = pl.program_id(1)
    first = lax.broadcasted_iota(jnp.int32, (TQ, LANES), 1) < HEAD_DIM
    _store_masked_queries(q_ref, qm_scr)
    ccol = ccol_ref[...]
    maps = [
        _MapPair(blk=b,
                 key_bias=lambda ki, b=b: (crow_ref[0, ki, 2 * b:2 * b + 1, :],
                                           crow_ref[0, ki, 2 * b + 1:2 * b + 2, :]),
                 row_shift=jnp.concatenate([ccol[:, 2 * b:2 * b + 1],
                                            ccol[:, 2 * b + 1:2 * b + 2]], axis=0))
        for b in range(FOX_HEADS // 2)
    ]

    def emit(b, o):
        blk = slice(b * LANES, (b + 1) * LANES)
        y = jnp.where(first, o[0:TQ, :], o[TQ:PAIR_ROWS, :]) * g_ref[:, blk].astype(F32)
        o_ref[:, blk] = y.astype(BF16)

    _attention_maps(maps, qi, qm_scr, k_ref, v_ref, ((ta_scr, mxa_scr), (tb_scr, mxb_scr)),
                    mb_scr, acc_scr, None, emit)


def _diff_kernel(q_ref, k_ref, v_ref, g_ref, lam_ref, ng_ref, o_ref,
                 qm_scr, ta_scr, tb_scr, mxa_scr, mxb_scr, mb_scr, acc_scr, ls_scr,
                 *, lam_init):
    qi = pl.program_id(1)
    _store_masked_queries(q_ref, qm_scr)
    lv = lam_ref[...]
    lam = (jnp.exp(jnp.sum(lv[0:1] * lv[1:2], axis=1, keepdims=True))
           - jnp.exp(jnp.sum(lv[2:3] * lv[3:4], axis=1, keepdims=True)) + lam_init)
    maps = [_MapPair(blk=h, key_bias=None, row_shift=None) for h in range(DIFF_HEADS)]

    def emit(h, o):
        blk = slice(h * LANES, (h + 1) * LANES)
        d = o[0:TQ, :] - lam * o[TQ:PAIR_ROWS, :]
        ms = jnp.mean(d * d, axis=-1, keepdims=True)
        y = d * lax.rsqrt(ms + RMS_EPS) * ng_ref[...] * (1.0 - lam_init)
        o_ref[:, blk] = (y * g_ref[:, blk].astype(F32)).astype(BF16)

    _attention_maps(maps, qi, qm_scr, k_ref, v_ref, ((ta_scr, mxa_scr), (tb_scr, mxb_scr)),
                    mb_scr, acc_scr, ls_scr, emit)


def _attn_scratch(seq, mxu_row_sums):
    stat = pltpu.VMEM((PAIR_ROWS, LANES), F32)
    logits = pltpu.VMEM((seq // TK, PAIR_ROWS, TK), F32)
    common = [pltpu.VMEM((WIDTH // LANES, PAIR_ROWS, LANES), BF16),
              logits, logits, stat, stat,
              stat]
    if mxu_row_sums:
        return common + [pltpu.VMEM((PAIR_ROWS, 2 * LANES), F32)]
    return common + [stat, stat]


def _per_sequence(block_shape, col):
    return pl.BlockSpec(block_shape, lambda b, i: (b, col))


def _fox_attention(pa, ps, crow, ccol, batch, seq):
    nq = seq // TQ
    return pl.pallas_call(
        _fox_kernel,
        grid=(batch, nq),
        in_specs=[
            pl.BlockSpec((TQ, WIDTH), lambda b, i: (b * nq + i, A_FQ)),
            _per_sequence((seq, WIDTH), A_FK),
            _per_sequence((seq, WIDTH), A_FV),
            pl.BlockSpec((TQ, WIDTH), lambda b, i: (b * nq + i, S_FG)),
            pl.BlockSpec((1, seq // TK, FOX_HEADS, TK), lambda b, i: (b, 0, 0, 0)),
            pl.BlockSpec((TQ, LANES), lambda b, i: (b * nq + i, 0)),
        ],
        out_specs=pl.BlockSpec((TQ, WIDTH), lambda b, i: (b * nq + i, 0)),
        out_shape=jax.ShapeDtypeStruct((batch * seq, WIDTH), BF16),
        scratch_shapes=_attn_scratch(seq, mxu_row_sums=True),
        compiler_params=pltpu.CompilerParams(
            dimension_semantics=("arbitrary", "arbitrary"), vmem_limit_bytes=ATTN_VMEM_LIMIT),
        name="fox_attention",
    )(pa, pa, pa, ps, crow, ccol)


def _diff_attention(pr, pa, ps, lam_vecs, norm_g, lam_init, batch, seq):
    nq = seq // TQ
    return pl.pallas_call(
        functools.partial(_diff_kernel, lam_init=lam_init),
        grid=(batch, nq),
        in_specs=[
            pl.BlockSpec((TQ, WIDTH), lambda b, i: (b * nq + i, R_DQ)),
            _per_sequence((seq, WIDTH), R_DK),
            _per_sequence((seq, WIDTH), A_DV),
            pl.BlockSpec((TQ, WIDTH), lambda b, i: (b * nq + i, S_DG)),
            pl.BlockSpec((4, HEAD_DIM), lambda b, i: (0, 0)),
            pl.BlockSpec((1, LANES), lambda b, i: (0, 0)),
        ],
        out_specs=pl.BlockSpec((TQ, WIDTH), lambda b, i: (b * nq + i, 0)),
        out_shape=jax.ShapeDtypeStruct((batch * seq, WIDTH), BF16),
        scratch_shapes=_attn_scratch(seq, mxu_row_sums=False),
        compiler_params=pltpu.CompilerParams(
            dimension_semantics=("arbitrary", "arbitrary"), vmem_limit_bytes=ATTN_VMEM_LIMIT),
        name="diff_attention",
    )(pr, pr, pa, ps, lam_vecs, norm_g)


def _merge_kernel(x_ref, yf_ref, yd_ref, cb_ref, cc_ref, cx_ref, cg_ref, hc_ref, hx_ref,
                  g0_ref, g1_ref, g2_ref, cw_ref, wf_ref, wd_ref, wc_ref, wo_ref, pg_ref,
                  o_ref, ext_scr, *, tiles_per_seq):
    i = pl.program_id(0)
    halo = SUBLANES
    u = cc_ref[...].astype(F32) * cx_ref[...].astype(F32)
    prev = hc_ref[...].astype(F32) * hx_ref[...].astype(F32)
    prev = jnp.where(i % tiles_per_seq == 0, jnp.zeros_like(prev), prev)
    ext_scr[0:halo, :] = prev
    ext_scr[halo:halo + TM_OUT, :] = u
    cw = cw_ref[...]
    conv = (cw[2:3] * u + cw[1:2] * ext_scr[halo - 1:halo - 1 + TM_OUT, :]
            + cw[0:1] * ext_scr[halo - 2:halo - 2 + TM_OUT, :])
    y_conv = (cb_ref[...].astype(F32) * conv * cg_ref[...].astype(F32)).astype(BF16)

    m = (g0_ref[...].astype(F32) * jnp.dot(yf_ref[...], wf_ref[...], preferred_element_type=F32)
         + g1_ref[...].astype(F32) * jnp.dot(yd_ref[...], wd_ref[...], preferred_element_type=F32)
         + g2_ref[...].astype(F32) * jnp.dot(y_conv, wc_ref[...], preferred_element_type=F32))
    o = jnp.dot(m.astype(BF16), wo_ref[...], preferred_element_type=F32)
    ms = jnp.mean(o * o, axis=-1, keepdims=True)
    o_ref[...] = x_ref[...] + o * lax.rsqrt(ms + RMS_EPS) * pg_ref[...]


def _merge(x2, y_fox, y_diff, pa, ps, pg, conv_w, w_fox, w_diff, w_conv, w_out, post_g, seq):
    t = x2.shape[0]
    tiles_per_seq = seq // TM_OUT
    halo_blocks = TM_OUT // SUBLANES

    def tile(col):
        return pl.BlockSpec((TM_OUT, WIDTH), lambda i: (i, col))

    def halo(col):
        return pl.BlockSpec((SUBLANES, WIDTH),
                            lambda i: (jnp.maximum(i * halo_blocks - 1, 0), col))

    def whole(shape):
        return pl.BlockSpec(shape, lambda i: (0, 0))

    return pl.pallas_call(
        functools.partial(_merge_kernel, tiles_per_seq=tiles_per_seq),
        grid=(t // TM_OUT,),
        in_specs=[
            pl.BlockSpec((TM_OUT, D_MODEL), lambda i: (i, 0)),
            pl.BlockSpec((TM_OUT, WIDTH), lambda i: (i, 0)),
            pl.BlockSpec((TM_OUT, WIDTH), lambda i: (i, 0)),
            tile(A_CB), tile(A_CC), tile(A_CX), tile(S_CG),
            halo(A_CC), halo(A_CX),
            pl.BlockSpec((TM_OUT, D_MODEL), lambda i: (i, 0)),
            pl.BlockSpec((TM_OUT, D_MODEL), lambda i: (i, 1)),
            pl.BlockSpec((TM_OUT, D_MODEL), lambda i: (i, 2)),
            whole((CONV_K, WIDTH)),
            whole((WIDTH, D_MODEL)), whole((WIDTH, D_MODEL)), whole((WIDTH, D_MODEL)),
            whole((D_MODEL, D_MODEL)),
            whole((1, D_MODEL)),
        ],
        out_specs=pl.BlockSpec((TM_OUT, D_MODEL), lambda i: (i, 0)),
        out_shape=jax.ShapeDtypeStruct((t, D_MODEL), F32),
        scratch_shapes=[pltpu.VMEM((TM_OUT + SUBLANES, WIDTH), F32)],
        compiler_params=pltpu.CompilerParams(
            dimension_semantics=("arbitrary",), vmem_limit_bytes=VMEM_LIMIT),
        name="merge",
    )(x2, y_fox, y_diff, pa, pa, pa, ps, pa, pa, pg, pg, pg,
      conv_w, w_fox, w_diff, w_conv, w_out, post_g)


def _rope_lane_tables(positions):
    inv_freq = ROPE_THETA ** (-jnp.arange(0, ROT_DIM, 2, dtype=F32) / ROT_DIM)
    ang = positions.astype(F32).reshape(-1, 1) * inv_freq
    cos, sin = jnp.cos(ang), jnp.sin(ang)
    t = ang.shape[0]
    ones = jnp.ones((t, HEAD_DIM - ROT_DIM), F32)
    zeros = jnp.zeros((t, HEAD_DIM - ROT_DIM), F32)
    c64 = jnp.concatenate([cos, cos, ones], axis=1)
    s64 = jnp.concatenate([-sin, sin, zeros], axis=1)
    return c64, s64


def _gather_cols(w, offsets):
    return jnp.concatenate([w[:, o:o + WIDTH] for o in offsets], axis=1).astype(BF16)


def _scale_vec(n_blocks, scaled_block):
    blk = jnp.arange(n_blocks * WIDTH, dtype=jnp.int32) // WIDTH
    q_scale = HEAD_DIM ** -0.5 * LOG2E
    return jnp.where(blk == scaled_block, q_scale, 1.0).astype(F32).reshape(1, -1)


def kernel(x, positions, pre_norm_g, w_in, b_forget, b_merge, conv_w, lam_q1, lam_k1, lam_q2,
           lam_k2, diff_norm_g, w_br_fox, w_br_diff, w_br_conv, w_out, post_norm_g):
    batch, seq, _ = x.shape
    depth = w_in.shape[0]
    assert seq % TQ == 0 and TQ == TK == TS_FF and (batch * seq) % TM_IN == 0
    rc, rs = _rope_lane_tables(positions)
    x2 = x.reshape(batch * seq, D_MODEL)
    for l in range(depth):
        lam_init = 0.8 - 0.6 * math.exp(-0.3 * l)
        w = w_in[l]
        w_plain = _gather_cols(w, [_OFF_FQ, _OFF_FK, _OFF_FV, _OFF_DV, _OFF_CB, _OFF_CC, _OFF_CX])
        w_silu = _gather_cols(w, [_OFF_FG, _OFF_DG, _OFF_CG])
        w_rot = _gather_cols(w, [_OFF_DQ, _OFF_DK])
        w_gate = w[:, _OFF_MG:_OFF_MG + N_BRANCH * D_MODEL].astype(BF16)
        w_ff_t = w[:, _OFF_FF:_OFF_FF + FOX_HEADS].T.astype(BF16)
        pa, h = _plain_proj(x2, pre_norm_g[l].reshape(1, D_MODEL), w_plain,
                            _scale_vec(7, A_FQ), TN_PLAIN)
        ps = _epilogue_proj(_silu_proj_kernel, "proj_silu", h, w_silu, TN_SILU, [], [])
        pr = _epilogue_proj(_rot_proj_kernel, "proj_rotary", h, w_rot, TN_ROT,
                            [_scale_vec(2, R_DQ), rc, rs],
                            [_col_vec(TN_ROT), _ROPE_TILE, _ROPE_TILE])
        pg = _epilogue_proj(_gate_proj_kernel, "proj_gate", h, w_gate, TN_GATE,
                            [b_merge[l].reshape(1, N_BRANCH * D_MODEL)], [_col_vec(TN_GATE)])
        crow, ccol = _forget_cumsum(h, w_ff_t, b_forget[l].reshape(FOX_HEADS, 1), batch, seq)
        y_fox = _fox_attention(pa, ps, crow, ccol, batch, seq)
        lam_vecs = jnp.stack([lam_q1[l], lam_k1[l], lam_q2[l], lam_k2[l]])
        y_diff = _diff_attention(pr, pa, ps, lam_vecs, diff_norm_g[l].reshape(1, LANES),
                                 lam_init, batch, seq)
        x2 = _merge(x2, y_fox, y_diff, pa, ps, pg, conv_w[l],
                    w_br_fox[l].astype(BF16), w_br_diff[l].astype(BF16),
                    w_br_conv[l].astype(BF16), w_out[l].astype(BF16),
                    post_norm_g[l].reshape(1, D_MODEL), seq)
    return x2.reshape(batch, seq, D_MODEL)
```

```python
import functools
import math
from typing import Any, Callable, NamedTuple, Optional

import jax
import jax.numpy as jnp
from jax import lax
from jax.experimental import pallas as pl
from jax.experimental.pallas import tpu as pltpu

F32 = jnp.float32
BF16 = jnp.bfloat16

D_MODEL = 1024
HEAD_DIM = 64
FOX_HEADS = 8
DIFF_HEADS = 4
WIDTH = 512
CONV_K = 3
N_BRANCH = 3
ROPE_THETA = 500000.0
ROT_DIM = HEAD_DIM // 4
RMS_EPS = 1e-6
LANES = 128
SUBLANES = 8
LOG2E = math.log2(math.e)

(A_FQ, A_FK, A_FV, A_DV, A_CB, A_CC, A_CX) = range(7)
(S_FG, S_DG, S_CG) = range(3)
(R_DQ, R_DK) = range(2)
N_GATE_BLOCKS = N_BRANCH * D_MODEL // WIDTH

_OFF_FQ, _OFF_FK, _OFF_FV = 0, 512, 1024
_OFF_FF = 1536
_OFF_FG = 1544
_OFF_DQ, _OFF_DK, _OFF_DV, _OFF_DG = 2056, 2568, 3080, 3592
_OFF_CB, _OFF_CC, _OFF_CX, _OFF_CG = 4104, 4616, 5128, 5640
_OFF_MG = 6152

TM_IN = 1024
TN_PLAIN = 1792
TN_SILU = 1536
TN_ROT = 1024
TN_GATE = 3072
TS_FF = 512
TQ = 512
TK = 512
TM_OUT = 1024
PAIR_ROWS = 2 * TQ
ATTN_VMEM_LIMIT = 58 * 1024 * 1024
VMEM_LIMIT = 48 * 1024 * 1024


def _silu(v):
    return v * jax.nn.sigmoid(v)


def _rotary_block(acc, c, s):
    half = ROT_DIM // 2
    lane = lax.broadcasted_iota(jnp.int32, (acc.shape[0], LANES), 1)
    takes_upper = lax.bitwise_and(lane, ROT_DIM - 1) < half
    outs = []
    for i in range(acc.shape[1] // LANES):
        v = acc[:, i * LANES:(i + 1) * LANES]
        partner = jnp.where(takes_upper, pltpu.roll(v, LANES - half, 1), pltpu.roll(v, half, 1))
        outs.append(v * c + partner * s)
    return jnp.concatenate(outs, axis=1)


def _plain_proj_kernel(x_ref, g_ref, w_ref, sc_ref, o_ref, h_ref, h_scr):
    @pl.when(pl.program_id(1) == 0)
    def _():
        x = x_ref[...]
        ms = jnp.mean(x * x, axis=-1, keepdims=True)
        h = (x * lax.rsqrt(ms + RMS_EPS) * g_ref[...]).astype(BF16)
        h_scr[...] = h
        h_ref[...] = h

    acc = jnp.dot(h_scr[...], w_ref[...], preferred_element_type=F32)
    o_ref[...] = (acc * sc_ref[...]).astype(BF16)


def _silu_proj_kernel(h_ref, w_ref, o_ref):
    acc = jnp.dot(h_ref[...], w_ref[...], preferred_element_type=F32)
    o_ref[...] = _silu(acc).astype(BF16)


def _rot_proj_kernel(h_ref, w_ref, sc_ref, rc_ref, rs_ref, o_ref):
    acc = jnp.dot(h_ref[...], w_ref[...], preferred_element_type=F32)
    r = _rotary_block(acc, rc_ref[...], rs_ref[...])
    o_ref[...] = (r * sc_ref[...]).astype(BF16)


def _gate_proj_kernel(h_ref, w_ref, b_ref, o_ref):
    acc = jnp.dot(h_ref[...], w_ref[...], preferred_element_type=F32)
    o_ref[...] = jax.nn.sigmoid(acc + b_ref[...]).astype(BF16)


_PROJ_PARAMS = pltpu.CompilerParams(
    dimension_semantics=("arbitrary", "arbitrary"), vmem_limit_bytes=VMEM_LIMIT)
_ROW_TILE = pl.BlockSpec((TM_IN, D_MODEL), lambda i, j: (i, 0))
_ROPE_TILE = pl.BlockSpec((TM_IN, LANES), lambda i, j: (i, 0))


def _w_block(tn):
    return pl.BlockSpec((D_MODEL, tn), lambda i, j: (0, j))


def _col_vec(tn):
    return pl.BlockSpec((1, tn), lambda i, j: (0, j))


def _out_block(tn):
    return pl.BlockSpec((TM_IN, tn), lambda i, j: (i, j))


def _plain_proj(x2, pre_g, w, col_scale, tn):
    t, n = x2.shape[0], w.shape[1]
    return pl.pallas_call(
        _plain_proj_kernel,
        grid=(t // TM_IN, n // tn),
        in_specs=[_ROW_TILE, pl.BlockSpec((1, D_MODEL), lambda i, j: (0, 0)),
                  _w_block(tn), _col_vec(tn)],
        out_specs=[_out_block(tn), _ROW_TILE],
        out_shape=[jax.ShapeDtypeStruct((t, n), BF16), jax.ShapeDtypeStruct((t, D_MODEL), BF16)],
        scratch_shapes=[pltpu.VMEM((TM_IN, D_MODEL), BF16)],
        compiler_params=_PROJ_PARAMS,
        name="proj_plain",
    )(x2, pre_g, w, col_scale)


def _epilogue_proj(body, name, h, w, tn, extra, extra_specs):
    t, n = h.shape[0], w.shape[1]
    return pl.pallas_call(
        body,
        grid=(t // TM_IN, n // tn),
        in_specs=[_ROW_TILE, _w_block(tn)] + extra_specs,
        out_specs=_out_block(tn),
        out_shape=jax.ShapeDtypeStruct((t, n), BF16),
        compiler_params=_PROJ_PARAMS,
        name=name,
    )(h, w, *extra)


def _split3(v):
    hi = v.astype(BF16)
    r1 = v - hi.astype(F32)
    mid = r1.astype(BF16)
    lo = (r1 - mid.astype(F32)).astype(BF16)
    return hi, mid, lo


def _forget_kernel(h_ref, wt_ref, bf_ref, crow_ref, ccol_ref, carry_scr):
    si = pl.program_id(1)

    @pl.when(si == 0)
    def _():
        carry_scr[...] = jnp.zeros_like(carry_scr)

    ff = lax.dot_general(wt_ref[...], h_ref[...], (((1,), (1,)), ((), ())),
                         preferred_element_type=F32)
    lf = jax.nn.log_sigmoid(ff + bf_ref[...])
    row = lax.broadcasted_iota(jnp.int32, (TS_FF, TS_FF), 0)
    col = lax.broadcasted_iota(jnp.int32, (TS_FF, TS_FF), 1)
    upper = (row <= col).astype(BF16)
    cum = jnp.zeros((FOX_HEADS, TS_FF), F32)
    for piece in _split3(lf):
        cum = cum + jnp.dot(piece, upper, preferred_element_type=F32)
    c = cum + carry_scr[:, 0:1]
    carry_scr[...] = jnp.broadcast_to(c[:, TS_FF - 1:TS_FF], carry_scr.shape)
    c2 = c * LOG2E
    crow_ref[0, 0] = c2
    padded = jnp.concatenate([c2, jnp.zeros((LANES - FOX_HEADS, TS_FF), F32)], axis=0)
    ccol_ref[...] = padded.T


def _forget_cumsum(h, w_ff_t, b_f, batch, seq):
    nt = seq // TS_FF
    return pl.pallas_call(
        _forget_kernel,
        grid=(batch, nt),
        in_specs=[
            pl.BlockSpec((TS_FF, D_MODEL), lambda b, s: (b * nt + s, 0)),
            pl.BlockSpec((FOX_HEADS, D_MODEL), lambda b, s: (0, 0)),
            pl.BlockSpec((FOX_HEADS, 1), lambda b, s: (0, 0)),
        ],
        out_specs=[
            pl.BlockSpec((1, 1, FOX_HEADS, TS_FF), lambda b, s: (b, s, 0, 0)),
            pl.BlockSpec((TS_FF, LANES), lambda b, s: (b * nt + s, 0)),
        ],
        out_shape=[
            jax.ShapeDtypeStruct((batch, nt, FOX_HEADS, TS_FF), F32),
            jax.ShapeDtypeStruct((batch * seq, LANES), F32),
        ],
        scratch_shapes=[pltpu.VMEM((FOX_HEADS, LANES), F32)],
        compiler_params=pltpu.CompilerParams(
            dimension_semantics=("arbitrary", "arbitrary"), vmem_limit_bytes=VMEM_LIMIT),
        name="forget_cumsum",
    )(h, w_ff_t, b_f)


class _MapPair(NamedTuple):
    blk: int
    key_bias: Optional[Callable[[Any], Any]]
    row_shift: Any


def _attention_maps(maps, qi, qm_scr, k_ref, v_ref, bufs, mb_scr, acc_scr, ls_scr, emit):
    mxu_row_sums = ls_scr is None
    row = lax.broadcasted_iota(jnp.int32, (TQ, TK), 0)
    col = lax.broadcasted_iota(jnp.int32, (TQ, TK), 1)
    causal = col <= row
    n_lane_blocks = TK // LANES

    def lanes(v, c):
        return v[:, c * LANES:(c + 1) * LANES]

    def chunk_rows(ref, mp, ki):
        start = pl.multiple_of(ki * TK, TK)
        return ref[pl.ds(start, TK), mp.blk * LANES:(mp.blk + 1) * LANES]

    def pass1_begin(buf):
        bufs[buf][1][...] = jnp.full((PAIR_ROWS, LANES), -jnp.inf, F32)

    def pass1_chunk(mp, buf, ki, diagonal):
        t_scr, mx_scr = bufs[buf]
        s2 = lax.dot_general(qm_scr[mp.blk], chunk_rows(k_ref, mp, ki),
                             (((1,), (1,)), ((), ())), preferred_element_type=F32)
        biases = mp.key_bias(ki) if mp.key_bias is not None else (None, None)
        for half, bias in enumerate(biases):
            rows = slice(half * TQ, (half + 1) * TQ)
            s = s2[rows, :]
            if bias is not None:
                s = s - bias
            if diagonal:
                s = jnp.where(causal, s, -jnp.inf)
            t_scr[ki, rows, :] = s
            m = mx_scr[rows, :]
            for c in range(n_lane_blocks):
                m = jnp.maximum(m, lanes(s, c))
            mx_scr[rows, :] = m

    def pass2_begin(mp, buf):
        m = jnp.max(bufs[buf][1][...], axis=1, keepdims=True)
        if mp.row_shift is not None:
            m = (m + mp.row_shift) - mp.row_shift
        mb_scr[...] = jnp.broadcast_to(m, (PAIR_ROWS, LANES))
        acc_scr[...] = jnp.zeros(acc_scr.shape, F32)
        if not mxu_row_sums:
            ls_scr[...] = jnp.zeros((PAIR_ROWS, LANES), F32)

    def pass2_chunk(mp, buf, ki):
        t = bufs[buf][0][ki]
        mb = mb_scr[...]
        ps = [jnp.exp2(lanes(t, c) - mb) for c in range(n_lane_blocks)]
        v = chunk_rows(v_ref, mp, ki)
        if mxu_row_sums:
            v = jnp.concatenate([v, jnp.ones((TK, LANES), BF16)], axis=1)
        else:
            ls = ls_scr[...]
            for p in ps:
                ls = ls + p
            ls_scr[...] = ls
        p = jnp.concatenate(ps, axis=1).astype(BF16)
        acc_scr[...] += jnp.dot(p, v, preferred_element_type=F32)

    def pass2_end(n):
        acc = acc_scr[...]
        if mxu_row_sums:
            l = acc[:, LANES:LANES + 1]
        else:
            l = jnp.sum(ls_scr[...], axis=1, keepdims=True)
        emit(n, acc[:, 0:LANES] / l)

    def loop(n_chunks, body):
        def step(i, carry):
            body(2 * i)
            body(2 * i + 1)
            return carry
        lax.fori_loop(0, lax.shift_right_logical(n_chunks, 1), step, 0)

        @pl.when(lax.bitwise_and(n_chunks, 1) == 1)
        def _():
            body(n_chunks - 1)

    pass1_begin(0)
    pass1_chunk(maps[0], 0, qi, True)
    loop(qi, lambda ki: pass1_chunk(maps[0], 0, ki, False))
    for n in range(1, len(maps)):
        buf, prev = n % 2, (n - 1) % 2
        pass2_begin(maps[n - 1], prev)
        pass1_begin(buf)
        pass1_chunk(maps[n], buf, qi, True)

        def both(ki, n=n, buf=buf, prev=prev):
            pass2_chunk(maps[n - 1], prev, ki)
            pass1_chunk(maps[n], buf, ki, False)

        loop(qi, both)
        pass2_chunk(maps[n - 1], prev, qi)
        pass2_end(n - 1)
    last = len(maps) - 1
    pass2_begin(maps[last], last % 2)
    loop(qi + 1, lambda ki: pass2_chunk(maps[last], last % 2, ki))
    pass2_end(last)


def _store_masked_queries(q_ref, qm_scr):
    first = lax.broadcasted_iota(jnp.int32, (TQ, LANES), 1) < HEAD_DIM
    for b in range(WIDTH // LANES):
        q = q_ref[:, b * LANES:(b + 1) * LANES]
        zero = jnp.zeros_like(q)
        qm_scr[b, 0:TQ, :] = jnp.where(first, q, zero)
        qm_scr[b, TQ:PAIR_ROWS, :] = jnp.where(first, zero, q)


def _fox_kernel(q_ref, k_ref, v_ref, g_ref, crow_ref, ccol_ref, o_ref,
                qm_scr, ta_scr, tb_scr, mxa_scr, mxb_scr, mb_scr, acc_scr):
    qi = pl.program_id(1)
    first = lax.broadcasted_iota(jnp.int32, (TQ, LANES), 1) < HEAD_DIM
    _store_masked_queries(q_ref, qm_scr)
    ccol = ccol_ref[...]
    maps = [
        _MapPair(blk=b,
                 key_bias=lambda ki, b=b: (crow_ref[0, ki, 2 * b:2 * b + 1, :],
                                           crow_ref[0, ki, 2 * b + 1:2 * b + 2, :]),
                 row_shift=jnp.concatenate([ccol[:, 2 * b:2 * b + 1],
                                            ccol[:, 2 * b + 1:2 * b + 2]], axis=0))
        for b in range(FOX_HEADS // 2)
    ]

    def emit(b, o):
        blk = slice(b * LANES, (b + 1) * LANES)
        y = jnp.where(first, o[0:TQ, :], o[TQ:PAIR_ROWS, :]) * g_ref[:, blk].astype(F32)
        o_ref[:, blk] = y.astype(BF16)

    _attention_maps(maps, qi, qm_scr, k_ref, v_ref, ((ta_scr, mxa_scr), (tb_scr, mxb_scr)),
                    mb_scr, acc_scr, None, emit)


def _diff_kernel(q_ref, k_ref, v_ref, g_ref, lam_ref, ng_ref, o_ref,
                 qm_scr, ta_scr, tb_scr, mxa_scr, mxb_scr, mb_scr, acc_scr, ls_scr,
                 *, lam_init):
    qi = pl.program_id(1)
    _store_masked_queries(q_ref, qm_scr)
    lv = lam_ref[...]
    lam = (jnp.exp(jnp.sum(lv[0:1] * lv[1:2], axis=1, keepdims=True))
           - jnp.exp(jnp.sum(lv[2:3] * lv[3:4], axis=1, keepdims=True)) + lam_init)
    maps = [_MapPair(blk=h, key_bias=None, row_shift=None) for h in range(DIFF_HEADS)]

    def emit(h, o):
        blk = slice(h * LANES, (h + 1) * LANES)
        d = o[0:TQ, :] - lam * o[TQ:PAIR_ROWS, :]
        ms = jnp.mean(d * d, axis=-1, keepdims=True)
        y = d * lax.rsqrt(ms + RMS_EPS) * ng_ref[...] * (1.0 - lam_init)
        o_ref[:, blk] = (y * g_ref[:, blk].astype(F32)).astype(BF16)

    _attention_maps(maps, qi, qm_scr, k_ref, v_ref, ((ta_scr, mxa_scr), (tb_scr, mxb_scr)),
                    mb_scr, acc_scr, ls_scr, emit)


def _attn_scratch(seq, mxu_row_sums):
    stat = pltpu.VMEM((PAIR_ROWS, LANES), F32)
    logits = pltpu.VMEM((seq // TK, PAIR_ROWS, TK), F32)
    common = [pltpu.VMEM((WIDTH // LANES, PAIR_ROWS, LANES), BF16),
              logits, logits, stat, stat,
              stat]
    if mxu_row_sums:
        return common + [pltpu.VMEM((PAIR_ROWS, 2 * LANES), F32)]
    return common + [stat, stat]


def _per_sequence(block_shape, col):
    return pl.BlockSpec(block_shape, lambda b, i: (b, col))


def _fox_attention(pa, ps, crow, ccol, batch, seq):
    nq = seq // TQ
    return pl.pallas_call(
        _fox_kernel,
        grid=(batch, nq),
        in_specs=[
            pl.BlockSpec((TQ, WIDTH), lambda b, i: (b * nq + i, A_FQ)),
            _per_sequence((seq, WIDTH), A_FK),
            _per_sequence((seq, WIDTH), A_FV),
            pl.BlockSpec((TQ, WIDTH), lambda b, i: (b * nq + i, S_FG)),
            pl.BlockSpec((1, seq // TK, FOX_HEADS, TK), lambda b, i: (b, 0, 0, 0)),
            pl.BlockSpec((TQ, LANES), lambda b, i: (b * nq + i, 0)),
        ],
        out_specs=pl.BlockSpec((TQ, WIDTH), lambda b, i: (b * nq + i, 0)),
        out_shape=jax.ShapeDtypeStruct((batch * seq, WIDTH), BF16),
        scratch_shapes=_attn_scratch(seq, mxu_row_sums=True),
        compiler_params=pltpu.CompilerParams(
            dimension_semantics=("arbitrary", "arbitrary"), vmem_limit_bytes=ATTN_VMEM_LIMIT),
        name="fox_attention",
    )(pa, pa, pa, ps, crow, ccol)


def _diff_attention(pr, pa, ps, lam_vecs, norm_g, lam_init, batch, seq):
    nq = seq // TQ
    return pl.pallas_call(
        functools.partial(_diff_kernel, lam_init=lam_init),
        grid=(batch, nq),
        in_specs=[
            pl.BlockSpec((TQ, WIDTH), lambda b, i: (b * nq + i, R_DQ)),
            _per_sequence((seq, WIDTH), R_DK),
            _per_sequence((seq, WIDTH), A_DV),
            pl.BlockSpec((TQ, WIDTH), lambda b, i: (b * nq + i, S_DG)),
            pl.BlockSpec((4, HEAD_DIM), lambda b, i: (0, 0)),
            pl.BlockSpec((1, LANES), lambda b, i: (0, 0)),
        ],
        out_specs=pl.BlockSpec((TQ, WIDTH), lambda b, i: (b * nq + i, 0)),
        out_shape=jax.ShapeDtypeStruct((batch * seq, WIDTH), BF16),
        scratch_shapes=_attn_scratch(seq, mxu_row_sums=False),
        compiler_params=pltpu.CompilerParams(
            dimension_semantics=("arbitrary", "arbitrary"), vmem_limit_bytes=ATTN_VMEM_LIMIT),
        name="diff_attention",
    )(pr, pr, pa, ps, lam_vecs, norm_g)


def _merge_kernel(x_ref, yf_ref, yd_ref, cb_ref, cc_ref, cx_ref, cg_ref, hc_ref, hx_ref,
                  g0_ref, g1_ref, g2_ref, cw_ref, wf_ref, wd_ref, wc_ref, wo_ref, pg_ref,
                  o_ref, ext_scr, *, tiles_per_seq):
    i = pl.program_id(0)
    halo = SUBLANES
    u = cc_ref[...].astype(F32) * cx_ref[...].astype(F32)
    prev = hc_ref[...].astype(F32) * hx_ref[...].astype(F32)
    prev = jnp.where(i % tiles_per_seq == 0, jnp.zeros_like(prev), prev)
    ext_scr[0:halo, :] = prev
    ext_scr[halo:halo + TM_OUT, :] = u
    cw = cw_ref[...]
    conv = (cw[2:3] * u + cw[1:2] * ext_scr[halo - 1:halo - 1 + TM_OUT, :]
            + cw[0:1] * ext_scr[halo - 2:halo - 2 + TM_OUT, :])
    y_conv = (cb_ref[...].astype(F32) * conv * cg_ref[...].astype(F32)).astype(BF16)

    m = (g0_ref[...].astype(F32) * jnp.dot(yf_ref[...], wf_ref[...], preferred_element_type=F32)
         + g1_ref[...].astype(F32) * jnp.dot(yd_ref[...], wd_ref[...], preferred_element_type=F32)
         + g2_ref[...].astype(F32) * jnp.dot(y_conv, wc_ref[...], preferred_element_type=F32))
    o = jnp.dot(m.astype(BF16), wo_ref[...], preferred_element_type=F32)
    ms = jnp.mean(o * o, axis=-1, keepdims=True)
    o_ref[...] = x_ref[...] + o * lax.rsqrt(ms + RMS_EPS) * pg_ref[...]


def _merge(x2, y_fox, y_diff, pa, ps, pg, conv_w, w_fox, w_diff, w_conv, w_out, post_g, seq):
    t = x2.shape[0]
    tiles_per_seq = seq // TM_OUT
    halo_blocks = TM_OUT // SUBLANES

    def tile(col):
        return pl.BlockSpec((TM_OUT, WIDTH), lambda i: (i, col))

    def halo(col):
        return pl.BlockSpec((SUBLANES, WIDTH),
                            lambda i: (jnp.maximum(i * halo_blocks - 1, 0), col))

    def whole(shape):
        return pl.BlockSpec(shape, lambda i: (0, 0), pipeline_mode=pl.Buffered(1))

    return pl.pallas_call(
        functools.partial(_merge_kernel, tiles_per_seq=tiles_per_seq),
        grid=(t // TM_OUT,),
        in_specs=[
            pl.BlockSpec((TM_OUT, D_MODEL), lambda i: (i, 0)),
            pl.BlockSpec((TM_OUT, WIDTH), lambda i: (i, 0)),
            pl.BlockSpec((TM_OUT, WIDTH), lambda i: (i, 0)),
            tile(A_CB), tile(A_CC), tile(A_CX), tile(S_CG),
            halo(A_CC), halo(A_CX),
            pl.BlockSpec((TM_OUT, D_MODEL), lambda i: (i, 0)),
            pl.BlockSpec((TM_OUT, D_MODEL), lambda i: (i, 1)),
            pl.BlockSpec((TM_OUT, D_MODEL), lambda i: (i, 2)),
            whole((CONV_K, WIDTH)),
            whole((WIDTH, D_MODEL)), whole((WIDTH, D_MODEL)), whole((WIDTH, D_MODEL)),
            whole((D_MODEL, D_MODEL)),
            whole((1, D_MODEL)),
        ],
        out_specs=pl.BlockSpec((TM_OUT, D_MODEL), lambda i: (i, 0)),
        out_shape=jax.ShapeDtypeStruct((t, D_MODEL), F32),
        scratch_shapes=[pltpu.VMEM((TM_OUT + SUBLANES, WIDTH), F32)],
        compiler_params=pltpu.CompilerParams(
            dimension_semantics=("arbitrary",), vmem_limit_bytes=VMEM_LIMIT),
        name="merge",
    )(x2, y_fox, y_diff, pa, pa, pa, ps, pa, pa, pg, pg, pg,
      conv_w, w_fox, w_diff, w_conv, w_out, post_g)


def _rope_lane_tables(positions):
    inv_freq = ROPE_THETA ** (-jnp.arange(0, ROT_DIM, 2, dtype=F32) / ROT_DIM)
    ang = positions.astype(F32).reshape(-1, 1) * inv_freq
    cos, sin = jnp.cos(ang), jnp.sin(ang)
    t = ang.shape[0]
    ones = jnp.ones((t, HEAD_DIM - ROT_DIM), F32)
    zeros = jnp.zeros((t, HEAD_DIM - ROT_DIM), F32)
    c64 = jnp.concatenate([cos, cos, ones], axis=1)
    s64 = jnp.concatenate([-sin, sin, zeros], axis=1)
    rep = LANES // HEAD_DIM
    return jnp.tile(c64, (1, rep)), jnp.tile(s64, (1, rep))


def _gather_cols(w, offsets):
    return jnp.concatenate([w[:, o:o + WIDTH] for o in offsets], axis=1).astype(BF16)


def _scale_vec(n_blocks, scaled_block):
    blk = jnp.arange(n_blocks * WIDTH, dtype=jnp.int32) // WIDTH
    q_scale = HEAD_DIM ** -0.5 * LOG2E
    return jnp.where(blk == scaled_block, q_scale, 1.0).astype(F32).reshape(1, -1)


def kernel(x, positions, pre_norm_g, w_in, b_forget, b_merge, conv_w, lam_q1, lam_k1, lam_q2,
           lam_k2, diff_norm_g, w_br_fox, w_br_diff, w_br_conv, w_out, post_norm_g):
    batch, seq, _ = x.shape
    depth = w_in.shape[0]
    assert seq % TQ == 0 and TQ == TK == TS_FF and (batch * seq) % TM_IN == 0
    rc, rs = _rope_lane_tables(positions)
    x2 = x.reshape(batch * seq, D_MODEL)
    for l in range(depth):
        lam_init = 0.8 - 0.6 * math.exp(-0.3 * l)
        w = w_in[l]
        w_plain = _gather_cols(w, [_OFF_FQ, _OFF_FK, _OFF_FV, _OFF_DV, _OFF_CB, _OFF_CC, _OFF_CX])
        w_silu = _gather_cols(w, [_OFF_FG, _OFF_DG, _OFF_CG])
        w_rot = _gather_cols(w, [_OFF_DQ, _OFF_DK])
        w_gate = w[:, _OFF_MG:_OFF_MG + N_BRANCH * D_MODEL].astype(BF16)
        w_ff_t = w[:, _OFF_FF:_OFF_FF + FOX_HEADS].T.astype(BF16)
        pa, h = _plain_proj(x2, pre_norm_g[l].reshape(1, D_MODEL), w_plain,
                            _scale_vec(7, A_FQ), TN_PLAIN)
        ps = _epilogue_proj(_silu_proj_kernel, "proj_silu", h, w_silu, TN_SILU, [], [])
        pr = _epilogue_proj(_rot_proj_kernel, "proj_rotary", h, w_rot, TN_ROT,
                            [_scale_vec(2, R_DQ), rc, rs],
                            [_col_vec(TN_ROT), _ROPE_TILE, _ROPE_TILE])
        pg = _epilogue_proj(_gate_proj_kernel, "proj_gate", h, w_gate, TN_GATE,
                            [b_merge[l].reshape(1, N_BRANCH * D_MODEL)], [_col_vec(TN_GATE)])
        crow, ccol = _forget_cumsum(h, w_ff_t, b_forget[l].reshape(FOX_HEADS, 1), batch, seq)
        y_fox = _fox_attention(pa, ps, crow, ccol, batch, seq)
        lam_vecs = jnp.stack([lam_q1[l], lam_k1[l], lam_q2[l], lam_k2[l]])
        y_diff = _diff_attention(pr, pa, ps, lam_vecs, diff_norm_g[l].reshape(1, LANES),
                                 lam_init, batch, seq)
        x2 = _merge(x2, y_fox, y_diff, pa, ps, pg, conv_w[l],
                    w_br_fox[l].astype(BF16), w_br_diff[l].astype(BF16),
                    w_br_conv[l].astype(BF16), w_out[l].astype(BF16),
                    post_norm_g[l].reshape(1, D_MODEL), seq)
    return x2.reshape(batch, seq, D_MODEL)
```

```python
import functools
import math
from typing import Any, Callable, NamedTuple, Optional

import jax
import jax.numpy as jnp
from jax import lax
from jax.experimental import pallas as pl
from jax.experimental.pallas import tpu as pltpu

F32 = jnp.float32
BF16 = jnp.bfloat16

D_MODEL = 1024
HEAD_DIM = 64
FOX_HEADS = 8
DIFF_HEADS = 4
WIDTH = 512
CONV_K = 3
N_BRANCH = 3
ROPE_THETA = 500000.0
ROT_DIM = HEAD_DIM // 4
RMS_EPS = 1e-6
LANES = 128
SUBLANES = 8
LOG2E = math.log2(math.e)

(A_FQ, A_FK, A_FV, A_DV, A_CB, A_CC, A_CX) = range(7)
(S_FG, S_DG, S_CG) = range(3)
(R_DQ, R_DK) = range(2)
N_GATE_BLOCKS = N_BRANCH * D_MODEL // WIDTH

_OFF_FQ, _OFF_FK, _OFF_FV = 0, 512, 1024
_OFF_FF = 1536
_OFF_FG = 1544
_OFF_DQ, _OFF_DK, _OFF_DV, _OFF_DG = 2056, 2568, 3080, 3592
_OFF_CB, _OFF_CC, _OFF_CX, _OFF_CG = 4104, 4616, 5128, 5640
_OFF_MG = 6152

TM_IN = 1024
TN_PLAIN = 1792
TN_SILU = 1536
TN_ROT = 1024
TN_GATE = 3072
TS_FF = 512
TQ = 512
TK = 512
TM_OUT = 1024
PAIR_ROWS = 2 * TQ
ATTN_VMEM_LIMIT = 58 * 1024 * 1024
VMEM_LIMIT = 48 * 1024 * 1024


def _silu(v):
    return v * jax.nn.sigmoid(v)


def _rotary_block(acc, c, s):
    half = ROT_DIM // 2
    lane = lax.broadcasted_iota(jnp.int32, (acc.shape[0], LANES), 1)
    takes_upper = lax.bitwise_and(lane, ROT_DIM - 1) < half
    outs = []
    for i in range(acc.shape[1] // LANES):
        v = acc[:, i * LANES:(i + 1) * LANES]
        partner = jnp.where(takes_upper, pltpu.roll(v, LANES - half, 1), pltpu.roll(v, half, 1))
        outs.append(v * c + partner * s)
    return jnp.concatenate(outs, axis=1)


def _plain_proj_kernel(x_ref, g_ref, w_ref, sc_ref, o_ref, h_ref, h_scr):
    @pl.when(pl.program_id(1) == 0)
    def _():
        x = x_ref[...]
        ms = jnp.mean(x * x, axis=-1, keepdims=True)
        h = (x * lax.rsqrt(ms + RMS_EPS) * g_ref[...]).astype(BF16)
        h_scr[...] = h
        h_ref[...] = h

    acc = jnp.dot(h_scr[...], w_ref[...], preferred_element_type=F32)
    o_ref[...] = (acc * sc_ref[...]).astype(BF16)


def _silu_proj_kernel(h_ref, w_ref, o_ref):
    acc = jnp.dot(h_ref[...], w_ref[...], preferred_element_type=F32)
    o_ref[...] = _silu(acc).astype(BF16)


def _rot_proj_kernel(h_ref, w_ref, sc_ref, rc_ref, rs_ref, o_ref):
    acc = jnp.dot(h_ref[...], w_ref[...], preferred_element_type=F32)
    r = _rotary_block(acc, rc_ref[...], rs_ref[...])
    o_ref[...] = (r * sc_ref[...]).astype(BF16)


def _gate_proj_kernel(h_ref, w_ref, b_ref, o_ref):
    acc = jnp.dot(h_ref[...], w_ref[...], preferred_element_type=F32)
    o_ref[...] = jax.nn.sigmoid(acc + b_ref[...]).astype(BF16)


def _proj_params(n_inputs, weight_index):
    fusable = [i == weight_index for i in range(n_inputs)]
    return pltpu.CompilerParams(
        dimension_semantics=("arbitrary", "arbitrary"), vmem_limit_bytes=VMEM_LIMIT,
        allow_input_fusion=fusable)
_ROW_TILE = pl.BlockSpec((TM_IN, D_MODEL), lambda i, j: (i, 0))
_ROPE_TILE = pl.BlockSpec((TM_IN, LANES), lambda i, j: (i, 0))


def _w_block(tn):
    return pl.BlockSpec((D_MODEL, tn), lambda i, j: (0, j))


def _col_vec(tn):
    return pl.BlockSpec((1, tn), lambda i, j: (0, j))


def _out_block(tn):
    return pl.BlockSpec((TM_IN, tn), lambda i, j: (i, j))


def _plain_proj(x2, pre_g, w, col_scale, tn):
    t, n = x2.shape[0], w.shape[1]
    return pl.pallas_call(
        _plain_proj_kernel,
        grid=(t // TM_IN, n // tn),
        in_specs=[_ROW_TILE, pl.BlockSpec((1, D_MODEL), lambda i, j: (0, 0)),
                  _w_block(tn), _col_vec(tn)],
        out_specs=[_out_block(tn), _ROW_TILE],
        out_shape=[jax.ShapeDtypeStruct((t, n), BF16), jax.ShapeDtypeStruct((t, D_MODEL), BF16)],
        scratch_shapes=[pltpu.VMEM((TM_IN, D_MODEL), BF16)],
        compiler_params=_proj_params(4, 2),
        name="proj_plain",
    )(x2, pre_g, w, col_scale)


def _epilogue_proj(body, name, h, w, tn, extra, extra_specs):
    t, n = h.shape[0], w.shape[1]
    return pl.pallas_call(
        body,
        grid=(t // TM_IN, n // tn),
        in_specs=[_ROW_TILE, _w_block(tn)] + extra_specs,
        out_specs=_out_block(tn),
        out_shape=jax.ShapeDtypeStruct((t, n), BF16),
        compiler_params=_proj_params(2 + len(extra), 1),
        name=name,
    )(h, w, *extra)


def _split3(v):
    hi = v.astype(BF16)
    r1 = v - hi.astype(F32)
    mid = r1.astype(BF16)
    lo = (r1 - mid.astype(F32)).astype(BF16)
    return hi, mid, lo


def _forget_kernel(h_ref, wt_ref, bf_ref, crow_ref, ccol_ref, carry_scr):
    si = pl.program_id(1)

    @pl.when(si == 0)
    def _():
        carry_scr[...] = jnp.zeros_like(carry_scr)

    ff = lax.dot_general(wt_ref[...], h_ref[...], (((1,), (1,)), ((), ())),
                         preferred_element_type=F32)
    lf = jax.nn.log_sigmoid(ff + bf_ref[...])
    row = lax.broadcasted_iota(jnp.int32, (TS_FF, TS_FF), 0)
    col = lax.broadcasted_iota(jnp.int32, (TS_FF, TS_FF), 1)
    upper = (row <= col).astype(BF16)
    cum = jnp.zeros((FOX_HEADS, TS_FF), F32)
    for piece in _split3(lf):
        cum = cum + jnp.dot(piece, upper, preferred_element_type=F32)
    c = cum + carry_scr[:, 0:1]
    carry_scr[...] = jnp.broadcast_to(c[:, TS_FF - 1:TS_FF], carry_scr.shape)
    c2 = c * LOG2E
    crow_ref[0, 0] = c2
    padded = jnp.concatenate([c2, jnp.zeros((LANES - FOX_HEADS, TS_FF), F32)], axis=0)
    ccol_ref[...] = padded.T


def _forget_cumsum(h, w_ff_t, b_f, batch, seq):
    nt = seq // TS_FF
    return pl.pallas_call(
        _forget_kernel,
        grid=(batch, nt),
        in_specs=[
            pl.BlockSpec((TS_FF, D_MODEL), lambda b, s: (b * nt + s, 0)),
            pl.BlockSpec((FOX_HEADS, D_MODEL), lambda b, s: (0, 0)),
            pl.BlockSpec((FOX_HEADS, 1), lambda b, s: (0, 0)),
        ],
        out_specs=[
            pl.BlockSpec((1, 1, FOX_HEADS, TS_FF), lambda b, s: (b, s, 0, 0)),
            pl.BlockSpec((TS_FF, LANES), lambda b, s: (b * nt + s, 0)),
        ],
        out_shape=[
            jax.ShapeDtypeStruct((batch, nt, FOX_HEADS, TS_FF), F32),
            jax.ShapeDtypeStruct((batch * seq, LANES), F32),
        ],
        scratch_shapes=[pltpu.VMEM((FOX_HEADS, LANES), F32)],
        compiler_params=pltpu.CompilerParams(
            dimension_semantics=("arbitrary", "arbitrary"), vmem_limit_bytes=VMEM_LIMIT),
        name="forget_cumsum",
    )(h, w_ff_t, b_f)


class _MapPair(NamedTuple):
    blk: int
    key_bias: Optional[Callable[[Any], Any]]
    row_shift: Any


def _attention_maps(maps, qi, qm_scr, k_ref, v_ref, bufs, mb_scr, acc_scr, ls_scr, emit):
    mxu_row_sums = ls_scr is None
    row = lax.broadcasted_iota(jnp.int32, (TQ, TK), 0)
    col = lax.broadcasted_iota(jnp.int32, (TQ, TK), 1)
    causal = col <= row
    n_lane_blocks = TK // LANES

    def lanes(v, c):
        return v[:, c * LANES:(c + 1) * LANES]

    def chunk_rows(ref, mp, ki):
        start = pl.multiple_of(ki * TK, TK)
        return ref[pl.ds(start, TK), mp.blk * LANES:(mp.blk + 1) * LANES]

    def pass1_begin(buf):
        bufs[buf][1][...] = jnp.full((PAIR_ROWS, LANES), -jnp.inf, F32)

    def pass1_chunk(mp, buf, ki, diagonal):
        t_scr, mx_scr = bufs[buf]
        s2 = lax.dot_general(qm_scr[mp.blk], chunk_rows(k_ref, mp, ki),
                             (((1,), (1,)), ((), ())), preferred_element_type=F32)
        biases = mp.key_bias(ki) if mp.key_bias is not None else (None, None)
        for half, bias in enumerate(biases):
            rows = slice(half * TQ, (half + 1) * TQ)
            s = s2[rows, :]
            if bias is not None:
                s = s - bias
            if diagonal:
                s = jnp.where(causal, s, -jnp.inf)
            t_scr[ki, rows, :] = s
            m = mx_scr[rows, :]
            for c in range(n_lane_blocks):
                m = jnp.maximum(m, lanes(s, c))
            mx_scr[rows, :] = m

    def pass2_begin(mp, buf):
        m = jnp.max(bufs[buf][1][...], axis=1, keepdims=True)
        if mp.row_shift is not None:
            m = (m + mp.row_shift) - mp.row_shift
        mb_scr[...] = jnp.broadcast_to(m, (PAIR_ROWS, LANES))
        acc_scr[...] = jnp.zeros(acc_scr.shape, F32)
        if not mxu_row_sums:
            ls_scr[...] = jnp.zeros((PAIR_ROWS, LANES), F32)

    def pass2_chunk(mp, buf, ki):
        t = bufs[buf][0][ki]
        mb = mb_scr[...]
        ps = [jnp.exp2(lanes(t, c) - mb) for c in range(n_lane_blocks)]
        v = chunk_rows(v_ref, mp, ki)
        if mxu_row_sums:
            v = jnp.concatenate([v, jnp.ones((TK, LANES), BF16)], axis=1)
        else:
            ls = ls_scr[...]
            for p in ps:
                ls = ls + p
            ls_scr[...] = ls
        p = jnp.concatenate(ps, axis=1).astype(BF16)
        acc_scr[...] += jnp.dot(p, v, preferred_element_type=F32)

    def pass2_end(n):
        acc = acc_scr[...]
        if mxu_row_sums:
            l = acc[:, LANES:LANES + 1]
        else:
            l = jnp.sum(ls_scr[...], axis=1, keepdims=True)
        emit(n, acc[:, 0:LANES] / l)

    def loop(n_chunks, body):
        def step(i, carry):
            body(2 * i)
            body(2 * i + 1)
            return carry
        lax.fori_loop(0, lax.shift_right_logical(n_chunks, 1), step, 0)

        @pl.when(lax.bitwise_and(n_chunks, 1) == 1)
        def _():
            body(n_chunks - 1)

    pass1_begin(0)
    pass1_chunk(maps[0], 0, qi, True)
    loop(qi, lambda ki: pass1_chunk(maps[0], 0, ki, False))
    for n in range(1, len(maps)):
        buf, prev = n % 2, (n - 1) % 2
        pass2_begin(maps[n - 1], prev)
        pass1_begin(buf)
        pass1_chunk(maps[n], buf, qi, True)

        def both(ki, n=n, buf=buf, prev=prev):
            pass2_chunk(maps[n - 1], prev, ki)
            pass1_chunk(maps[n], buf, ki, False)

        loop(qi, both)
        pass2_chunk(maps[n - 1], prev, qi)
        pass2_end(n - 1)
    last = len(maps) - 1
    pass2_begin(maps[last], last % 2)
    loop(qi + 1, lambda ki: pass2_chunk(maps[last], last % 2, ki))
    pass2_end(last)


def _store_masked_queries(q_ref, qm_scr):
    first = lax.broadcasted_iota(jnp.int32, (TQ, LANES), 1) < HEAD_DIM
    for b in range(WIDTH // LANES):
        q = q_ref[:, b * LANES:(b + 1) * LANES]
        zero = jnp.zeros_like(q)
        qm_scr[b, 0:TQ, :] = jnp.where(first, q, zero)
        qm_scr[b, TQ:PAIR_ROWS, :] = jnp.where(first, zero, q)


def _fox_kernel(q_ref, k_ref, v_ref, g_ref, crow_ref, ccol_ref, o_ref,
                qm_scr, ta_scr, tb_scr, mxa_scr, mxb_scr, mb_scr, acc_scr):
    qi = pl.program_id(1)
    first = lax.broadcasted_iota(jnp.int32, (TQ, LANES), 1) < HEAD_DIM
    _store_masked_queries(q_ref, qm_scr)
    ccol = ccol_ref[...]
    maps = [
        _MapPair(blk=b,
                 key_bias=lambda ki, b=b: (crow_ref[0, ki, 2 * b:2 * b + 1, :],
                                           crow_ref[0, ki, 2 * b + 1:2 * b + 2, :]),
                 row_shift=jnp.concatenate([ccol[:, 2 * b:2 * b + 1],
                                            ccol[:, 2 * b + 1:2 * b + 2]], axis=0))
        for b in range(FOX_HEADS // 2)
    ]

    def emit(b, o):
        blk = slice(b * LANES, (b + 1) * LANES)
        y = jnp.where(first, o[0:TQ, :], o[TQ:PAIR_ROWS, :]) * g_ref[:, blk].astype(F32)
        o_ref[:, blk] = y.astype(BF16)

    _attention_maps(maps, qi, qm_scr, k_ref, v_ref, ((ta_scr, mxa_scr), (tb_scr, mxb_scr)),
                    mb_scr, acc_scr, None, emit)


def _diff_kernel(q_ref, k_ref, v_ref, g_ref, lam_ref, ng_ref, o_ref,
                 qm_scr, ta_scr, tb_scr, mxa_scr, mxb_scr, mb_scr, acc_scr, ls_scr,
                 *, lam_init):
    qi = pl.program_id(1)
    _store_masked_queries(q_ref, qm_scr)
    lv = lam_ref[...]
    lam = (jnp.exp(jnp.sum(lv[0:1] * lv[1:2], axis=1, keepdims=True))
           - jnp.exp(jnp.sum(lv[2:3] * lv[3:4], axis=1, keepdims=True)) + lam_init)
    maps = [_MapPair(blk=h, key_bias=None, row_shift=None) for h in range(DIFF_HEADS)]

    def emit(h, o):
        blk = slice(h * LANES, (h + 1) * LANES)
        d = o[0:TQ, :] - lam * o[TQ:PAIR_ROWS, :]
        ms = jnp.mean(d * d, axis=-1, keepdims=True)
        y = d * lax.rsqrt(ms + RMS_EPS) * ng_ref[...] * (1.0 - lam_init)
        o_ref[:, blk] = (y * g_ref[:, blk].astype(F32)).astype(BF16)

    _attention_maps(maps, qi, qm_scr, k_ref, v_ref, ((ta_scr, mxa_scr), (tb_scr, mxb_scr)),
                    mb_scr, acc_scr, ls_scr, emit)


def _attn_scratch(seq, mxu_row_sums):
    stat = pltpu.VMEM((PAIR_ROWS, LANES), F32)
    logits = pltpu.VMEM((seq // TK, PAIR_ROWS, TK), F32)
    common = [pltpu.VMEM((WIDTH // LANES, PAIR_ROWS, LANES), BF16),
              logits, logits, stat, stat,
              stat]
    if mxu_row_sums:
        return common + [pltpu.VMEM((PAIR_ROWS, 2 * LANES), F32)]
    return common + [stat, stat]


def _per_sequence(block_shape, col):
    return pl.BlockSpec(block_shape, lambda b, i: (b, col))


def _fox_attention(pa, ps, crow, ccol, batch, seq):
    nq = seq // TQ
    return pl.pallas_call(
        _fox_kernel,
        grid=(batch, nq),
        in_specs=[
            pl.BlockSpec((TQ, WIDTH), lambda b, i: (b * nq + i, A_FQ)),
            _per_sequence((seq, WIDTH), A_FK),
            _per_sequence((seq, WIDTH), A_FV),
            pl.BlockSpec((TQ, WIDTH), lambda b, i: (b * nq + i, S_FG)),
            pl.BlockSpec((1, seq // TK, FOX_HEADS, TK), lambda b, i: (b, 0, 0, 0)),
            pl.BlockSpec((TQ, LANES), lambda b, i: (b * nq + i, 0)),
        ],
        out_specs=pl.BlockSpec((TQ, WIDTH), lambda b, i: (b * nq + i, 0)),
        out_shape=jax.ShapeDtypeStruct((batch * seq, WIDTH), BF16),
        scratch_shapes=_attn_scratch(seq, mxu_row_sums=True),
        compiler_params=pltpu.CompilerParams(
            dimension_semantics=("arbitrary", "arbitrary"), vmem_limit_bytes=ATTN_VMEM_LIMIT),
        name="fox_attention",
    )(pa, pa, pa, ps, crow, ccol)


def _diff_attention(pr, pa, ps, lam_vecs, norm_g, lam_init, batch, seq):
    nq = seq // TQ
    return pl.pallas_call(
        functools.partial(_diff_kernel, lam_init=lam_init),
        grid=(batch, nq),
        in_specs=[
            pl.BlockSpec((TQ, WIDTH), lambda b, i: (b * nq + i, R_DQ)),
            _per_sequence((seq, WIDTH), R_DK),
            _per_sequence((seq, WIDTH), A_DV),
            pl.BlockSpec((TQ, WIDTH), lambda b, i: (b * nq + i, S_DG)),
            pl.BlockSpec((4, HEAD_DIM), lambda b, i: (0, 0)),
            pl.BlockSpec((1, LANES), lambda b, i: (0, 0)),
        ],
        out_specs=pl.BlockSpec((TQ, WIDTH), lambda b, i: (b * nq + i, 0)),
        out_shape=jax.ShapeDtypeStruct((batch * seq, WIDTH), BF16),
        scratch_shapes=_attn_scratch(seq, mxu_row_sums=False),
        compiler_params=pltpu.CompilerParams(
            dimension_semantics=("arbitrary", "arbitrary"), vmem_limit_bytes=ATTN_VMEM_LIMIT),
        name="diff_attention",
    )(pr, pr, pa, ps, lam_vecs, norm_g)


def _merge_kernel(x_ref, yf_ref, yd_ref, cb_ref, cc_ref, cx_ref, cg_ref, hc_ref, hx_ref,
                  g0_ref, g1_ref, g2_ref, cw_ref, wf_ref, wd_ref, wc_ref, wo_ref, pg_ref,
                  o_ref, ext_scr, *, tiles_per_seq):
    i = pl.program_id(0)
    halo = SUBLANES
    u = cc_ref[...].astype(F32) * cx_ref[...].astype(F32)
    prev = hc_ref[...].astype(F32) * hx_ref[...].astype(F32)
    prev = jnp.where(i % tiles_per_seq == 0, jnp.zeros_like(prev), prev)
    ext_scr[0:halo, :] = prev
    ext_scr[halo:halo + TM_OUT, :] = u
    cw = cw_ref[...]
    conv = (cw[2:3] * u + cw[1:2] * ext_scr[halo - 1:halo - 1 + TM_OUT, :]
            + cw[0:1] * ext_scr[halo - 2:halo - 2 + TM_OUT, :])
    y_conv = (cb_ref[...].astype(F32) * conv * cg_ref[...].astype(F32)).astype(BF16)

    m = (g0_ref[...].astype(F32) * jnp.dot(yf_ref[...], wf_ref[...], preferred_element_type=F32)
         + g1_ref[...].astype(F32) * jnp.dot(yd_ref[...], wd_ref[...], preferred_element_type=F32)
         + g2_ref[...].astype(F32) * jnp.dot(y_conv, wc_ref[...], preferred_element_type=F32))
    o = jnp.dot(m.astype(BF16), wo_ref[...], preferred_element_type=F32)
    ms = jnp.mean(o * o, axis=-1, keepdims=True)
    o_ref[...] = x_ref[...] + o * lax.rsqrt(ms + RMS_EPS) * pg_ref[...]


def _merge(x2, y_fox, y_diff, pa, ps, pg, conv_w, w_fox, w_diff, w_conv, w_out, post_g, seq):
    t = x2.shape[0]
    tiles_per_seq = seq // TM_OUT
    halo_blocks = TM_OUT // SUBLANES

    def tile(col):
        return pl.BlockSpec((TM_OUT, WIDTH), lambda i: (i, col))

    def halo(col):
        return pl.BlockSpec((SUBLANES, WIDTH),
                            lambda i: (jnp.maximum(i * halo_blocks - 1, 0), col))

    def whole(shape):
        return pl.BlockSpec(shape, lambda i: (0, 0), pipeline_mode=pl.Buffered(1))

    return pl.pallas_call(
        functools.partial(_merge_kernel, tiles_per_seq=tiles_per_seq),
        grid=(t // TM_OUT,),
        in_specs=[
            pl.BlockSpec((TM_OUT, D_MODEL), lambda i: (i, 0)),
            pl.BlockSpec((TM_OUT, WIDTH), lambda i: (i, 0)),
            pl.BlockSpec((TM_OUT, WIDTH), lambda i: (i, 0)),
            tile(A_CB), tile(A_CC), tile(A_CX), tile(S_CG),
            halo(A_CC), halo(A_CX),
            pl.BlockSpec((TM_OUT, D_MODEL), lambda i: (i, 0)),
            pl.BlockSpec((TM_OUT, D_MODEL), lambda i: (i, 1)),
            pl.BlockSpec((TM_OUT, D_MODEL), lambda i: (i, 2)),
            whole((CONV_K, WIDTH)),
            whole((WIDTH, D_MODEL)), whole((WIDTH, D_MODEL)), whole((WIDTH, D_MODEL)),
            whole((D_MODEL, D_MODEL)),
            whole((1, D_MODEL)),
        ],
        out_specs=pl.BlockSpec((TM_OUT, D_MODEL), lambda i: (i, 0)),
        out_shape=jax.ShapeDtypeStruct((t, D_MODEL), F32),
        scratch_shapes=[pltpu.VMEM((TM_OUT + SUBLANES, WIDTH), F32)],
        compiler_params=pltpu.CompilerParams(
            dimension_semantics=("arbitrary",), vmem_limit_bytes=VMEM_LIMIT,
            allow_input_fusion=[13 <= i <= 16 for i in range(18)]),
        name="merge",
    )(x2, y_fox, y_diff, pa, pa, pa, ps, pa, pa, pg, pg, pg,
      conv_w, w_fox, w_diff, w_conv, w_out, post_g)


def _rope_lane_tables(positions):
    inv_freq = ROPE_THETA ** (-jnp.arange(0, ROT_DIM, 2, dtype=F32) / ROT_DIM)
    ang = positions.astype(F32).reshape(-1, 1) * inv_freq
    cos, sin = jnp.cos(ang), jnp.sin(ang)
    t = ang.shape[0]
    ones = jnp.ones((t, HEAD_DIM - ROT_DIM), F32)
    zeros = jnp.zeros((t, HEAD_DIM - ROT_DIM), F32)
    c64 = jnp.concatenate([cos, cos, ones], axis=1)
    s64 = jnp.concatenate([-sin, sin, zeros], axis=1)
    rep = LANES // HEAD_DIM
    return jnp.tile(c64, (1, rep)), jnp.tile(s64, (1, rep))


def _gather_cols(w, offsets):
    return jnp.concatenate([w[:, o:o + WIDTH] for o in offsets], axis=1).astype(BF16)


def _scale_vec(n_blocks, scaled_block):
    blk = jnp.arange(n_blocks * WIDTH, dtype=jnp.int32) // WIDTH
    q_scale = HEAD_DIM ** -0.5 * LOG2E
    return jnp.where(blk == scaled_block, q_scale, 1.0).astype(F32).reshape(1, -1)


def kernel(x, positions, pre_norm_g, w_in, b_forget, b_merge, conv_w, lam_q1, lam_k1, lam_q2,
           lam_k2, diff_norm_g, w_br_fox, w_br_diff, w_br_conv, w_out, post_norm_g):
    batch, seq, _ = x.shape
    depth = w_in.shape[0]
    assert seq % TQ == 0 and TQ == TK == TS_FF and (batch * seq) % TM_IN == 0
    rc, rs = _rope_lane_tables(positions)
    x2 = x.reshape(batch * seq, D_MODEL)
    for l in range(depth):
        lam_init = 0.8 - 0.6 * math.exp(-0.3 * l)
        w = w_in[l]
        w_plain = _gather_cols(w, [_OFF_FQ, _OFF_FK, _OFF_FV, _OFF_DV, _OFF_CB, _OFF_CC, _OFF_CX])
        w_silu = _gather_cols(w, [_OFF_FG, _OFF_DG, _OFF_CG])
        w_rot = _gather_cols(w, [_OFF_DQ, _OFF_DK])
        w_gate = w[:, _OFF_MG:_OFF_MG + N_BRANCH * D_MODEL].astype(BF16)
        w_ff_t = w[:, _OFF_FF:_OFF_FF + FOX_HEADS].T.astype(BF16)
        pa, h = _plain_proj(x2, pre_norm_g[l].reshape(1, D_MODEL), w_plain,
                            _scale_vec(7, A_FQ), TN_PLAIN)
        ps = _epilogue_proj(_silu_proj_kernel, "proj_silu", h, w_silu, TN_SILU, [], [])
        pr = _epilogue_proj(_rot_proj_kernel, "proj_rotary", h, w_rot, TN_ROT,
                            [_scale_vec(2, R_DQ), rc, rs],
                            [_col_vec(TN_ROT), _ROPE_TILE, _ROPE_TILE])
        pg = _epilogue_proj(_gate_proj_kernel, "proj_gate", h, w_gate, TN_GATE,
                            [b_merge[l].reshape(1, N_BRANCH * D_MODEL)], [_col_vec(TN_GATE)])
        crow, ccol = _forget_cumsum(h, w_ff_t, b_forget[l].reshape(FOX_HEADS, 1), batch, seq)
        y_fox = _fox_attention(pa, ps, crow, ccol, batch, seq)
        lam_vecs = jnp.stack([lam_q1[l], lam_k1[l], lam_q2[l], lam_k2[l]])
        y_diff = _diff_attention(pr, pa, ps, lam_vecs, diff_norm_g[l].reshape(1, LANES),
                                 lam_init, batch, seq)
        x2 = _merge(x2, y_fox, y_diff, pa, ps, pg, conv_w[l],
                    w_br_fox[l].astype(BF16), w_br_diff[l].astype(BF16),
                    w_br_conv[l].astype(BF16), w_out[l].astype(BF16),
                    post_norm_g[l].reshape(1, D_MODEL), seq)
    return x2.reshape(batch, seq, D_MODEL)
```

```python
import functools
import math
from typing import Any, Callable, NamedTuple, Optional

import jax
import jax.numpy as jnp
from jax import lax
from jax.experimental import pallas as pl
from jax.experimental.pallas import tpu as pltpu

F32 = jnp.float32
BF16 = jnp.bfloat16

D_MODEL = 1024
HEAD_DIM = 64
FOX_HEADS = 8
DIFF_HEADS = 4
WIDTH = 512
CONV_K = 3
N_BRANCH = 3
ROPE_THETA = 500000.0
ROT_DIM = HEAD_DIM // 4
RMS_EPS = 1e-6
LANES = 128
SUBLANES = 8
LOG2E = math.log2(math.e)

(A_FQ, A_FK, A_FV, A_DV, A_CB, A_CC, A_CX) = range(7)
(S_FG, S_DG, S_CG) = range(3)
(R_DQ, R_DK) = range(2)
N_GATE_BLOCKS = N_BRANCH * D_MODEL // WIDTH

_OFF_FQ, _OFF_FK, _OFF_FV = 0, 512, 1024
_OFF_FF = 1536
_OFF_FG = 1544
_OFF_DQ, _OFF_DK, _OFF_DV, _OFF_DG = 2056, 2568, 3080, 3592
_OFF_CB, _OFF_CC, _OFF_CX, _OFF_CG = 4104, 4616, 5128, 5640
_OFF_MG = 6152

TM_IN = 1024
TN_PLAIN = 1792
TN_SILU = 1536
TN_ROT = 1024
TN_GATE = 3072
TS_FF = 512
FF_CHUNKS_PER_STEP = 4
TQ = 512
TK = 512
TM_OUT = 1024
PAIR_ROWS = 2 * TQ
ATTN_VMEM_LIMIT = 58 * 1024 * 1024
VMEM_LIMIT = 48 * 1024 * 1024


def _silu(v):
    return v * jax.nn.sigmoid(v)


def _rotary_block(acc, c, s):
    half = ROT_DIM // 2
    lane = lax.broadcasted_iota(jnp.int32, (acc.shape[0], LANES), 1)
    takes_upper = lax.bitwise_and(lane, ROT_DIM - 1) < half
    outs = []
    for i in range(acc.shape[1] // LANES):
        v = acc[:, i * LANES:(i + 1) * LANES]
        partner = jnp.where(takes_upper, pltpu.roll(v, LANES - half, 1), pltpu.roll(v, half, 1))
        outs.append(v * c + partner * s)
    return jnp.concatenate(outs, axis=1)


def _plain_proj_kernel(x_ref, g_ref, w_ref, sc_ref, o_ref, h_ref, h_scr):
    @pl.when(pl.program_id(1) == 0)
    def _():
        x = x_ref[...]
        ms = jnp.mean(x * x, axis=-1, keepdims=True)
        h = (x * lax.rsqrt(ms + RMS_EPS) * g_ref[...]).astype(BF16)
        h_scr[...] = h
        h_ref[...] = h

    acc = jnp.dot(h_scr[...], w_ref[...], preferred_element_type=F32)
    o_ref[...] = (acc * sc_ref[...]).astype(BF16)


def _silu_proj_kernel(h_ref, w_ref, o_ref):
    acc = jnp.dot(h_ref[...], w_ref[...], preferred_element_type=F32)
    o_ref[...] = _silu(acc).astype(BF16)


def _rot_proj_kernel(h_ref, w_ref, sc_ref, rc_ref, rs_ref, o_ref):
    acc = jnp.dot(h_ref[...], w_ref[...], preferred_element_type=F32)
    r = _rotary_block(acc, rc_ref[...], rs_ref[...])
    o_ref[...] = (r * sc_ref[...]).astype(BF16)


def _gate_proj_kernel(h_ref, w_ref, b_ref, o_ref):
    acc = jnp.dot(h_ref[...], w_ref[...], preferred_element_type=F32)
    o_ref[...] = jax.nn.sigmoid(acc + b_ref[...]).astype(BF16)


_PROJ_PARAMS = pltpu.CompilerParams(
    dimension_semantics=("arbitrary", "arbitrary"), vmem_limit_bytes=VMEM_LIMIT)
_ROW_TILE = pl.BlockSpec((TM_IN, D_MODEL), lambda i, j: (i, 0))
_ROPE_TILE = pl.BlockSpec((TM_IN, LANES), lambda i, j: (i, 0))


def _w_block(tn):
    return pl.BlockSpec((D_MODEL, tn), lambda i, j: (0, j))


def _col_vec(tn):
    return pl.BlockSpec((1, tn), lambda i, j: (0, j))


def _out_block(tn):
    return pl.BlockSpec((TM_IN, tn), lambda i, j: (i, j))


def _plain_proj(x2, pre_g, w, col_scale, tn):
    t, n = x2.shape[0], w.shape[1]
    return pl.pallas_call(
        _plain_proj_kernel,
        grid=(t // TM_IN, n // tn),
        in_specs=[_ROW_TILE, pl.BlockSpec((1, D_MODEL), lambda i, j: (0, 0)),
                  _w_block(tn), _col_vec(tn)],
        out_specs=[_out_block(tn), _ROW_TILE],
        out_shape=[jax.ShapeDtypeStruct((t, n), BF16), jax.ShapeDtypeStruct((t, D_MODEL), BF16)],
        scratch_shapes=[pltpu.VMEM((TM_IN, D_MODEL), BF16)],
        compiler_params=_PROJ_PARAMS,
        name="proj_plain",
    )(x2, pre_g, w, col_scale)


def _epilogue_proj(body, name, h, w, tn, extra, extra_specs):
    t, n = h.shape[0], w.shape[1]
    return pl.pallas_call(
        body,
        grid=(t // TM_IN, n // tn),
        in_specs=[_ROW_TILE, _w_block(tn)] + extra_specs,
        out_specs=_out_block(tn),
        out_shape=jax.ShapeDtypeStruct((t, n), BF16),
        compiler_params=_PROJ_PARAMS,
        name=name,
    )(h, w, *extra)


def _split3(v):
    hi = v.astype(BF16)
    r1 = v - hi.astype(F32)
    mid = r1.astype(BF16)
    lo = (r1 - mid.astype(F32)).astype(BF16)
    return hi, mid, lo


def _forget_kernel(h_ref, wt_ref, bf_ref, crow_ref, ccol_ref, carry_scr):
    si = pl.program_id(1)

    @pl.when(si == 0)
    def _():
        carry_scr[...] = jnp.zeros_like(carry_scr)

    row = lax.broadcasted_iota(jnp.int32, (TS_FF, TS_FF), 0)
    col = lax.broadcasted_iota(jnp.int32, (TS_FF, TS_FF), 1)
    upper = (row <= col).astype(BF16)
    for sub in range(FF_CHUNKS_PER_STEP):
        rows = slice(sub * TS_FF, (sub + 1) * TS_FF)
        ff = lax.dot_general(wt_ref[...], h_ref[rows, :], (((1,), (1,)), ((), ())),
                             preferred_element_type=F32)
        lf = jax.nn.log_sigmoid(ff + bf_ref[...])
        cum = jnp.zeros((FOX_HEADS, TS_FF), F32)
        for piece in _split3(lf):
            cum = cum + jnp.dot(piece, upper, preferred_element_type=F32)
        c = cum + carry_scr[:, 0:1]
        carry_scr[...] = jnp.broadcast_to(c[:, TS_FF - 1:TS_FF], carry_scr.shape)
        c2 = c * LOG2E
        crow_ref[0, sub] = c2
        padded = jnp.concatenate([c2, jnp.zeros((LANES - FOX_HEADS, TS_FF), F32)], axis=0)
        ccol_ref[rows, :] = padded.T


def _forget_cumsum(h, w_ff_t, b_f, batch, seq):
    nt = seq // TS_FF
    step_rows = FF_CHUNKS_PER_STEP * TS_FF
    ns = seq // step_rows
    return pl.pallas_call(
        _forget_kernel,
        grid=(batch, ns),
        in_specs=[
            pl.BlockSpec((step_rows, D_MODEL), lambda b, s: (b * ns + s, 0)),
            pl.BlockSpec((FOX_HEADS, D_MODEL), lambda b, s: (0, 0)),
            pl.BlockSpec((FOX_HEADS, 1), lambda b, s: (0, 0)),
        ],
        out_specs=[
            pl.BlockSpec((1, FF_CHUNKS_PER_STEP, FOX_HEADS, TS_FF), lambda b, s: (b, s, 0, 0)),
            pl.BlockSpec((step_rows, LANES), lambda b, s: (b * ns + s, 0)),
        ],
        out_shape=[
            jax.ShapeDtypeStruct((batch, nt, FOX_HEADS, TS_FF), F32),
            jax.ShapeDtypeStruct((batch * seq, LANES), F32),
        ],
        scratch_shapes=[pltpu.VMEM((FOX_HEADS, LANES), F32)],
        compiler_params=pltpu.CompilerParams(
            dimension_semantics=("arbitrary", "arbitrary"), vmem_limit_bytes=VMEM_LIMIT),
        name="forget_cumsum",
    )(h, w_ff_t, b_f)


class _MapPair(NamedTuple):
    blk: int
    key_bias: Optional[Callable[[Any], Any]]
    row_shift: Any


def _attention_maps(maps, qi, qm_scr, k_ref, v_ref, bufs, mb_scr, acc_scr, ls_scr, emit):
    mxu_row_sums = ls_scr is None
    row = lax.broadcasted_iota(jnp.int32, (TQ, TK), 0)
    col = lax.broadcasted_iota(jnp.int32, (TQ, TK), 1)
    causal = col <= row
    n_lane_blocks = TK // LANES

    def lanes(v, c):
        return v[:, c * LANES:(c + 1) * LANES]

    def chunk_rows(ref, mp, ki):
        start = pl.multiple_of(ki * TK, TK)
        return ref[pl.ds(start, TK), mp.blk * LANES:(mp.blk + 1) * LANES]

    def pass1_begin(buf):
        bufs[buf][1][...] = jnp.full((PAIR_ROWS, LANES), -jnp.inf, F32)

    def pass1_chunk(mp, buf, ki, diagonal):
        t_scr, mx_scr = bufs[buf]
        s2 = lax.dot_general(qm_scr[mp.blk], chunk_rows(k_ref, mp, ki),
                             (((1,), (1,)), ((), ())), preferred_element_type=F32)
        biases = mp.key_bias(ki) if mp.key_bias is not None else (None, None)
        for half, bias in enumerate(biases):
            rows = slice(half * TQ, (half + 1) * TQ)
            s = s2[rows, :]
            if bias is not None:
                s = s - bias
            if diagonal:
                s = jnp.where(causal, s, -jnp.inf)
            t_scr[ki, rows, :] = s
            m = mx_scr[rows, :]
            for c in range(n_lane_blocks):
                m = jnp.maximum(m, lanes(s, c))
            mx_scr[rows, :] = m

    def pass2_begin(mp, buf):
        m = jnp.max(bufs[buf][1][...], axis=1, keepdims=True)
        if mp.row_shift is not None:
            m = (m + mp.row_shift) - mp.row_shift
        mb_scr[...] = jnp.broadcast_to(m, (PAIR_ROWS, LANES))
        acc_scr[...] = jnp.zeros(acc_scr.shape, F32)
        if not mxu_row_sums:
            ls_scr[...] = jnp.zeros((PAIR_ROWS, LANES), F32)

    def pass2_chunk(mp, buf, ki):
        t = bufs[buf][0][ki]
        mb = mb_scr[...]
        ps = [jnp.exp2(lanes(t, c) - mb) for c in range(n_lane_blocks)]
        v = chunk_rows(v_ref, mp, ki)
        if mxu_row_sums:
            v = jnp.concatenate([v, jnp.ones((TK, LANES), BF16)], axis=1)
        else:
            ls = ls_scr[...]
            for p in ps:
                ls = ls + p
            ls_scr[...] = ls
        p = jnp.concatenate(ps, axis=1).astype(BF16)
        acc_scr[...] += jnp.dot(p, v, preferred_element_type=F32)

    def pass2_end(n):
        acc = acc_scr[...]
        if mxu_row_sums:
            l = acc[:, LANES:LANES + 1]
        else:
            l = jnp.sum(ls_scr[...], axis=1, keepdims=True)
        emit(n, acc[:, 0:LANES] / l)

    def loop(n_chunks, body):
        def step(i, carry):
            body(2 * i)
            body(2 * i + 1)
            return carry
        lax.fori_loop(0, lax.shift_right_logical(n_chunks, 1), step, 0)

        @pl.when(lax.bitwise_and(n_chunks, 1) == 1)
        def _():
            body(n_chunks - 1)

    pass1_begin(0)
    pass1_chunk(maps[0], 0, qi, True)
    loop(qi, lambda ki: pass1_chunk(maps[0], 0, ki, False))
    for n in range(1, len(maps)):
        buf, prev = n % 2, (n - 1) % 2
        pass2_begin(maps[n - 1], prev)
        pass1_begin(buf)
        pass1_chunk(maps[n], buf, qi, True)

        def both(ki, n=n, buf=buf, prev=prev):
            pass2_chunk(maps[n - 1], prev, ki)
            pass1_chunk(maps[n], buf, ki, False)

        loop(qi, both)
        pass2_chunk(maps[n - 1], prev, qi)
        pass2_end(n - 1)
    last = len(maps) - 1
    pass2_begin(maps[last], last % 2)
    loop(qi + 1, lambda ki: pass2_chunk(maps[last], last % 2, ki))
    pass2_end(last)


def _store_masked_queries(q_ref, qm_scr):
    first = lax.broadcasted_iota(jnp.int32, (TQ, LANES), 1) < HEAD_DIM
    for b in range(WIDTH // LANES):
        q = q_ref[:, b * LANES:(b + 1) * LANES]
        zero = jnp.zeros_like(q)
        qm_scr[b, 0:TQ, :] = jnp.where(first, q, zero)
        qm_scr[b, TQ:PAIR_ROWS, :] = jnp.where(first, zero, q)


def _fox_kernel(q_ref, k_ref, v_ref, g_ref, crow_ref, ccol_ref, o_ref,
                qm_scr, ta_scr, tb_scr, mxa_scr, mxb_scr, mb_scr, acc_scr):
    qi = pl.program_id(1)
    first = lax.broadcasted_iota(jnp.int32, (TQ, LANES), 1) < HEAD_DIM
    _store_masked_queries(q_ref, qm_scr)
    ccol = ccol_ref[...]
    maps = [
        _MapPair(blk=b,
                 key_bias=lambda ki, b=b: (crow_ref[0, ki, 2 * b:2 * b + 1, :],
                                           crow_ref[0, ki, 2 * b + 1:2 * b + 2, :]),
                 row_shift=jnp.concatenate([ccol[:, 2 * b:2 * b + 1],
                                            ccol[:, 2 * b + 1:2 * b + 2]], axis=0))
        for b in range(FOX_HEADS // 2)
    ]

    def emit(b, o):
        blk = slice(b * LANES, (b + 1) * LANES)
        y = jnp.where(first, o[0:TQ, :], o[TQ:PAIR_ROWS, :]) * g_ref[:, blk].astype(F32)
        o_ref[:, blk] = y.astype(BF16)

    _attention_maps(maps, qi, qm_scr, k_ref, v_ref, ((ta_scr, mxa_scr), (tb_scr, mxb_scr)),
                    mb_scr, acc_scr, None, emit)


def _diff_kernel(q_ref, k_ref, v_ref, g_ref, lam_ref, ng_ref, o_ref,
                 qm_scr, ta_scr, tb_scr, mxa_scr, mxb_scr, mb_scr, acc_scr, ls_scr,
                 *, lam_init):
    qi = pl.program_id(1)
    _store_masked_queries(q_ref, qm_scr)
    lv = lam_ref[...]
    lam = (jnp.exp(jnp.sum(lv[0:1] * lv[1:2], axis=1, keepdims=True))
           - jnp.exp(jnp.sum(lv[2:3] * lv[3:4], axis=1, keepdims=True)) + lam_init)
    maps = [_MapPair(blk=h, key_bias=None, row_shift=None) for h in range(DIFF_HEADS)]

    def emit(h, o):
        blk = slice(h * LANES, (h + 1) * LANES)
        d = o[0:TQ, :] - lam * o[TQ:PAIR_ROWS, :]
        ms = jnp.mean(d * d, axis=-1, keepdims=True)
        y = d * lax.rsqrt(ms + RMS_EPS) * ng_ref[...] * (1.0 - lam_init)
        o_ref[:, blk] = (y * g_ref[:, blk].astype(F32)).astype(BF16)

    _attention_maps(maps, qi, qm_scr, k_ref, v_ref, ((ta_scr, mxa_scr), (tb_scr, mxb_scr)),
                    mb_scr, acc_scr, ls_scr, emit)


def _attn_scratch(seq, mxu_row_sums):
    stat = pltpu.VMEM((PAIR_ROWS, LANES), F32)
    logits = pltpu.VMEM((seq // TK, PAIR_ROWS, TK), F32)
    common = [pltpu.VMEM((WIDTH // LANES, PAIR_ROWS, LANES), BF16),
              logits, logits, stat, stat,
              stat]
    if mxu_row_sums:
        return common + [pltpu.VMEM((PAIR_ROWS, 2 * LANES), F32)]
    return common + [stat, stat]


def _per_sequence(block_shape, col):
    return pl.BlockSpec(block_shape, lambda b, i: (b, col))


def _fox_attention(pa, ps, crow, ccol, batch, seq):
    nq = seq // TQ
    return pl.pallas_call(
        _fox_kernel,
        grid=(batch, nq),
        in_specs=[
            pl.BlockSpec((TQ, WIDTH), lambda b, i: (b * nq + i, A_FQ)),
            _per_sequence((seq, WIDTH), A_FK),
            _per_sequence((seq, WIDTH), A_FV),
            pl.BlockSpec((TQ, WIDTH), lambda b, i: (b * nq + i, S_FG)),
            pl.BlockSpec((1, seq // TK, FOX_HEADS, TK), lambda b, i: (b, 0, 0, 0)),
            pl.BlockSpec((TQ, LANES), lambda b, i: (b * nq + i, 0)),
        ],
        out_specs=pl.BlockSpec((TQ, WIDTH), lambda b, i: (b * nq + i, 0)),
        out_shape=jax.ShapeDtypeStruct((batch * seq, WIDTH), BF16),
        scratch_shapes=_attn_scratch(seq, mxu_row_sums=True),
        compiler_params=pltpu.CompilerParams(
            dimension_semantics=("arbitrary", "arbitrary"), vmem_limit_bytes=ATTN_VMEM_LIMIT),
        name="fox_attention",
    )(pa, pa, pa, ps, crow, ccol)


def _diff_attention(pr, pa, ps, lam_vecs, norm_g, lam_init, batch, seq):
    nq = seq // TQ
    return pl.pallas_call(
        functools.partial(_diff_kernel, lam_init=lam_init),
        grid=(batch, nq),
        in_specs=[
            pl.BlockSpec((TQ, WIDTH), lambda b, i: (b * nq + i, R_DQ)),
            _per_sequence((seq, WIDTH), R_DK),
            _per_sequence((seq, WIDTH), A_DV),
            pl.BlockSpec((TQ, WIDTH), lambda b, i: (b * nq + i, S_DG)),
            pl.BlockSpec((4, HEAD_DIM), lambda b, i: (0, 0)),
            pl.BlockSpec((1, LANES), lambda b, i: (0, 0)),
        ],
        out_specs=pl.BlockSpec((TQ, WIDTH), lambda b, i: (b * nq + i, 0)),
        out_shape=jax.ShapeDtypeStruct((batch * seq, WIDTH), BF16),
        scratch_shapes=_attn_scratch(seq, mxu_row_sums=False),
        compiler_params=pltpu.CompilerParams(
            dimension_semantics=("arbitrary", "arbitrary"), vmem_limit_bytes=ATTN_VMEM_LIMIT),
        name="diff_attention",
    )(pr, pr, pa, ps, lam_vecs, norm_g)


def _merge_kernel(x_ref, yf_ref, yd_ref, cb_ref, cc_ref, cx_ref, cg_ref, hc_ref, hx_ref,
                  g0_ref, g1_ref, g2_ref, cw_ref, wf_ref, wd_ref, wc_ref, wo_ref, pg_ref,
                  o_ref, ext_scr, *, tiles_per_seq):
    i = pl.program_id(0)
    halo = SUBLANES
    u = cc_ref[...].astype(F32) * cx_ref[...].astype(F32)
    prev = hc_ref[...].astype(F32) * hx_ref[...].astype(F32)
    prev = jnp.where(i % tiles_per_seq == 0, jnp.zeros_like(prev), prev)
    ext_scr[0:halo, :] = prev
    ext_scr[halo:halo + TM_OUT, :] = u
    cw = cw_ref[...]
    conv = (cw[2:3] * u + cw[1:2] * ext_scr[halo - 1:halo - 1 + TM_OUT, :]
            + cw[0:1] * ext_scr[halo - 2:halo - 2 + TM_OUT, :])
    y_conv = (cb_ref[...].astype(F32) * conv * cg_ref[...].astype(F32)).astype(BF16)

    m = (g0_ref[...].astype(F32) * jnp.dot(yf_ref[...], wf_ref[...], preferred_element_type=F32)
         + g1_ref[...].astype(F32) * jnp.dot(yd_ref[...], wd_ref[...], preferred_element_type=F32)
         + g2_ref[...].astype(F32) * jnp.dot(y_conv, wc_ref[...], preferred_element_type=F32))
    o = jnp.dot(m.astype(BF16), wo_ref[...], preferred_element_type=F32)
    ms = jnp.mean(o * o, axis=-1, keepdims=True)
    o_ref[...] = x_ref[...] + o * lax.rsqrt(ms + RMS_EPS) * pg_ref[...]


def _merge(x2, y_fox, y_diff, pa, ps, pg, conv_w, w_fox, w_diff, w_conv, w_out, post_g, seq):
    t = x2.shape[0]
    tiles_per_seq = seq // TM_OUT
    halo_blocks = TM_OUT // SUBLANES

    def tile(col):
        return pl.BlockSpec((TM_OUT, WIDTH), lambda i: (i, col))

    def halo(col):
        return pl.BlockSpec((SUBLANES, WIDTH),
                            lambda i: (jnp.maximum(i * halo_blocks - 1, 0), col))

    def whole(shape):
        return pl.BlockSpec(shape, lambda i: (0, 0), pipeline_mode=pl.Buffered(1))

    return pl.pallas_call(
        functools.partial(_merge_kernel, tiles_per_seq=tiles_per_seq),
        grid=(t // TM_OUT,),
        in_specs=[
            pl.BlockSpec((TM_OUT, D_MODEL), lambda i: (i, 0)),
            pl.BlockSpec((TM_OUT, WIDTH), lambda i: (i, 0)),
            pl.BlockSpec((TM_OUT, WIDTH), lambda i: (i, 0)),
            tile(A_CB), tile(A_CC), tile(A_CX), tile(S_CG),
            halo(A_CC), halo(A_CX),
            pl.BlockSpec((TM_OUT, D_MODEL), lambda i: (i, 0)),
            pl.BlockSpec((TM_OUT, D_MODEL), lambda i: (i, 1)),
            pl.BlockSpec((TM_OUT, D_MODEL), lambda i: (i, 2)),
            whole((CONV_K, WIDTH)),
            whole((WIDTH, D_MODEL)), whole((WIDTH, D_MODEL)), whole((WIDTH, D_MODEL)),
            whole((D_MODEL, D_MODEL)),
            whole((1, D_MODEL)),
        ],
        out_specs=pl.BlockSpec((TM_OUT, D_MODEL), lambda i: (i, 0)),
        out_shape=jax.ShapeDtypeStruct((t, D_MODEL), F32),
        scratch_shapes=[pltpu.VMEM((TM_OUT + SUBLANES, WIDTH), F32)],
        compiler_params=pltpu.CompilerParams(
            dimension_semantics=("arbitrary",), vmem_limit_bytes=VMEM_LIMIT),
        name="merge",
    )(x2, y_fox, y_diff, pa, pa, pa, ps, pa, pa, pg, pg, pg,
      conv_w, w_fox, w_diff, w_conv, w_out, post_g)


def _rope_lane_tables(positions):
    inv_freq = ROPE_THETA ** (-jnp.arange(0, ROT_DIM, 2, dtype=F32) / ROT_DIM)
    ang = positions.astype(F32).reshape(-1, 1) * inv_freq
    cos, sin = jnp.cos(ang), jnp.sin(ang)
    t = ang.shape[0]
    ones = jnp.ones((t, HEAD_DIM - ROT_DIM), F32)
    zeros = jnp.zeros((t, HEAD_DIM - ROT_DIM), F32)
    c64 = jnp.concatenate([cos, cos, ones], axis=1)
    s64 = jnp.concatenate([-sin, sin, zeros], axis=1)
    rep = LANES // HEAD_DIM
    return jnp.tile(c64, (1, rep)), jnp.tile(s64, (1, rep))


def _gather_cols(w, offsets):
    return jnp.concatenate([w[:, o:o + WIDTH] for o in offsets], axis=1).astype(BF16)


def _scale_vec(n_blocks, scaled_block):
    blk = jnp.arange(n_blocks * WIDTH, dtype=jnp.int32) // WIDTH
    q_scale = HEAD_DIM ** -0.5 * LOG2E
    return jnp.where(blk == scaled_block, q_scale, 1.0).astype(F32).reshape(1, -1)


def kernel(x, positions, pre_norm_g, w_in, b_forget, b_merge, conv_w, lam_q1, lam_k1, lam_q2,
           lam_k2, diff_norm_g, w_br_fox, w_br_diff, w_br_conv, w_out, post_norm_g):
    batch, seq, _ = x.shape
    depth = w_in.shape[0]
    assert seq % TQ == 0 and TQ == TK == TS_FF and (batch * seq) % TM_IN == 0
    rc, rs = _rope_lane_tables(positions)
    x2 = x.reshape(batch * seq, D_MODEL)
    for l in range(depth):
        lam_init = 0.8 - 0.6 * math.exp(-0.3 * l)
        w = w_in[l]
        w_plain = _gather_cols(w, [_OFF_FQ, _OFF_FK, _OFF_FV, _OFF_DV, _OFF_CB, _OFF_CC, _OFF_CX])
        w_silu = _gather_cols(w, [_OFF_FG, _OFF_DG, _OFF_CG])
        w_rot = _gather_cols(w, [_OFF_DQ, _OFF_DK])
        w_gate = w[:, _OFF_MG:_OFF_MG + N_BRANCH * D_MODEL].astype(BF16)
        w_ff_t = w[:, _OFF_FF:_OFF_FF + FOX_HEADS].T.astype(BF16)
        pa, h = _plain_proj(x2, pre_norm_g[l].reshape(1, D_MODEL), w_plain,
                            _scale_vec(7, A_FQ), TN_PLAIN)
        ps = _epilogue_proj(_silu_proj_kernel, "proj_silu", h, w_silu, TN_SILU, [], [])
        pr = _epilogue_proj(_rot_proj_kernel, "proj_rotary", h, w_rot, TN_ROT,
                            [_scale_vec(2, R_DQ), rc, rs],
                            [_col_vec(TN_ROT), _ROPE_TILE, _ROPE_TILE])
        pg = _epilogue_proj(_gate_proj_kernel, "proj_gate", h, w_gate, TN_GATE,
                            [b_merge[l].reshape(1, N_BRANCH * D_MODEL)], [_col_vec(TN_GATE)])
        crow, ccol = _forget_cumsum(h, w_ff_t, b_forget[l].reshape(FOX_HEADS, 1), batch, seq)
        y_fox = _fox_attention(pa, ps, crow, ccol, batch, seq)
        lam_vecs = jnp.stack([lam_q1[l], lam_k1[l], lam_q2[l], lam_k2[l]])
        y_diff = _diff_attention(pr, pa, ps, lam_vecs, diff_norm_g[l].reshape(1, LANES),
                                 lam_init, batch, seq)
        x2 = _merge(x2, y_fox, y_diff, pa, ps, pg, conv_w[l],
                    w_br_fox[l].astype(BF16), w_br_diff[l].astype(BF16),
                    w_br_conv[l].astype(BF16), w_out[l].astype(BF16),
                    post_norm_g[l].reshape(1, D_MODEL), seq)
    return x2.reshape(batch, seq, D_MODEL)
```
